```python
import math
import jax, jax.numpy as jnp
from jax import lax
import numpy as np

D_MODEL = 1024
BATCH = 8
SEQ = 4096
DEPTH = 4

CHUNK = 64
Q_BLOCK = 128
D_FF = 2816
NORM_EPS = 1e-6

A_HEADS = 4
A_QK_DIM = 64
A_V_DIM = 2 * A_QK_DIM
A_WIDTH = A_HEADS * A_V_DIM
A_COLS = 3 * A_WIDTH

B_HEAD_DIM = 64
B_WIDTH = D_MODEL - A_WIDTH
B_HEADS = B_WIDTH // B_HEAD_DIM
B_DECAY_RANK = 64
B_A_RANK = 64
B_GATE_RANK = 128
B_VRES_RANK = 32
B_LN_EPS = 64e-5
B_COLS = 3 * B_WIDTH + B_DECAY_RANK + B_A_RANK + B_GATE_RANK
B_SPLITS = [B_WIDTH, 2 * B_WIDTH, 3 * B_WIDTH, 3 * B_WIDTH + B_DECAY_RANK,
            3 * B_WIDTH + B_DECAY_RANK + B_A_RANK]
EVEN_IN = A_COLS + B_COLS

C_HEADS = 4
C_QK_DIM = D_MODEL // C_HEADS
C_V_DIM = 2 * C_QK_DIM
C_QK_WIDTH = C_HEADS * C_QK_DIM
C_V_WIDTH = C_HEADS * C_V_DIM
ODD_IN = 2 * C_QK_WIDTH + 2 * C_V_WIDTH

N_EVEN = (DEPTH + 1) // 2
N_ODD = DEPTH // 2

kernel_name = 'hybrid_diffattn_rwkv7_retention_macaron'


def rms_norm(x, g, eps=NORM_EPS):
    xf = x.astype(jnp.float32)
    y = xf * lax.rsqrt(jnp.mean(xf * xf, axis=-1, keepdims=True) + eps)
    return (y * g.astype(jnp.float32)).astype(x.dtype)


def swiglu(x, w_gu, w_d):
    gate, up = jnp.split(x @ w_gu, 2, axis=-1)
    return (jax.nn.silu(gate) * up) @ w_d


def token_shift(z, mu):
    z_prev = jnp.pad(z, ((0, 0), (1, 0), (0, 0)))[:, :-1]
    return z + (z_prev - z) * mu


def diff_attention(q, k, v, lam):
    S_ = q.shape[1]
    q = q * (A_QK_DIM ** -0.5)
    slopes = 2.0 ** (-8.0 / A_HEADS * jnp.arange(1, A_HEADS + 1, dtype=jnp.float32))
    pos = jnp.arange(S_)
    chunk_id = pos // CHUNK
    outs = []
    for s0 in range(0, S_, Q_BLOCK):
        kend = s0 + Q_BLOCK
        scores = jnp.einsum('bqhmd,bkhmd->bhmqk', q[:, s0:kend], k[:, :kend]).astype(jnp.float32)
        dist = jnp.abs(pos[s0:kend][:, None] - pos[:kend][None, :]).astype(jnp.float32)
        bias = -slopes[:, None, None, None] * dist
        allowed = chunk_id[:kend][None, :] <= chunk_id[s0:kend][:, None]
        scores = jnp.where(allowed, scores + bias, -jnp.inf)
        p = jax.nn.softmax(scores, axis=-1)
        p = p[:, :, 0] - lam * p[:, :, 1]
        outs.append(jnp.einsum('bhqk,bkhe->bqhe', p.astype(v.dtype), v[:, :kend]))
    return jnp.concatenate(outs, axis=1)


def rwkv7_recurrence(r, w, k, v, kk, a):
    Bn, _, H, N = r.shape

    def step(state, inp):
        r_t, w_t, k_t, v_t, kk_t, a_t = inp
        sa = jnp.einsum('bhvk,bhk->bhv', state, -kk_t)
        state = (state * w_t[:, :, None, :]
                 + sa[..., None] * (kk_t * a_t)[:, :, None, :]
                 + v_t[..., None] * k_t[:, :, None, :])
        return state, jnp.einsum('bhvk,bhk->bhv', state, r_t)

    xs = tuple(jnp.moveaxis(t, 1, 0) for t in (r, w, k, v, kk, a))
    state0 = jnp.zeros((Bn, H, N, N), jnp.float32)
    _, y = lax.scan(step, state0, xs)
    return jnp.moveaxis(y, 0, 1)


def head_group_norm(y, w, b):
    Bn, S_ = y.shape[:2]
    mean = jnp.mean(y, axis=-1, keepdims=True)
    var = jnp.mean(jnp.square(y - mean), axis=-1, keepdims=True)
    yn = (y - mean) * lax.rsqrt(var + B_LN_EPS)
    return yn.reshape(Bn, S_, -1) * w + b


def rwkv7_time_mix(zb, mu, vec, w_up, a_up, g_up, v_first, vres):
    Bn, S_, _ = zb.shape
    f32 = jnp.float32
    zs = token_shift(zb.astype(f32), mu.astype(f32))
    r, k, v, xw, xa, xg = jnp.split(zs, B_SPLITS, axis=-1)
    w0, a0, k_k, k_a, r_k, ln_w, ln_b = vec.astype(f32)
    w = -jax.nn.softplus(-(w0 + jnp.tanh(xw) @ w_up.astype(f32))) - 0.5
    decay = jnp.exp(-jnp.exp(w))
    if vres is not None:
        v0, v_down, v_up = vres
        v = v + (v_first - v) * jax.nn.sigmoid(v0.astype(f32) + (v @ v_down.astype(f32)) @ v_up.astype(f32))
    a = jax.nn.sigmoid(a0 + xa @ a_up.astype(f32))
    g = jax.nn.sigmoid(xg) @ g_up.astype(f32)
    heads = lambda t: t.reshape(Bn, S_, B_HEADS, B_HEAD_DIM)
    kk = heads(k * k_k)
    kk = kk / jnp.maximum(jnp.sqrt(jnp.sum(kk * kk, axis=-1, keepdims=True)), 1e-12)
    k = k * (1.0 + (a - 1.0) * k_a)
    rh, kh, vh = heads(r), heads(k), heads(v)
    y = rwkv7_recurrence(rh, heads(decay), kh, vh, kk, heads(a))
    y = head_group_norm(y, ln_w, ln_b)
    r_k_h = r_k.reshape(B_HEADS, B_HEAD_DIM)
    bonus = jnp.sum(rh * kh * r_k_h, axis=-1, keepdims=True) * vh
    y = y + bonus.reshape(Bn, S_, B_WIDTH)
    return y * g, v


def diff_rwkv_mixer(h, w_in, w_out, lam_vecs, subln, lam_init, mu, vec, w_up, a_up, g_up, v_first, vres):
    Bn, S_, _ = h.shape
    z = h @ w_in
    za, zb = z[..., :A_COLS], z[..., A_COLS:]
    qa, ka, va = jnp.split(za, 3, axis=-1)
    qa = qa.reshape(Bn, S_, A_HEADS, 2, A_QK_DIM)
    ka = ka.reshape(Bn, S_, A_HEADS, 2, A_QK_DIM)
    va = va.reshape(Bn, S_, A_HEADS, A_V_DIM)
    lv = lam_vecs.astype(jnp.float32)
    lam = jnp.exp(jnp.sum(lv[0] * lv[1])) - jnp.exp(jnp.sum(lv[2] * lv[3])) + lam_init
    oa = diff_attention(qa, ka, va, lam)
    oa = rms_norm(oa, subln) * (1.0 - lam_init)
    ob, v_b = rwkv7_time_mix(zb, mu, vec, w_up, a_up, g_up, v_first, vres)
    mix = jnp.concatenate([oa.reshape(Bn, S_, A_WIDTH), ob.astype(h.dtype)], axis=-1)
    return mix @ w_out, v_b


def retention(q, k, v):
    Bn, S_, H, _ = q.shape
    dk, dv = q.shape[-1], v.shape[-1]
    n_chunks = S_ // CHUNK
    log_g = jnp.log(1.0 - 2.0 ** (-5.0 - jnp.arange(H, dtype=jnp.float32)))
    idx = jnp.arange(CHUNK, dtype=jnp.float32)
    intra = jnp.exp(log_g[:, None, None] * jnp.abs(idx[:, None] - idx[None, :]))
    q_decay = jnp.exp(log_g[:, None] * (idx + 1.0))
    k_decay = jnp.exp(log_g[:, None] * (CHUNK - 1.0 - idx))
    chunk_decay = jnp.exp(log_g * CHUNK)

    def to_chunks(t):
        return t.reshape(Bn, n_chunks, CHUNK, H, t.shape[-1]).transpose(1, 0, 3, 2, 4)

    def step(state, inp):
        qc, kc, vc = inp
        scores = jnp.einsum('bhid,bhjd->bhij', qc, kc) * intra
        y = (jnp.einsum('bhij,bhjv->bhiv', scores, vc)
             + jnp.einsum('bhid,bhdv->bhiv', qc, state) * q_decay[:, :, None])
        state = (state * chunk_decay[:, None, None]
                 + jnp.einsum('bhjd,bhjv->bhdv', kc * k_decay[:, :, None], vc))
        return state, y

    state0 = jnp.zeros((Bn, H, dk, dv), jnp.float32)
    _, y = lax.scan(step, state0, (to_chunks(q), to_chunks(k), to_chunks(v)))
    return y.transpose(1, 0, 3, 2, 4).reshape(Bn, S_, H, dv)


def retention_mixer(h, w_in, w_out):
    Bn, S_, _ = h.shape
    z = (h @ w_in).astype(jnp.float32)
    q, k, v, g = jnp.split(z, [C_QK_WIDTH, 2 * C_QK_WIDTH, 2 * C_QK_WIDTH + C_V_WIDTH], axis=-1)
    q = q.reshape(Bn, S_, C_HEADS, C_QK_DIM)
    k = k.reshape(Bn, S_, C_HEADS, C_QK_DIM) * (C_QK_DIM ** -0.5)
    v = v.reshape(Bn, S_, C_HEADS, C_V_DIM)
    y = retention(q, k, v)
    y = y * lax.rsqrt(jnp.mean(y * y, axis=-1, keepdims=True) + NORM_EPS)
    y = jax.nn.silu(g) * y.reshape(Bn, S_, C_V_WIDTH)
    return y.astype(h.dtype) @ w_out


def setup_inputs(seed: int = 0) -> dict:
    key = jax.random.key(seed)
    ks = jax.random.split(key, 24)
    nrm = lambda k, shape, s: s * jax.random.normal(k, shape, jnp.float32)
    x = nrm(ks[0], (BATCH, SEQ, D_MODEL), 1.0)
    norms = 1.0 + nrm(ks[1], (DEPTH, 6, D_MODEL), 0.05)
    ffn_wgu = nrm(ks[2], (DEPTH, 2, D_MODEL, 2 * D_FF), D_MODEL ** -0.5)
    ffn_wd = nrm(ks[3], (DEPTH, 2, D_FF, D_MODEL), D_FF ** -0.5)
    even_w_in = nrm(ks[4], (N_EVEN, D_MODEL, EVEN_IN), D_MODEL ** -0.5)
    even_w_out = nrm(ks[5], (N_EVEN, D_MODEL, D_MODEL), D_MODEL ** -0.5)
    diff_lam = nrm(ks[6], (N_EVEN, 4, A_QK_DIM), 0.1)
    diff_subln = 1.0 + nrm(ks[7], (N_EVEN, A_V_DIM), 0.05)
    rwkv_mu = jax.random.uniform(ks[8], (N_EVEN, B_COLS), jnp.float32)
    vshape = (N_EVEN, B_WIDTH)
    w0 = jnp.linspace(-6.0, -1.0, B_WIDTH, dtype=jnp.float32)[None, :] + nrm(ks[9], vshape, 0.1)
    a0 = nrm(ks[10], vshape, 0.1)
    k_k = 0.85 + nrm(ks[11], vshape, 0.05)
    k_a = 1.0 + nrm(ks[12], vshape, 0.05)
    r_k = nrm(ks[13], vshape, 0.1)
    ln_w = 1.0 + nrm(ks[14], vshape, 0.05)
    ln_b = nrm(ks[15], vshape, 0.02)
    rwkv_vec = jnp.stack([w0, a0, k_k, k_a, r_k, ln_w, ln_b], axis=1)
    rwkv_w_up = nrm(ks[16], (N_EVEN, B_DECAY_RANK, B_WIDTH), 0.1 * B_DECAY_RANK ** -0.5)
    rwkv_a_up = nrm(ks[17], (N_EVEN, B_A_RANK, B_WIDTH), 0.5 * B_A_RANK ** -0.5)
    rwkv_g_up = nrm(ks[18], (N_EVEN, B_GATE_RANK, B_WIDTH), B_GATE_RANK ** -0.5)
    rwkv_v0 = nrm(ks[19], (N_EVEN - 1, B_WIDTH), 0.1)
    rwkv_v_down = nrm(ks[20], (N_EVEN - 1, B_WIDTH, B_VRES_RANK), B_WIDTH ** -0.5)
    rwkv_v_up = nrm(ks[21], (N_EVEN - 1, B_VRES_RANK, B_WIDTH), 0.5 * B_VRES_RANK ** -0.5)
    odd_w_in = nrm(ks[22], (N_ODD, D_MODEL, ODD_IN), D_MODEL ** -0.5)
    odd_w_out = nrm(ks[23], (N_ODD, C_V_WIDTH, D_MODEL), C_V_WIDTH ** -0.5)
    return {'x': x, 'norms': norms, 'ffn_wgu': ffn_wgu, 'ffn_wd': ffn_wd,
            'even_w_in': even_w_in, 'even_w_out': even_w_out,
            'diff_lam': diff_lam, 'diff_subln': diff_subln,
            'rwkv_mu': rwkv_mu, 'rwkv_vec': rwkv_vec,
            'rwkv_w_up': rwkv_w_up, 'rwkv_a_up': rwkv_a_up, 'rwkv_g_up': rwkv_g_up,
            'rwkv_v0': rwkv_v0, 'rwkv_v_down': rwkv_v_down, 'rwkv_v_up': rwkv_v_up,
            'odd_w_in': odd_w_in, 'odd_w_out': odd_w_out}


def reference(x, norms, ffn_wgu, ffn_wd, even_w_in, even_w_out, diff_lam, diff_subln,
              rwkv_mu, rwkv_vec, rwkv_w_up, rwkv_a_up, rwkv_g_up,
              rwkv_v0, rwkv_v_down, rwkv_v_up, odd_w_in, odd_w_out):
    v_first = None
    for i in range(DEPTH):
        g = norms[i]
        x = x + 0.5 * rms_norm(swiglu(rms_norm(x, g[0]), ffn_wgu[i, 0], ffn_wd[i, 0]), g[1])
        h = rms_norm(x, g[2])
        j = i // 2
        if i % 2 == 0:
            lam_init = 0.8 - 0.6 * math.exp(-0.3 * i)
            vres = None if j == 0 else (rwkv_v0[j - 1], rwkv_v_down[j - 1], rwkv_v_up[j - 1])
            mix, v_b = diff_rwkv_mixer(h, even_w_in[j], even_w_out[j], diff_lam[j], diff_subln[j],
                                       lam_init, rwkv_mu[j], rwkv_vec[j], rwkv_w_up[j],
                                       rwkv_a_up[j], rwkv_g_up[j], v_first, vres)
            if j == 0:
                v_first = v_b
        else:
            mix = retention_mixer(h, odd_w_in[j], odd_w_out[j])
        x = x + rms_norm(mix, g[3])
        x = x + 0.5 * rms_norm(swiglu(rms_norm(x, g[4]), ffn_wgu[i, 1], ffn_wd[i, 1]), g[5])
    return x
```

```python
import functools
import math

import jax
import jax.numpy as jnp
from jax import lax
from jax.experimental import pallas as pl
from jax.experimental.pallas import tpu as pltpu

F32 = jnp.float32
BF16 = jnp.bfloat16

D_MODEL = 1024
DEPTH = 4
CHUNK = 64
D_FF = 2816
NORM_EPS = 1e-6

A_HEADS = 4
A_QK_DIM = 64
A_V_DIM = 2 * A_QK_DIM
A_WIDTH = A_HEADS * A_V_DIM
A_COLS = 3 * A_WIDTH

B_HEAD_DIM = 64
B_WIDTH = D_MODEL - A_WIDTH
B_HEADS = B_WIDTH // B_HEAD_DIM
B_DECAY_RANK = 64
B_A_RANK = 64
B_GATE_RANK = 128
B_VRES_RANK = 32
B_LN_EPS = 64e-5
B_COLS = 3 * B_WIDTH + B_DECAY_RANK + B_A_RANK + B_GATE_RANK
EVEN_IN = A_COLS + B_COLS

C_HEADS = 4
C_QK_DIM = D_MODEL // C_HEADS
C_V_DIM = 2 * C_QK_DIM
C_QK_WIDTH = C_HEADS * C_QK_DIM
C_V_WIDTH = C_HEADS * C_V_DIM
ODD_IN = 2 * C_QK_WIDTH + 2 * C_V_WIDTH

LANES = 128
ROW_TILE = 512
FF_TILE = 256
SEQ_TILE = 256
VMEM_LIMIT = 52 * 1024 * 1024
NEG_BIG = -1e30


def _dot(a, b):
    return jnp.dot(a, b, preferred_element_type=F32)


def _dot_nt(a, b):
    return lax.dot_general(a, b, (((1,), (1,)), ((), ())), preferred_element_type=F32)


def _dot_tn(a, b):
    return lax.dot_general(a, b, (((0,), (0,)), ((), ())), preferred_element_type=F32)


def _split3(x):
    h1 = x.astype(BF16)
    r1 = x - h1.astype(F32)
    h2 = r1.astype(BF16)
    h3 = (r1 - h2.astype(F32)).astype(BF16)
    return h1, h2, h3


def _ones_matmul_right(x, m):
    h1, h2, h3 = _split3(x)
    return _dot(h1, m) + _dot(h2, m) + _dot(h3, m)


def _ones_matmul_left(m, x):
    h1, h2, h3 = _split3(x)
    return _dot(m, h1) + _dot(m, h2) + _dot(m, h3)


def _rms(x, g, eps=NORM_EPS):
    return x * lax.rsqrt(jnp.mean(x * x, axis=-1, keepdims=True) + eps) * g


def _sigmoid(x):
    return 1.0 / (1.0 + jnp.exp(-x))


def _row_tile(m):
    return ROW_TILE if m % ROW_TILE == 0 else m


def _seq_tile(s):
    return SEQ_TILE if s % SEQ_TILE == 0 else s


def _resident(block_shape, index_map):
    return pl.BlockSpec(block_shape, index_map, pipeline_mode=pl.Buffered(1))


def _ffn_kernel(x_ref, g_ref, wgu_ref, wd_ref, o_ref, act_ref, *, g_pre, g_post):
    x = x_ref[...]
    h = _rms(x, g_ref[g_pre:g_pre + 1, :]).astype(BF16)
    for j in range(D_FF // FF_TILE):
        lo = j * FF_TILE
        gate = _dot(h, wgu_ref[:, lo:lo + FF_TILE])
        up = _dot(h, wgu_ref[:, D_FF + lo:D_FF + lo + FF_TILE])
        act_ref[:, lo:lo + FF_TILE] = (gate * _sigmoid(gate) * up).astype(BF16)
    y = _dot(act_ref[...], wd_ref[...])
    o_ref[...] = x + 0.5 * _rms(y, g_ref[g_post:g_post + 1, :])


def _ffn(x2, norms, wgu, wd, layer, which):
    m, d = x2.shape
    tm = _row_tile(m)
    kern = functools.partial(_ffn_kernel, g_pre=4 * which, g_post=4 * which + 1)
    return pl.pallas_call(
        kern,
        out_shape=jax.ShapeDtypeStruct((m, d), F32),
        grid=(m // tm,),
        in_specs=[
            pl.BlockSpec((tm, d), lambda i: (i, 0)),
            _resident((None, 6, d), lambda i: (layer, 0, 0)),
            _resident((None, None, d, 2 * D_FF), lambda i: (layer, which, 0, 0)),
            _resident((None, None, D_FF, d), lambda i: (layer, which, 0, 0)),
        ],
        out_specs=pl.BlockSpec((tm, d), lambda i: (i, 0)),
        scratch_shapes=[pltpu.VMEM((tm, D_FF), BF16)],
        compiler_params=pltpu.CompilerParams(
            dimension_semantics=("parallel",), vmem_limit_bytes=VMEM_LIMIT),
        name="ffn",
    )(x2, norms, wgu, wd)


def _inproj_even_kernel(x_ref, g_ref, w_ref, za_ref, zb_ref):
    h = _rms(x_ref[...], g_ref[2:3, :]).astype(BF16)
    za_ref[...] = _dot(h, w_ref[:, :A_COLS]).astype(BF16)
    zb_ref[...] = _dot(h, w_ref[:, A_COLS:])


def _inproj_even(x2, norms, w_in, layer, j):
    m, d = x2.shape
    tm = _row_tile(m)
    return pl.pallas_call(
        _inproj_even_kernel,
        out_shape=(jax.ShapeDtypeStruct((m, A_COLS), BF16),
                   jax.ShapeDtypeStruct((m, B_COLS), F32)),
        grid=(m // tm,),
        in_specs=[
            pl.BlockSpec((tm, d), lambda i: (i, 0)),
            _resident((None, 6, d), lambda i: (layer, 0, 0)),
            _resident((None, d, EVEN_IN), lambda i: (j, 0, 0)),
        ],
        out_specs=(pl.BlockSpec((tm, A_COLS), lambda i: (i, 0)),
                   pl.BlockSpec((tm, B_COLS), lambda i: (i, 0))),
        compiler_params=pltpu.CompilerParams(
            dimension_semantics=("parallel",), vmem_limit_bytes=VMEM_LIMIT),
        name="inproj_even",
    )(x2, norms, w_in)


def _inproj_odd_kernel(x_ref, g_ref, w_ref, qk_ref, v_ref, gate_ref):
    h = _rms(x_ref[...], g_ref[2:3, :]).astype(BF16)
    qk_ref[...] = _dot(h, w_ref[:, :2 * C_QK_WIDTH]).astype(BF16)
    v_ref[...] = _dot(h, w_ref[:, 2 * C_QK_WIDTH:2 * C_QK_WIDTH + C_V_WIDTH]).astype(BF16)
    gate_ref[...] = _dot(h, w_ref[:, 2 * C_QK_WIDTH + C_V_WIDTH:])


def _inproj_odd(x2, norms, w_in, layer, j):
    m, d = x2.shape
    tm = _row_tile(m)
    return pl.pallas_call(
        _inproj_odd_kernel,
        out_shape=(jax.ShapeDtypeStruct((m, 2 * C_QK_WIDTH), BF16),
                   jax.ShapeDtypeStruct((m, C_V_WIDTH), BF16),
                   jax.ShapeDtypeStruct((m, C_V_WIDTH), F32)),
        grid=(m // tm,),
        in_specs=[
            pl.BlockSpec((tm, d), lambda i: (i, 0)),
            _resident((None, 6, d), lambda i: (layer, 0, 0)),
            _resident((None, d, ODD_IN), lambda i: (j, 0, 0)),
        ],
        out_specs=(pl.BlockSpec((tm, 2 * C_QK_WIDTH), lambda i: (i, 0)),
                   pl.BlockSpec((tm, C_V_WIDTH), lambda i: (i, 0)),
                   pl.BlockSpec((tm, C_V_WIDTH), lambda i: (i, 0))),
        compiler_params=pltpu.CompilerParams(
            dimension_semantics=("parallel",), vmem_limit_bytes=VMEM_LIMIT),
        name="inproj_odd",
    )(x2, norms, w_in)


def _outproj_kernel(*refs, n_parts):
    x_ref, g_ref = refs[0], refs[1]
    part_refs = refs[2:2 + n_parts]
    w_ref, o_ref = refs[2 + n_parts], refs[3 + n_parts]
    y = None
    lo = 0
    for p in part_refs:
        width = p.shape[-1]
        term = _dot(p[...], w_ref[lo:lo + width, :])
        y = term if y is None else y + term
        lo += width
    o_ref[...] = x_ref[...] + _rms(y, g_ref[3:4, :])


def _outproj(x2, norms, parts, w_out, layer, j):
    m, d = x2.shape
    tm = _row_tile(m)
    k_total = sum(p.shape[-1] for p in parts)
    kern = functools.partial(_outproj_kernel, n_parts=len(parts))
    return pl.pallas_call(
        kern,
        out_shape=jax.ShapeDtypeStruct((m, d), F32),
        grid=(m // tm,),
        in_specs=[pl.BlockSpec((tm, d), lambda i: (i, 0)),
                  _resident((None, 6, d), lambda i: (layer, 0, 0))]
        + [pl.BlockSpec((tm, p.shape[-1]), lambda i: (i, 0)) for p in parts]
        + [_resident((None, k_total, d), lambda i: (j, 0, 0))],
        out_specs=pl.BlockSpec((tm, d), lambda i: (i, 0)),
        compiler_params=pltpu.CompilerParams(
            dimension_semantics=("parallel",), vmem_limit_bytes=VMEM_LIMIT),
        name="outproj",
    )(x2, norms, *parts, w_out)


def _attn_kernel(lam_ref, subln_ref, q_ref, k_ref, v_ref, o_ref, *, tq, lam_init):
    head = pl.program_id(1)
    qi = pl.program_id(2)
    slope = jnp.exp((head + 1).astype(F32) * jnp.full((1, 1), -8.0 / A_HEADS * math.log(2.0), F32))

    lane = lax.broadcasted_iota(jnp.int32, (1, A_V_DIM), 1)
    q = q_ref[...] * jnp.asarray(A_QK_DIM ** -0.5, BF16)
    zero = jnp.zeros_like(q)
    q_maps = (jnp.where(lane < A_QK_DIM, q, zero), jnp.where(lane >= A_QK_DIM, q, zero))

    def block_scores(kb_start):
        k = k_ref[pl.ds(kb_start, tq), :]
        return [_dot_nt(qm, k) for qm in q_maps]

    row = lax.broadcasted_iota(jnp.int32, (tq, tq), 0)
    col = lax.broadcasted_iota(jnp.int32, (tq, tq), 1)
    allowed = (col // CHUNK) <= (row // CHUNK)
    diag_bias = slope * (row - jnp.abs(row - col)).astype(F32)
    diag_start = pl.multiple_of(qi * tq, tq)
    v_diag = v_ref[pl.ds(diag_start, tq), :]
    state = []
    for s in block_scores(diag_start):
        s = jnp.where(allowed, s + diag_bias, NEG_BIG)
        m = jnp.max(s, axis=-1, keepdims=True)
        p = jnp.exp(s - m)
        l = jnp.sum(p, axis=-1, keepdims=True)
        state += [m, l, _dot(p.astype(BF16), v_diag)]

    col_row = lax.broadcasted_iota(jnp.int32, (1, tq), 1)

    def body(kb, carry):
        start = pl.multiple_of(kb * tq, tq)
        v_blk = v_ref[pl.ds(start, tq), :]
        bias = slope * (col_row + (kb - qi) * tq).astype(F32)
        out = []
        for idx, s in enumerate(block_scores(start)):
            m_old, l_old, acc_old = carry[3 * idx:3 * idx + 3]
            s = s + bias
            m_new = jnp.maximum(m_old, jnp.max(s, axis=-1, keepdims=True))
            alpha = jnp.exp(m_old - m_new)
            p = jnp.exp(s - m_new)
            l_new = alpha * l_old + jnp.sum(p, axis=-1, keepdims=True)
            acc_new = alpha * acc_old + _dot(p.astype(BF16), v_blk)
            out += [m_new, l_new, acc_new]
        return tuple(out)

    m0, l0, acc0, m1, l1, acc1 = lax.fori_loop(0, qi, body, tuple(state))

    lv = lam_ref[...]
    lam = (jnp.exp(jnp.sum(lv[0:1] * lv[1:2], axis=-1, keepdims=True))
           - jnp.exp(jnp.sum(lv[2:3] * lv[3:4], axis=-1, keepdims=True)) + lam_init)
    o = acc0 / l0 - lam * (acc1 / l1)
    o_ref[...] = (_rms(o, subln_ref[...]) * (1.0 - lam_init)).astype(BF16)


def _diff_attention(za, lam_vecs, subln, lam_init, j):
    b, s, _ = za.shape
    tq = _seq_tile(s)
    kern = functools.partial(_attn_kernel, tq=tq, lam_init=lam_init)
    return pl.pallas_call(
        kern,
        out_shape=jax.ShapeDtypeStruct((b, s, A_WIDTH), BF16),
        grid=(b, A_HEADS, s // tq),
        in_specs=[
            _resident((None, 4, A_QK_DIM), lambda bi, h, qi: (j, 0, 0)),
            _resident((None, 1, A_V_DIM), lambda bi, h, qi: (j, 0, 0)),
            pl.BlockSpec((None, tq, A_V_DIM), lambda bi, h, qi: (bi, qi, h)),
            pl.BlockSpec((None, s, A_V_DIM), lambda bi, h, qi: (bi, 0, A_HEADS + h)),
            pl.BlockSpec((None, s, A_V_DIM), lambda bi, h, qi: (bi, 0, 2 * A_HEADS + h)),
        ],
        out_specs=pl.BlockSpec((None, tq, A_V_DIM), lambda bi, h, qi: (bi, qi, h)),
        compiler_params=pltpu.CompilerParams(
            dimension_semantics=("parallel", "parallel", "parallel"),
            vmem_limit_bytes=VMEM_LIMIT),
        name="diff_attention",
    )(lam_vecs, subln, za, za, za)


def _tri_inverse(n):
    row = lax.broadcasted_iota(jnp.int32, (CHUNK, CHUNK), 0)
    col = lax.broadcasted_iota(jnp.int32, (CHUNK, CHUNK), 1)
    inv = jnp.where(row == col, 1.0, 0.0) + n
    power = n
    for _ in range(int(math.log2(CHUNK)) - 1):
        pb = power.astype(BF16)
        power = _dot(pb, pb)
        inv = inv + _dot(inv.astype(BF16), power.astype(BF16))
    return inv


def _rwkv_kernel(*refs, tile, has_vres):
    (zb_ref, mu_ref, vec_ref, wup_ref, aup_ref, gup_ref, hsum_ref, ltri_ref, lall_ref) = refs[:9]
    if has_vres:
        vfirst_ref, v0_ref, vdown_ref, vup_ref, ob_ref = refs[9:14]
        scratch = refs[14:]
    else:
        ob_ref, vb_ref = refs[9:11]
        scratch = refs[11:]
    (state_ref, prev_ref, at_s, rt_s, bt_s, kt_s, be_s, ke_s, v_s, gam_s, y_s) = scratch

    @pl.when(pl.program_id(1) == 0)
    def _():
        state_ref[...] = jnp.zeros_like(state_ref)
        prev_ref[...] = jnp.zeros_like(prev_ref)

    z = zb_ref[...]
    row = lax.broadcasted_iota(jnp.int32, z.shape, 0)
    z_prev = jnp.where(row == 0, prev_ref[...], pltpu.roll(z, 1, axis=0))
    prev_ref[...] = z[tile - 1:tile, :]
    zs = z + (z_prev - z) * mu_ref[...]

    w3 = B_WIDTH
    r = zs[:, 0:w3]
    k = zs[:, w3:2 * w3]
    v = zs[:, 2 * w3:3 * w3]
    x_wa = zs[:, 3 * w3:3 * w3 + B_DECAY_RANK + B_A_RANK]
    x_g = zs[:, 3 * w3 + B_DECAY_RANK + B_A_RANK:]
    w0, a0, k_k, k_a, r_k, ln_w, ln_b = (vec_ref[i:i + 1, :] for i in range(7))

    d_in = -(w0 + _dot(jnp.tanh(x_wa).astype(BF16), wup_ref[...]))
    softplus = jnp.maximum(d_in, 0.0) + jnp.log(1.0 + jnp.exp(-jnp.abs(d_in)))
    log_decay = -jnp.exp(-softplus - 0.5)

    if has_vres:
        low = _dot(v.astype(BF16), vdown_ref[...])
        mixed = _dot(low.astype(BF16), vup_ref[...])
        v = v + (vfirst_ref[...] - v) * _sigmoid(v0_ref[...] + mixed)
    else:
        vb_ref[...] = v
    gate_a = _sigmoid(a0 + _dot(x_wa.astype(BF16), aup_ref[...]))
    gate_out = _dot(_sigmoid(x_g).astype(BF16), gup_ref[...])

    hsum = hsum_ref[...]
    kk = k * k_k
    kk = kk / jnp.maximum(jnp.sqrt(_ones_matmul_right(kk * kk, hsum)), 1e-12)
    k = k * (1.0 + (gate_a - 1.0) * k_a)
    b_vec = kk * gate_a

    cum = _ones_matmul_left(ltri_ref[...], log_decay)
    total = _ones_matmul_left(lall_ref[...], log_decay)
    g_inc = jnp.exp(cum)
    g_inv = jnp.exp(-cum)
    g_tail = jnp.exp(total - cum)
    rt_s[...] = r * g_inc
    kt_s[...] = k * g_inv
    bt_s[...] = b_vec * g_inv
    at_s[...] = -kk * jnp.exp(cum - log_decay)
    ke_s[...] = k * g_tail
    be_s[...] = b_vec * g_tail
    gam_s[...] = jnp.exp(total)
    v_s[...] = v

    row_c = lax.broadcasted_iota(jnp.int32, (CHUNK, 2 * CHUNK), 0)
    col_c = lax.broadcasted_iota(jnp.int32, (CHUNK, 2 * CHUNK), 1)
    col_local = jnp.where(col_c >= CHUNK, col_c - CHUNK, col_c)
    strict = col_local < row_c
    incl = col_local <= row_c
    k_half = col_c >= CHUNK

    def chunk_body(c, carry):
        rows = pl.ds(pl.multiple_of(c * CHUNK, CHUNK), CHUNK)
        for h in range(B_HEADS):
            cols = slice(h * B_HEAD_DIM, (h + 1) * B_HEAD_DIM)
            a_t = at_s[rows, cols].astype(BF16)
            r_t = rt_s[rows, cols].astype(BF16)
            bk_t = jnp.concatenate([bt_s[rows, cols], kt_s[rows, cols]], axis=0).astype(BF16)
            bk_e = jnp.concatenate([be_s[rows, cols], ke_s[rows, cols]], axis=0).astype(BF16)
            v_h = v_s[rows, cols]
            v_b = v_h.astype(BF16)
            s_old = state_ref[h]
            s_b = s_old.astype(BF16)

            a_scores = jnp.where(strict, _dot_nt(a_t, bk_t), 0.0)
            r_scores = jnp.where(incl, _dot_nt(r_t, bk_t), 0.0)
            inv = _tri_inverse(a_scores[:, 0:CHUNK])
            ak_only = jnp.where(k_half, a_scores, 0.0).astype(BF16)
            rhs = _dot_nt(a_t, s_b) + _dot(ak_only, jnp.concatenate([v_b, v_b], axis=0))
            u = _dot(inv.astype(BF16), rhs.astype(BF16))
            uv = jnp.concatenate([u, v_h], axis=0).astype(BF16)
            y_s[rows, cols] = _dot_nt(r_t, s_b) + _dot(r_scores.astype(BF16), uv)
            state_ref[h] = s_old * gam_s[rows, cols][0:1, :] + _dot_tn(uv, bk_e)
        return carry

    lax.fori_loop(0, tile // CHUNK, chunk_body, 0)

    y = y_s[...]
    inv_n = 1.0 / B_HEAD_DIM
    mean = _ones_matmul_right(y, hsum) * inv_n
    yc = y - mean
    var = _ones_matmul_right(yc * yc, hsum) * inv_n
    yn = yc * lax.rsqrt(var + B_LN_EPS) * ln_w + ln_b
    bonus = _ones_matmul_right(r * k * r_k, hsum) * v
    ob_ref[...] = ((yn + bonus) * gate_out).astype(BF16)


def _block_ones(n, block, lower):
    i = jnp.arange(n)
    same = (i[:, None] // block) == (i[None, :] // block)
    if lower:
        same = same & (i[None, :] <= i[:, None])
    return same.astype(BF16)


def _pad_rows(w, lo, total):
    return jnp.zeros((total, w.shape[1]), w.dtype).at[lo:lo + w.shape[0]].set(w)


def _rwkv(zb, mu, vec, w_up, a_up, g_up, v_first, vres):
    b, s, _ = zb.shape
    tile = _seq_tile(s)
    lora_in = B_DECAY_RANK + B_A_RANK
    wup_p = _pad_rows(w_up, 0, lora_in).astype(BF16)
    aup_p = _pad_rows(a_up, B_DECAY_RANK, lora_in).astype(BF16)
    consts = [mu.reshape(1, B_COLS), vec, wup_p, aup_p, g_up.astype(BF16),
              _block_ones(B_WIDTH, B_HEAD_DIM, False), _block_ones(tile, CHUNK, True),
              _block_ones(tile, CHUNK, False)]
    full = lambda a: _resident(a.shape, lambda bi, t: (0,) * a.ndim)
    seq = lambda width: pl.BlockSpec((None, tile, width), lambda bi, t: (bi, t, 0))
    in_specs = [seq(B_COLS)] + [full(a) for a in consts]
    args = [zb] + consts
    out_shape = [jax.ShapeDtypeStruct((b, s, B_WIDTH), BF16)]
    out_specs = [seq(B_WIDTH)]
    if vres is not None:
        v0, v_down, v_up = vres
        vdown_p = jnp.zeros((B_WIDTH, LANES), F32).at[:, :B_VRES_RANK].set(v_down).astype(BF16)
        vup_p = _pad_rows(v_up, 0, LANES).astype(BF16)
        extra = [v0.reshape(1, B_WIDTH), vdown_p, vup_p]
        in_specs += [seq(B_WIDTH)] + [full(a) for a in extra]
        args += [v_first] + extra
    else:
        out_shape.append(jax.ShapeDtypeStruct((b, s, B_WIDTH), F32))
        out_specs.append(seq(B_WIDTH))
    wide = lambda: pltpu.VMEM((tile, B_WIDTH), F32)
    kern = functools.partial(_rwkv_kernel, tile=tile, has_vres=vres is not None)
    outs = pl.pallas_call(
        kern,
        out_shape=tuple(out_shape),
        grid=(b, s // tile),
        in_specs=in_specs,
        out_specs=tuple(out_specs),
        scratch_shapes=[pltpu.VMEM((B_HEADS, B_HEAD_DIM, B_HEAD_DIM), F32),
                        pltpu.VMEM((1, B_COLS), F32)] + [wide() for _ in range(9)],
        compiler_params=pltpu.CompilerParams(
            dimension_semantics=("parallel", "arbitrary"), vmem_limit_bytes=VMEM_LIMIT),
        name="rwkv7",
    )(*args)
    return outs if vres is None else (outs[0], None)


def _retention_kernel(mask_ref, qdec_ref, kdec_ref, cdec_ref, q_ref, k_ref, v_ref, g_ref,
                      o_ref, state_ref):
    @pl.when(pl.program_id(2) == 0)
    def _():
        state_ref[...] = jnp.zeros_like(state_ref)

    q = q_ref[...]
    k = k_ref[...]
    v = v_ref[...]
    state = state_ref[...]
    scores = _dot_nt(q, k) * mask_ref[...]
    q_dec = (q.astype(F32) * qdec_ref[...]).astype(BF16)
    y = _dot(scores.astype(BF16), v) + _dot(q_dec, state.astype(BF16))
    k_dec = (k.astype(F32) * kdec_ref[...]).astype(BF16)
    state_ref[...] = state * cdec_ref[...] + _dot_tn(k_dec, v)
    yn = y * lax.rsqrt(jnp.mean(y * y, axis=-1, keepdims=True) + NORM_EPS)
    g = g_ref[...]
    o_ref[...] = (g * _sigmoid(g) * yn).astype(BF16)


def _retention_tables(tile):
    log_g = jnp.log(1.0 - 2.0 ** (-5.0 - jnp.arange(C_HEADS, dtype=F32)))[:, None, None]
    i = jnp.arange(tile)
    dist = jnp.abs(i[:, None] - i[None, :]).astype(F32)
    visible = (i[None, :] // CHUNK) <= (i[:, None] // CHUNK)
    scale = C_QK_DIM ** -0.5
    mask = jnp.where(visible[None], jnp.exp(log_g * dist[None]), 0.0) * scale
    idx = i.astype(F32)[None, :, None]
    qdec = jnp.broadcast_to(jnp.exp(log_g * (idx + 1.0)), (C_HEADS, tile, C_QK_DIM))
    kdec = jnp.broadcast_to(jnp.exp(log_g * (tile - 1.0 - idx)) * scale, (C_HEADS, tile, C_QK_DIM))
    cdec = jnp.broadcast_to(jnp.exp(log_g * tile), (C_HEADS, 1, C_V_DIM))
    return mask, qdec, kdec, cdec


def _retention(qk, v, gate):
    b, s, _ = v.shape
    tile = _seq_tile(s)
    mask, qdec, kdec, cdec = _retention_tables(tile)
    return pl.pallas_call(
        _retention_kernel,
        out_shape=jax.ShapeDtypeStruct((b, s, C_V_WIDTH), BF16),
        grid=(b, C_HEADS, s // tile),
        in_specs=[
            pl.BlockSpec((None, tile, tile), lambda bi, h, t: (h, 0, 0)),
            pl.BlockSpec((None, tile, C_QK_DIM), lambda bi, h, t: (h, 0, 0)),
            pl.BlockSpec((None, tile, C_QK_DIM), lambda bi, h, t: (h, 0, 0)),
            pl.BlockSpec((None, 1, C_V_DIM), lambda bi, h, t: (h, 0, 0)),
            pl.BlockSpec((None, tile, C_QK_DIM), lambda bi, h, t: (bi, t, h)),
            pl.BlockSpec((None, tile, C_QK_DIM), lambda bi, h, t: (bi, t, C_HEADS + h)),
            pl.BlockSpec((None, tile, C_V_DIM), lambda bi, h, t: (bi, t, h)),
            pl.BlockSpec((None, tile, C_V_DIM), lambda bi, h, t: (bi, t, h)),
        ],
        out_specs=pl.BlockSpec((None, tile, C_V_DIM), lambda bi, h, t: (bi, t, h)),
        scratch_shapes=[pltpu.VMEM((C_QK_DIM, C_V_DIM), F32)],
        compiler_params=pltpu.CompilerParams(
            dimension_semantics=("parallel", "parallel", "arbitrary"),
            vmem_limit_bytes=VMEM_LIMIT),
        name="retention",
    )(mask, qdec, kdec, cdec, qk, qk, v, gate)


def kernel(x, norms, ffn_wgu, ffn_wd, even_w_in, even_w_out, diff_lam, diff_subln, rwkv_mu, rwkv_vec, rwkv_w_up, rwkv_a_up, rwkv_g_up, rwkv_v0, rwkv_v_down, rwkv_v_up, odd_w_in, odd_w_out):
    b, s, d = x.shape
    m = b * s
    wgu = ffn_wgu.astype(BF16)
    wd = ffn_wd.astype(BF16)
    even_in = even_w_in.astype(BF16)
    even_out = even_w_out.astype(BF16)
    odd_in = odd_w_in.astype(BF16)
    odd_out = odd_w_out.astype(BF16)
    subln = diff_subln.reshape(-1, 1, A_V_DIM)

    x2 = x.reshape(m, d)
    v_first = None
    for i in range(DEPTH):
        j = i // 2
        x2 = _ffn(x2, norms, wgu, wd, i, 0)
        if i % 2 == 0:
            lam_init = 0.8 - 0.6 * math.exp(-0.3 * i)
            za, zb = _inproj_even(x2, norms, even_in, i, j)
            oa = _diff_attention(za.reshape(b, s, A_COLS), diff_lam, subln, lam_init, j)
            vres = None if j == 0 else (rwkv_v0[j - 1], rwkv_v_down[j - 1], rwkv_v_up[j - 1])
            ob, v_b = _rwkv(zb.reshape(b, s, B_COLS), rwkv_mu[j], rwkv_vec[j], rwkv_w_up[j],
                            rwkv_a_up[j], rwkv_g_up[j], v_first, vres)
            if j == 0:
                v_first = v_b
            parts = [oa.reshape(m, A_WIDTH), ob.reshape(m, B_WIDTH)]
            x2 = _outproj(x2, norms, parts, even_out, i, j)
        else:
            qk, v, gate = _inproj_odd(x2, norms, odd_in, i, j)
            y = _retention(qk.reshape(b, s, -1), v.reshape(b, s, -1), gate.reshape(b, s, -1))
            x2 = _outproj(x2, norms, [y.reshape(m, C_V_WIDTH)], odd_out, i, j)
        x2 = _ffn(x2, norms, wgu, wd, i, 1)
    return x2.reshape(b, s, d)
```

```python
import functools
import math

import jax
import jax.numpy as jnp
from jax import lax
from jax.experimental import pallas as pl
from jax.experimental.pallas import tpu as pltpu

F32 = jnp.float32
BF16 = jnp.bfloat16

D_MODEL = 1024
DEPTH = 4
CHUNK = 64
D_FF = 2816
NORM_EPS = 1e-6

A_HEADS = 4
A_QK_DIM = 64
A_V_DIM = 2 * A_QK_DIM
A_WIDTH = A_HEADS * A_V_DIM
A_COLS = 3 * A_WIDTH

B_HEAD_DIM = 64
B_WIDTH = D_MODEL - A_WIDTH
B_HEADS = B_WIDTH // B_HEAD_DIM
B_DECAY_RANK = 64
B_A_RANK = 64
B_GATE_RANK = 128
B_VRES_RANK = 32
B_LN_EPS = 64e-5
B_COLS = 3 * B_WIDTH + B_DECAY_RANK + B_A_RANK + B_GATE_RANK
EVEN_IN = A_COLS + B_COLS

C_HEADS = 4
C_QK_DIM = D_MODEL // C_HEADS
C_V_DIM = 2 * C_QK_DIM
C_QK_WIDTH = C_HEADS * C_QK_DIM
C_V_WIDTH = C_HEADS * C_V_DIM
ODD_IN = 2 * C_QK_WIDTH + 2 * C_V_WIDTH

LANES = 128
MXU_TILE = 256
ROW_TILE = 512
FF_TILE = 256
SEQ_TILE = 256
ATTN_GROUP = 4
RWKV_GROUP = 4
VMEM_LIMIT = 52 * 1024 * 1024
NEG_BIG = -1e30


def _dot(a, b):
    return jnp.dot(a, b, preferred_element_type=F32)


def _dot_nt(a, b):
    return lax.dot_general(a, b, (((1,), (1,)), ((), ())), preferred_element_type=F32)


def _dot_tn(a, b):
    return lax.dot_general(a, b, (((0,), (0,)), ((), ())), preferred_element_type=F32)


def _split2(x):
    hi = x.astype(BF16)
    lo = (x - hi.astype(F32)).astype(BF16)
    return hi, lo


def _head_sums(x, ones_block):
    width = ones_block.shape[0]
    out = []
    for lo_col in range(0, x.shape[1], width):
        hi, lo = _split2(x[:, lo_col:lo_col + width])
        out.append(_dot(hi, ones_block) + _dot(lo, ones_block))
    return jnp.concatenate(out, axis=1)


def _ones_matmul_left(m, x):
    hi, lo = _split2(x)
    return _dot(m, hi) + _dot(m, lo)


def _rms(x, g, eps=NORM_EPS):
    return x * lax.rsqrt(jnp.mean(x * x, axis=-1, keepdims=True) + eps) * g


def _sigmoid(x):
    return 1.0 / (1.0 + jnp.exp(-x))


def _row_tile(m):
    return ROW_TILE if m % ROW_TILE == 0 else m


def _seq_tile(s):
    return SEQ_TILE if s % SEQ_TILE == 0 else s


def _resident(block_shape, index_map):
    return pl.BlockSpec(block_shape, index_map, pipeline_mode=pl.Buffered(1))


def _ffn_kernel(x_ref, g_ref, wgu_ref, wd_ref, o_ref, act_ref, *, g_pre, g_post):
    x = x_ref[...]
    h = _rms(x, g_ref[g_pre:g_pre + 1, :]).astype(BF16)
    for j in range(D_FF // FF_TILE):
        lo = j * FF_TILE
        gate = _dot(h, wgu_ref[:, lo:lo + FF_TILE])
        up = _dot(h, wgu_ref[:, D_FF + lo:D_FF + lo + FF_TILE])
        act_ref[:, lo:lo + FF_TILE] = (gate * _sigmoid(gate) * up).astype(BF16)
    y = _dot(act_ref[...], wd_ref[...])
    o_ref[...] = x + 0.5 * _rms(y, g_ref[g_post:g_post + 1, :])


def _ffn(x2, norms, wgu, wd, layer, which):
    m, d = x2.shape
    tm = _row_tile(m)
    kern = functools.partial(_ffn_kernel, g_pre=4 * which, g_post=4 * which + 1)
    return pl.pallas_call(
        kern,
        out_shape=jax.ShapeDtypeStruct((m, d), F32),
        grid=(m // tm,),
        in_specs=[
            pl.BlockSpec((tm, d), lambda i: (i, 0)),
            _resident((None, 6, d), lambda i: (layer, 0, 0)),
            _resident((None, None, d, 2 * D_FF), lambda i: (layer, which, 0, 0)),
            _resident((None, None, D_FF, d), lambda i: (layer, which, 0, 0)),
        ],
        out_specs=pl.BlockSpec((tm, d), lambda i: (i, 0)),
        scratch_shapes=[pltpu.VMEM((tm, D_FF), BF16)],
        compiler_params=pltpu.CompilerParams(
            dimension_semantics=("parallel",), vmem_limit_bytes=VMEM_LIMIT),
        name="ffn",
    )(x2, norms, wgu, wd)


def _inproj_even_kernel(x_ref, g_ref, w_ref, za_ref, zb_ref):
    h = _rms(x_ref[...], g_ref[2:3, :]).astype(BF16)
    za_ref[...] = _dot(h, w_ref[:, :A_COLS]).astype(BF16)
    zb_ref[...] = _dot(h, w_ref[:, A_COLS:])


def _inproj_even(x2, norms, w_in, layer, j):
    m, d = x2.shape
    tm = _row_tile(m)
    return pl.pallas_call(
        _inproj_even_kernel,
        out_shape=(jax.ShapeDtypeStruct((m, A_COLS), BF16),
                   jax.ShapeDtypeStruct((m, B_COLS), F32)),
        grid=(m // tm,),
        in_specs=[
            pl.BlockSpec((tm, d), lambda i: (i, 0)),
            _resident((None, 6, d), lambda i: (layer, 0, 0)),
            _resident((None, d, EVEN_IN), lambda i: (j, 0, 0)),
        ],
        out_specs=(pl.BlockSpec((tm, A_COLS), lambda i: (i, 0)),
                   pl.BlockSpec((tm, B_COLS), lambda i: (i, 0))),
        compiler_params=pltpu.CompilerParams(
            dimension_semantics=("parallel",), vmem_limit_bytes=VMEM_LIMIT),
        name="inproj_even",
    )(x2, norms, w_in)


def _inproj_odd_kernel(x_ref, g_ref, w_ref, qk_ref, v_ref, gate_ref):
    h = _rms(x_ref[...], g_ref[2:3, :]).astype(BF16)
    qk_ref[...] = _dot(h, w_ref[:, :2 * C_QK_WIDTH]).astype(BF16)
    v_ref[...] = _dot(h, w_ref[:, 2 * C_QK_WIDTH:2 * C_QK_WIDTH + C_V_WIDTH]).astype(BF16)
    gate_ref[...] = _dot(h, w_ref[:, 2 * C_QK_WIDTH + C_V_WIDTH:])


def _inproj_odd(x2, norms, w_in, layer, j):
    m, d = x2.shape
    tm = _row_tile(m)
    return pl.pallas_call(
        _inproj_odd_kernel,
        out_shape=(jax.ShapeDtypeStruct((m, 2 * C_QK_WIDTH), BF16),
                   jax.ShapeDtypeStruct((m, C_V_WIDTH), BF16),
                   jax.ShapeDtypeStruct((m, C_V_WIDTH), F32)),
        grid=(m // tm,),
        in_specs=[
            pl.BlockSpec((tm, d), lambda i: (i, 0)),
            _resident((None, 6, d), lambda i: (layer, 0, 0)),
            _resident((None, d, ODD_IN), lambda i: (j, 0, 0)),
        ],
        out_specs=(pl.BlockSpec((tm, 2 * C_QK_WIDTH), lambda i: (i, 0)),
                   pl.BlockSpec((tm, C_V_WIDTH), lambda i: (i, 0)),
                   pl.BlockSpec((tm, C_V_WIDTH), lambda i: (i, 0))),
        compiler_params=pltpu.CompilerParams(
            dimension_semantics=("parallel",), vmem_limit_bytes=VMEM_LIMIT),
        name="inproj_odd",
    )(x2, norms, w_in)


def _outproj_kernel(*refs, n_parts):
    x_ref, g_ref = refs[0], refs[1]
    part_refs = refs[2:2 + n_parts]
    w_ref, o_ref = refs[2 + n_parts], refs[3 + n_parts]
    y = None
    lo = 0
    for p in part_refs:
        width = p.shape[-1]
        term = _dot(p[...], w_ref[lo:lo + width, :])
        y = term if y is None else y + term
        lo += width
    o_ref[...] = x_ref[...] + _rms(y, g_ref[3:4, :])


def _outproj(x2, norms, parts, w_out, layer, j):
    m, d = x2.shape
    tm = _row_tile(m)
    k_total = sum(p.shape[-1] for p in parts)
    kern = functools.partial(_outproj_kernel, n_parts=len(parts))
    return pl.pallas_call(
        kern,
        out_shape=jax.ShapeDtypeStruct((m, d), F32),
        grid=(m // tm,),
        in_specs=[pl.BlockSpec((tm, d), lambda i: (i, 0)),
                  _resident((None, 6, d), lambda i: (layer, 0, 0))]
        + [pl.BlockSpec((tm, p.shape[-1]), lambda i: (i, 0)) for p in parts]
        + [_resident((None, k_total, d), lambda i: (j, 0, 0))],
        out_specs=pl.BlockSpec((tm, d), lambda i: (i, 0)),
        compiler_params=pltpu.CompilerParams(
            dimension_semantics=("parallel",), vmem_limit_bytes=VMEM_LIMIT),
        name="outproj",
    )(x2, norms, *parts, w_out)


def _lane_tiles(x):
    return [x[:, i:i + LANES] for i in range(0, x.shape[1], LANES)]


def _attn_kernel(lam_ref, subln_ref, q_ref, k_ref, v_ref, o_ref, s_ref, m_ref, l_ref, acc_ref,
                 *, tq, lam_init):
    head = pl.program_id(1)
    qi = pl.program_id(2)
    slope = jnp.exp((head + 1).astype(F32) * jnp.full((1, 1), -8.0 / A_HEADS * math.log(2.0), F32))

    lane = lax.broadcasted_iota(jnp.int32, (1, A_V_DIM), 1)
    q = q_ref[...] * jnp.asarray(A_QK_DIM ** -0.5, BF16)
    zero = jnp.zeros_like(q)
    q_maps = (jnp.where(lane < A_QK_DIM, q, zero), jnp.where(lane >= A_QK_DIM, q, zero))
    n_maps = len(q_maps)
    col_row = lax.broadcasted_iota(jnp.int32, (1, tq), 1)

    def lane_max(m, s):
        for t in _lane_tiles(s):
            m = t if m is None else jnp.maximum(m, t)
        return m

    def pass1(blocks):
        m_new = [m_ref[mi] for mi in range(n_maps)]
        for kb in blocks:
            start = pl.multiple_of(kb * tq, tq)
            k = k_ref[pl.ds(start, tq), :]
            bias = slope * (col_row + (kb - qi) * tq).astype(F32)
            for mi, qm in enumerate(q_maps):
                s = _dot_nt(qm, k) + bias
                s_ref[mi, kb] = s
                m_new[mi] = lane_max(m_new[mi], s)
        for mi in range(n_maps):
            m_ref[mi] = m_new[mi]

    row = lax.broadcasted_iota(jnp.int32, (tq, tq), 0)
    col = lax.broadcasted_iota(jnp.int32, (tq, tq), 1)
    diag_bias = jnp.where((col // CHUNK) <= (row // CHUNK),
                          slope * (row - jnp.abs(row - col)).astype(F32), NEG_BIG)
    diag_start = pl.multiple_of(qi * tq, tq)
    k_diag = k_ref[pl.ds(diag_start, tq), :]
    for mi, qm in enumerate(q_maps):
        s = _dot_nt(qm, k_diag) + diag_bias
        s_ref[mi, qi] = s
        m_ref[mi] = lane_max(None, s)

    n_group = qi // ATTN_GROUP

    def p1_group(it, carry):
        pass1([it * ATTN_GROUP + g for g in range(ATTN_GROUP)])
        return carry

    def p1_single(kb, carry):
        pass1([kb])
        return carry

    lax.fori_loop(0, n_group, p1_group, 0)
    lax.fori_loop(n_group * ATTN_GROUP, qi, p1_single, 0)
    for mi in range(n_maps):
        m_ref[mi] = jnp.broadcast_to(jnp.max(m_ref[mi], axis=-1, keepdims=True), (tq, LANES))

    def pass2(blocks, first):
        for mi in range(n_maps):
            m_row = m_ref[mi]
            l_new = None if first else l_ref[mi]
            acc_new = None if first else acc_ref[mi]
            for kb in blocks:
                start = pl.multiple_of(kb * tq, tq)
                tiles = [jnp.exp(t - m_row) for t in _lane_tiles(s_ref[mi, kb])]
                for t in tiles:
                    l_new = t if l_new is None else l_new + t
                pv = _dot(jnp.concatenate(tiles, axis=1).astype(BF16), v_ref[pl.ds(start, tq), :])
                acc_new = pv if acc_new is None else acc_new + pv
            l_ref[mi] = l_new
            acc_ref[mi] = acc_new

    pass2([qi], True)

    def p2_group(it, carry):
        pass2([it * ATTN_GROUP + g for g in range(ATTN_GROUP)], False)
        return carry

    def p2_single(kb, carry):
        pass2([kb], False)
        return carry

    lax.fori_loop(0, n_group, p2_group, 0)
    lax.fori_loop(n_group * ATTN_GROUP, qi, p2_single, 0)

    lv = lam_ref[...]
    lam = (jnp.exp(jnp.sum(lv[0:1] * lv[1:2], axis=-1, keepdims=True))
           - jnp.exp(jnp.sum(lv[2:3] * lv[3:4], axis=-1, keepdims=True)) + lam_init)
    outs = [acc_ref[mi] / jnp.sum(l_ref[mi], axis=-1, keepdims=True) for mi in range(n_maps)]
    o = outs[0] - lam * outs[1]
    o_ref[...] = (_rms(o, subln_ref[...]) * (1.0 - lam_init)).astype(BF16)


def _diff_attention(za, lam_vecs, subln, lam_init, j):
    b, s, _ = za.shape
    tq = _seq_tile(s)
    kern = functools.partial(_attn_kernel, tq=tq, lam_init=lam_init)
    n_maps = 2
    return pl.pallas_call(
        kern,
        out_shape=jax.ShapeDtypeStruct((b, s, A_WIDTH), BF16),
        grid=(b, A_HEADS, s // tq),
        in_specs=[
            _resident((None, 4, A_QK_DIM), lambda bi, h, qi: (j, 0, 0)),
            _resident((None, 1, A_V_DIM), lambda bi, h, qi: (j, 0, 0)),
            pl.BlockSpec((None, tq, A_V_DIM), lambda bi, h, qi: (bi, qi, h)),
            pl.BlockSpec((None, s, A_V_DIM), lambda bi, h, qi: (bi, 0, A_HEADS + h)),
            pl.BlockSpec((None, s, A_V_DIM), lambda bi, h, qi: (bi, 0, 2 * A_HEADS + h)),
        ],
        out_specs=pl.BlockSpec((None, tq, A_V_DIM), lambda bi, h, qi: (bi, qi, h)),
        scratch_shapes=[pltpu.VMEM((n_maps, s // tq, tq, tq), F32),
                        pltpu.VMEM((n_maps, tq, LANES), F32),
                        pltpu.VMEM((n_maps, tq, LANES), F32),
                        pltpu.VMEM((n_maps, tq, A_V_DIM), F32)],
        compiler_params=pltpu.CompilerParams(
            dimension_semantics=("parallel", "parallel", "parallel"),
            vmem_limit_bytes=VMEM_LIMIT),
        name="diff_attention",
    )(lam_vecs, subln, za, za, za)


def _rwkv_kernel(*refs, tile, has_vres):
    (zb_ref, mu_ref, vec_ref, wup_ref, aup_ref, gup_ref, hsum_ref, ltri_ref, lall_ref) = refs[:9]
    if has_vres:
        vfirst_ref, v0_ref, vdown_ref, vup_ref, ob_ref = refs[9:14]
        scratch = refs[14:]
    else:
        ob_ref, vb_ref = refs[9:11]
        scratch = refs[11:]
    (state_ref, prev_ref, at_s, rt_s, bt_s, kt_s, be_s, ke_s, v_s, gam_s, y_s) = scratch

    @pl.when(pl.program_id(1) == 0)
    def _():
        state_ref[...] = jnp.zeros_like(state_ref)
        prev_ref[...] = jnp.zeros_like(prev_ref)

    z = zb_ref[...]
    row = lax.broadcasted_iota(jnp.int32, z.shape, 0)
    z_prev = jnp.where(row == 0, prev_ref[...], pltpu.roll(z, 1, axis=0))
    prev_ref[...] = z[tile - 1:tile, :]
    zs = z + (z_prev - z) * mu_ref[...]

    w3 = B_WIDTH
    r = zs[:, 0:w3]
    k = zs[:, w3:2 * w3]
    v = zs[:, 2 * w3:3 * w3]
    x_wa = zs[:, 3 * w3:3 * w3 + B_DECAY_RANK + B_A_RANK]
    x_g = zs[:, 3 * w3 + B_DECAY_RANK + B_A_RANK:]
    w0, a0, k_k, k_a, r_k, ln_w, ln_b = (vec_ref[i:i + 1, :] for i in range(7))

    d_in = -(w0 + _dot(jnp.tanh(x_wa).astype(BF16), wup_ref[...]))
    softplus = jnp.maximum(d_in, 0.0) + jnp.log(1.0 + jnp.exp(-jnp.abs(d_in)))
    log_decay = -jnp.exp(-softplus - 0.5)

    if has_vres:
        low = _dot(v.astype(BF16), vdown_ref[...])
        mixed = _dot(low.astype(BF16), vup_ref[...])
        v = v + (vfirst_ref[...] - v) * _sigmoid(v0_ref[...] + mixed)
    else:
        vb_ref[...] = v
    gate_a = _sigmoid(a0 + _dot(x_wa.astype(BF16), aup_ref[...]))
    gate_out = _dot(_sigmoid(x_g).astype(BF16), gup_ref[...])

    hsum = hsum_ref[...]
    kk = k * k_k
    kk = kk / jnp.maximum(jnp.sqrt(_head_sums(kk * kk, hsum)), 1e-12)
    k = k * (1.0 + (gate_a - 1.0) * k_a)
    b_vec = kk * gate_a

    cum = _ones_matmul_left(ltri_ref[...], log_decay)
    total = _ones_matmul_left(lall_ref[...], log_decay)
    g_inc = jnp.exp(cum)
    g_inv = jnp.exp(-cum)
    g_tail = jnp.exp(total - cum)
    rt_s[...] = r * g_inc
    kt_s[...] = k * g_inv
    bt_s[...] = b_vec * g_inv
    at_s[...] = -kk * jnp.exp(cum - log_decay)
    ke_s[...] = k * g_tail
    be_s[...] = b_vec * g_tail
    gam_s[...] = jnp.exp(total)
    v_s[...] = v

    row_c = lax.broadcasted_iota(jnp.int32, (CHUNK, 2 * CHUNK), 0)
    col_c = lax.broadcasted_iota(jnp.int32, (CHUNK, 2 * CHUNK), 1)
    k_half = col_c >= CHUNK
    col_local = jnp.where(k_half, col_c - CHUNK, col_c)
    strict = col_local < row_c
    incl = col_local <= row_c
    eye = jnp.where(lax.broadcasted_iota(jnp.int32, (CHUNK, CHUNK), 0)
                    == lax.broadcasted_iota(jnp.int32, (CHUNK, CHUNK), 1), 1.0, 0.0)
    zeros_half = jnp.zeros((CHUNK, B_HEAD_DIM), F32)

    def group_body(g, carry):
        pairs = [(ci, h) for ci in range(RWKV_GROUP) for h in range(B_HEADS)]
        rows = [pl.ds(pl.multiple_of((g * RWKV_GROUP + ci) * CHUNK, CHUNK), CHUNK)
                for ci in range(RWKV_GROUP)]
        load = lambda ref: [ref[rw, :] for rw in rows]
        a_all, r_all, bt_all, kt_all, be_all, ke_all, v_all = (
            load(ref) for ref in (at_s, rt_s, bt_s, kt_s, be_s, ke_s, v_s))
        gam_all = [gam_s[pl.ds(pl.multiple_of((g * RWKV_GROUP + ci) * CHUNK, CHUNK), 1), :]
                   for ci in range(RWKV_GROUP)]
        head = lambda arrs, ci, h: arrs[ci][:, h * B_HEAD_DIM:(h + 1) * B_HEAD_DIM]

        a_t, r_t, v_h, scores = {}, {}, {}, {}
        for p in pairs:
            a_t[p] = head(a_all, *p)
            r_t[p] = head(r_all, *p)
            v_h[p] = head(v_all, *p)
            lhs = jnp.concatenate([a_t[p], r_t[p]], axis=0).astype(BF16)
            rhs = jnp.concatenate([head(bt_all, *p), head(kt_all, *p)], axis=0).astype(BF16)
            scores[p] = _dot_nt(lhs, rhs)
        a_sc = {p: jnp.where(strict, scores[p][:CHUNK], 0.0) for p in pairs}
        r_sc = {p: jnp.where(incl, scores[p][CHUNK:], 0.0).astype(BF16) for p in pairs}

        power, inv, akv = {}, {}, {}
        for p in pairs:
            n = a_sc[p][:, :CHUNK]
            nb = n.astype(BF16)
            inv[p] = eye + n
            power[p] = _dot(nb, nb)
            ak_only = jnp.where(k_half, a_sc[p], 0.0).astype(BF16)
            akv[p] = _dot(ak_only, jnp.concatenate([v_h[p], v_h[p]], axis=0).astype(BF16))
        for _ in range(int(math.log2(CHUNK)) - 2):
            for p in pairs:
                pb = power[p].astype(BF16)
                both = _dot(jnp.concatenate([inv[p].astype(BF16), pb], axis=0), pb)
                inv[p] = inv[p] + both[:CHUNK]
                power[p] = both[CHUNK:]
        for p in pairs:
            inv[p] = inv[p] + _dot(inv[p].astype(BF16), power[p].astype(BF16))

        z_mat, g_mat, h_mat = {}, {}, {}
        for p in pairs:
            wu = _dot(inv[p].astype(BF16),
                      jnp.concatenate([a_t[p], akv[p]], axis=1).astype(BF16))
            z_mat[p] = jnp.concatenate(
                [wu, jnp.concatenate([zeros_half, v_h[p]], axis=1)], axis=0).astype(BF16)
        for p in pairs:
            g_mat[p] = _dot(r_sc[p], z_mat[p])
            bk_e = jnp.concatenate([head(be_all, *p), head(ke_all, *p)], axis=0).astype(BF16)
            h_mat[p] = _dot_tn(z_mat[p], bk_e)

        state = [state_ref[h] for h in range(B_HEADS)]
        y_rows = []
        for ci in range(RWKV_GROUP):
            y_heads = []
            for h in range(B_HEADS):
                p = (ci, h)
                s_b = state[h].astype(BF16)
                r_eff = (r_t[p] + g_mat[p][:, :B_HEAD_DIM]).astype(BF16)
                y_heads.append(_dot_nt(r_eff, s_b) + g_mat[p][:, B_HEAD_DIM:])
                state[h] = (state[h] * head(gam_all, ci, h)
                            + _dot(s_b, h_mat[p][:B_HEAD_DIM].astype(BF16)) + h_mat[p][B_HEAD_DIM:])
            y_rows.append(jnp.concatenate(y_heads, axis=1))
        for ci in range(RWKV_GROUP):
            y_s[rows[ci], :] = y_rows[ci]
        for h in range(B_HEADS):
            state_ref[h] = state[h]
        return carry

    lax.fori_loop(0, tile // (CHUNK * RWKV_GROUP), group_body, 0)

    y = y_s[...]
    inv_n = 1.0 / B_HEAD_DIM
    mean = _head_sums(y, hsum) * inv_n
    yc = y - mean
    var = _head_sums(yc * yc, hsum) * inv_n
    yn = yc * lax.rsqrt(var + B_LN_EPS) * ln_w + ln_b
    bonus = _head_sums(r * k * r_k, hsum) * v
    ob_ref[...] = ((yn + bonus) * gate_out).astype(BF16)


def _block_ones(n, block, lower):
    i = jnp.arange(n)
    same = (i[:, None] // block) == (i[None, :] // block)
    if lower:
        same = same & (i[None, :] <= i[:, None])
    return same.astype(BF16)


def _pad_rows(w, lo, total):
    return jnp.zeros((total, w.shape[1]), w.dtype).at[lo:lo + w.shape[0]].set(w)


def _rwkv(zb, mu, vec, w_up, a_up, g_up, v_first, vres):
    b, s, _ = zb.shape
    tile = _seq_tile(s)
    lora_in = B_DECAY_RANK + B_A_RANK
    wup_p = _pad_rows(w_up, 0, lora_in).astype(BF16)
    aup_p = _pad_rows(a_up, B_DECAY_RANK, lora_in).astype(BF16)
    consts = [mu.reshape(1, B_COLS), vec, wup_p, aup_p, g_up.astype(BF16),
              _block_ones(MXU_TILE, B_HEAD_DIM, False), _block_ones(tile, CHUNK, True),
              _block_ones(tile, CHUNK, False)]
    full = lambda a: _resident(a.shape, lambda bi, t: (0,) * a.ndim)
    seq = lambda width: pl.BlockSpec((None, tile, width), lambda bi, t: (bi, t, 0))
    in_specs = [seq(B_COLS)] + [full(a) for a in consts]
    args = [zb] + consts
    out_shape = [jax.ShapeDtypeStruct((b, s, B_WIDTH), BF16)]
    out_specs = [seq(B_WIDTH)]
    if vres is not None:
        v0, v_down, v_up = vres
        vdown_p = jnp.zeros((B_WIDTH, LANES), F32).at[:, :B_VRES_RANK].set(v_down).astype(BF16)
        vup_p = _pad_rows(v_up, 0, LANES).astype(BF16)
        extra = [v0.reshape(1, B_WIDTH), vdown_p, vup_p]
        in_specs += [seq(B_WIDTH)] + [full(a) for a in extra]
        args += [v_first] + extra
    else:
        out_shape.append(jax.ShapeDtypeStruct((b, s, B_WIDTH), F32))
        out_specs.append(seq(B_WIDTH))
    wide = lambda: pltpu.VMEM((tile, B_WIDTH), F32)
    kern = functools.partial(_rwkv_kernel, tile=tile, has_vres=vres is not None)
    outs = pl.pallas_call(
        kern,
        out_shape=tuple(out_shape),
        grid=(b, s // tile),
        in_specs=in_specs,
        out_specs=tuple(out_specs),
        scratch_shapes=[pltpu.VMEM((B_HEADS, B_HEAD_DIM, B_HEAD_DIM), F32),
                        pltpu.VMEM((1, B_COLS), F32)] + [wide() for _ in range(9)],
        compiler_params=pltpu.CompilerParams(
            dimension_semantics=("parallel", "arbitrary"), vmem_limit_bytes=VMEM_LIMIT),
        name="rwkv7",
    )(*args)
    return outs if vres is None else (outs[0], None)


def _retention_kernel(mask_ref, qdec_ref, kdec_ref, cdec_ref, q_ref, k_ref, v_ref, g_ref,
                      o_ref, state_ref):
    @pl.when(pl.program_id(2) == 0)
    def _():
        state_ref[...] = jnp.zeros_like(state_ref)

    q = q_ref[...]
    k = k_ref[...]
    v = v_ref[...]
    state = state_ref[...]
    scores = _dot_nt(q, k) * mask_ref[...]
    q_dec = (q.astype(F32) * qdec_ref[...]).astype(BF16)
    y = _dot(scores.astype(BF16), v) + _dot(q_dec, state.astype(BF16))
    k_dec = (k.astype(F32) * kdec_ref[...]).astype(BF16)
    state_ref[...] = state * cdec_ref[...] + _dot_tn(k_dec, v)
    yn = y * lax.rsqrt(jnp.mean(y * y, axis=-1, keepdims=True) + NORM_EPS)
    g = g_ref[...]
    o_ref[...] = (g * _sigmoid(g) * yn).astype(BF16)


def _retention_tables(tile):
    log_g = jnp.log(1.0 - 2.0 ** (-5.0 - jnp.arange(C_HEADS, dtype=F32)))[:, None, None]
    i = jnp.arange(tile)
    dist = jnp.abs(i[:, None] - i[None, :]).astype(F32)
    visible = (i[None, :] // CHUNK) <= (i[:, None] // CHUNK)
    scale = C_QK_DIM ** -0.5
    mask = jnp.where(visible[None], jnp.exp(log_g * dist[None]), 0.0) * scale
    idx = i.astype(F32)[None, :, None]
    qdec = jnp.broadcast_to(jnp.exp(log_g * (idx + 1.0)), (C_HEADS, tile, C_QK_DIM))
    kdec = jnp.broadcast_to(jnp.exp(log_g * (tile - 1.0 - idx)) * scale, (C_HEADS, tile, C_QK_DIM))
    cdec = jnp.broadcast_to(jnp.exp(log_g * tile), (C_HEADS, 1, C_V_DIM))
    return mask, qdec, kdec, cdec


def _retention(qk, v, gate):
    b, s, _ = v.shape
    tile = _seq_tile(s)
    mask, qdec, kdec, cdec = _retention_tables(tile)
    return pl.pallas_call(
        _retention_kernel,
        out_shape=jax.ShapeDtypeStruct((b, s, C_V_WIDTH), BF16),
        grid=(b, C_HEADS, s // tile),
        in_specs=[
            pl.BlockSpec((None, tile, tile), lambda bi, h, t: (h, 0, 0)),
            pl.BlockSpec((None, tile, C_QK_DIM), lambda bi, h, t: (h, 0, 0)),
            pl.BlockSpec((None, tile, C_QK_DIM), lambda bi, h, t: (h, 0, 0)),
            pl.BlockSpec((None, 1, C_V_DIM), lambda bi, h, t: (h, 0, 0)),
            pl.BlockSpec((None, tile, C_QK_DIM), lambda bi, h, t: (bi, t, h)),
            pl.BlockSpec((None, tile, C_QK_DIM), lambda bi, h, t: (bi, t, C_HEADS + h)),
            pl.BlockSpec((None, tile, C_V_DIM), lambda bi, h, t: (bi, t, h)),
            pl.BlockSpec((None, tile, C_V_DIM), lambda bi, h, t: (bi, t, h)),
        ],
        out_specs=pl.BlockSpec((None, tile, C_V_DIM), lambda bi, h, t: (bi, t, h)),
        scratch_shapes=[pltpu.VMEM((C_QK_DIM, C_V_DIM), F32)],
        compiler_params=pltpu.CompilerParams(
            dimension_semantics=("parallel", "parallel", "arbitrary"),
            vmem_limit_bytes=VMEM_LIMIT),
        name="retention",
    )(mask, qdec, kdec, cdec, qk, qk, v, gate)


def kernel(x, norms, ffn_wgu, ffn_wd, even_w_in, even_w_out, diff_lam, diff_subln, rwkv_mu, rwkv_vec, rwkv_w_up, rwkv_a_up, rwkv_g_up, rwkv_v0, rwkv_v_down, rwkv_v_up, odd_w_in, odd_w_out):
    b, s, d = x.shape
    m = b * s
    wgu = ffn_wgu.astype(BF16)
    wd = ffn_wd.astype(BF16)
    even_in = even_w_in.astype(BF16)
    even_out = even_w_out.astype(BF16)
    odd_in = odd_w_in.astype(BF16)
    odd_out = odd_w_out.astype(BF16)
    subln = diff_subln.reshape(-1, 1, A_V_DIM)

    x2 = x.reshape(m, d)
    v_first = None
    for i in range(DEPTH):
        j = i // 2
        x2 = _ffn(x2, norms, wgu, wd, i, 0)
        if i % 2 == 0:
            lam_init = 0.8 - 0.6 * math.exp(-0.3 * i)
            za, zb = _inproj_even(x2, norms, even_in, i, j)
            oa = _diff_attention(za.reshape(b, s, A_COLS), diff_lam, subln, lam_init, j)
            vres = None if j == 0 else (rwkv_v0[j - 1], rwkv_v_down[j - 1], rwkv_v_up[j - 1])
            ob, v_b = _rwkv(zb.reshape(b, s, B_COLS), rwkv_mu[j], rwkv_vec[j], rwkv_w_up[j],
                            rwkv_a_up[j], rwkv_g_up[j], v_first, vres)
            if j == 0:
                v_first = v_b
            parts = [oa.reshape(m, A_WIDTH), ob.reshape(m, B_WIDTH)]
            x2 = _outproj(x2, norms, parts, even_out, i, j)
        else:
            qk, v, gate = _inproj_odd(x2, norms, odd_in, i, j)
            y = _retention(qk.reshape(b, s, -1), v.reshape(b, s, -1), gate.reshape(b, s, -1))
            x2 = _outproj(x2, norms, [y.reshape(m, C_V_WIDTH)], odd_out, i, j)
        x2 = _ffn(x2, norms, wgu, wd, i, 1)
    return x2.reshape(b, s, d)
```

```python
import functools
import math

import jax
import jax.numpy as jnp
from jax import lax
from jax.experimental import pallas as pl
from jax.experimental.pallas import tpu as pltpu

F32 = jnp.float32
BF16 = jnp.bfloat16

D_MODEL = 1024
DEPTH = 4
CHUNK = 64
D_FF = 2816
NORM_EPS = 1e-6

A_HEADS = 4
A_QK_DIM = 64
A_V_DIM = 2 * A_QK_DIM
A_WIDTH = A_HEADS * A_V_DIM
A_COLS = 3 * A_WIDTH

B_HEAD_DIM = 64
B_WIDTH = D_MODEL - A_WIDTH
B_HEADS = B_WIDTH // B_HEAD_DIM
B_DECAY_RANK = 64
B_A_RANK = 64
B_GATE_RANK = 128
B_VRES_RANK = 32
B_LN_EPS = 64e-5
B_COLS = 3 * B_WIDTH + B_DECAY_RANK + B_A_RANK + B_GATE_RANK
EVEN_IN = A_COLS + B_COLS

C_HEADS = 4
C_QK_DIM = D_MODEL // C_HEADS
C_V_DIM = 2 * C_QK_DIM
C_QK_WIDTH = C_HEADS * C_QK_DIM
C_V_WIDTH = C_HEADS * C_V_DIM
ODD_IN = 2 * C_QK_WIDTH + 2 * C_V_WIDTH

LANES = 128
MXU_TILE = 256
ROW_TILE = 512
FF_TILE = 256
SEQ_TILE = 256
ATTN_Q_TILE = 512
ATTN_K_TILE = 256
RWKV_GROUP = 4
VMEM_LIMIT = 52 * 1024 * 1024
NEG_BIG = -1e30


def _dot(a, b):
    return jnp.dot(a, b, preferred_element_type=F32)


def _dot_nt(a, b):
    return lax.dot_general(a, b, (((1,), (1,)), ((), ())), preferred_element_type=F32)


def _dot_tn(a, b):
    return lax.dot_general(a, b, (((0,), (0,)), ((), ())), preferred_element_type=F32)


def _split2(x):
    hi = x.astype(BF16)
    lo = (x - hi.astype(F32)).astype(BF16)
    return hi, lo


def _head_sums(x, ones_block):
    width = ones_block.shape[0]
    out = []
    for lo_col in range(0, x.shape[1], width):
        hi, lo = _split2(x[:, lo_col:lo_col + width])
        out.append(_dot(hi, ones_block) + _dot(lo, ones_block))
    return jnp.concatenate(out, axis=1)


def _ones_matmul_left(m, x):
    hi, lo = _split2(x)
    return _dot(m, hi) + _dot(m, lo)


def _rms(x, g, eps=NORM_EPS):
    return x * lax.rsqrt(jnp.mean(x * x, axis=-1, keepdims=True) + eps) * g


def _sigmoid(x):
    return 1.0 / (1.0 + jnp.exp(-x))


def _row_tile(m):
    return ROW_TILE if m % ROW_TILE == 0 else m


def _seq_tile(s):
    return SEQ_TILE if s % SEQ_TILE == 0 else s


def _resident(block_shape, index_map):
    return pl.BlockSpec(block_shape, index_map, pipeline_mode=pl.Buffered(1))


def _ffn_kernel(x_ref, g_ref, wgu_ref, wd_ref, o_ref, act_ref, *, g_pre, g_post):
    x = x_ref[...]
    h = _rms(x, g_ref[g_pre:g_pre + 1, :]).astype(BF16)
    for j in range(D_FF // FF_TILE):
        lo = j * FF_TILE
        gate = _dot(h, wgu_ref[:, lo:lo + FF_TILE])
        up = _dot(h, wgu_ref[:, D_FF + lo:D_FF + lo + FF_TILE])
        act_ref[:, lo:lo + FF_TILE] = (gate * _sigmoid(gate) * up).astype(BF16)
    y = _dot(act_ref[...], wd_ref[...])
    o_ref[...] = x + 0.5 * _rms(y, g_ref[g_post:g_post + 1, :])


def _ffn(x2, norms, wgu, wd, layer, which):
    m, d = x2.shape
    tm = _row_tile(m)
    kern = functools.partial(_ffn_kernel, g_pre=4 * which, g_post=4 * which + 1)
    return pl.pallas_call(
        kern,
        out_shape=jax.ShapeDtypeStruct((m, d), F32),
        grid=(m // tm,),
        in_specs=[
            pl.BlockSpec((tm, d), lambda i: (i, 0)),
            _resident((None, 6, d), lambda i: (layer, 0, 0)),
            _resident((None, None, d, 2 * D_FF), lambda i: (layer, which, 0, 0)),
            _resident((None, None, D_FF, d), lambda i: (layer, which, 0, 0)),
        ],
        out_specs=pl.BlockSpec((tm, d), lambda i: (i, 0)),
        scratch_shapes=[pltpu.VMEM((tm, D_FF), BF16)],
        compiler_params=pltpu.CompilerParams(
            dimension_semantics=("parallel",), vmem_limit_bytes=VMEM_LIMIT),
        name="ffn",
    )(x2, norms, wgu, wd)


def _inproj_even_kernel(x_ref, g_ref, w_ref, za_ref, zb_ref):
    h = _rms(x_ref[...], g_ref[2:3, :]).astype(BF16)
    za_ref[...] = _dot(h, w_ref[:, :A_COLS]).astype(BF16)
    zb_ref[...] = _dot(h, w_ref[:, A_COLS:])


def _inproj_even(x2, norms, w_in, layer, j):
    m, d = x2.shape
    tm = _row_tile(m)
    return pl.pallas_call(
        _inproj_even_kernel,
        out_shape=(jax.ShapeDtypeStruct((m, A_COLS), BF16),
                   jax.ShapeDtypeStruct((m, B_COLS), F32)),
        grid=(m // tm,),
        in_specs=[
            pl.BlockSpec((tm, d), lambda i: (i, 0)),
            _resident((None, 6, d), lambda i: (layer, 0, 0)),
            _resident((None, d, EVEN_IN), lambda i: (j, 0, 0)),
        ],
        out_specs=(pl.BlockSpec((tm, A_COLS), lambda i: (i, 0)),
                   pl.BlockSpec((tm, B_COLS), lambda i: (i, 0))),
        compiler_params=pltpu.CompilerParams(
            dimension_semantics=("parallel",), vmem_limit_bytes=VMEM_LIMIT),
        name="inproj_even",
    )(x2, norms, w_in)


def _inproj_odd_kernel(x_ref, g_ref, w_ref, qk_ref, v_ref, gate_ref):
    h = _rms(x_ref[...], g_ref[2:3, :]).astype(BF16)
    qk_ref[...] = _dot(h, w_ref[:, :2 * C_QK_WIDTH]).astype(BF16)
    v_ref[...] = _dot(h, w_ref[:, 2 * C_QK_WIDTH:2 * C_QK_WIDTH + C_V_WIDTH]).astype(BF16)
    gate_ref[...] = _dot(h, w_ref[:, 2 * C_QK_WIDTH + C_V_WIDTH:])


def _inproj_odd(x2, norms, w_in, layer, j):
    m, d = x2.shape
    tm = _row_tile(m)
    return pl.pallas_call(
        _inproj_odd_kernel,
        out_shape=(jax.ShapeDtypeStruct((m, 2 * C_QK_WIDTH), BF16),
                   jax.ShapeDtypeStruct((m, C_V_WIDTH), BF16),
                   jax.ShapeDtypeStruct((m, C_V_WIDTH), F32)),
        grid=(m // tm,),
        in_specs=[
            pl.BlockSpec((tm, d), lambda i: (i, 0)),
            _resident((None, 6, d), lambda i: (layer, 0, 0)),
            _resident((None, d, ODD_IN), lambda i: (j, 0, 0)),
        ],
        out_specs=(pl.BlockSpec((tm, 2 * C_QK_WIDTH), lambda i: (i, 0)),
                   pl.BlockSpec((tm, C_V_WIDTH), lambda i: (i, 0)),
                   pl.BlockSpec((tm, C_V_WIDTH), lambda i: (i, 0))),
        compiler_params=pltpu.CompilerParams(
            dimension_semantics=("parallel",), vmem_limit_bytes=VMEM_LIMIT),
        name="inproj_odd",
    )(x2, norms, w_in)


def _outproj_kernel(*refs, n_parts):
    x_ref, g_ref = refs[0], refs[1]
    part_refs = refs[2:2 + n_parts]
    w_ref, o_ref = refs[2 + n_parts], refs[3 + n_parts]
    y = None
    lo = 0
    for p in part_refs:
        width = p.shape[-1]
        term = _dot(p[...], w_ref[lo:lo + width, :])
        y = term if y is None else y + term
        lo += width
    o_ref[...] = x_ref[...] + _rms(y, g_ref[3:4, :])


def _outproj(x2, norms, parts, w_out, layer, j):
    m, d = x2.shape
    tm = _row_tile(m)
    k_total = sum(p.shape[-1] for p in parts)
    kern = functools.partial(_outproj_kernel, n_parts=len(parts))
    return pl.pallas_call(
        kern,
        out_shape=jax.ShapeDtypeStruct((m, d), F32),
        grid=(m // tm,),
        in_specs=[pl.BlockSpec((tm, d), lambda i: (i, 0)),
                  _resident((None, 6, d), lambda i: (layer, 0, 0))]
        + [pl.BlockSpec((tm, p.shape[-1]), lambda i: (i, 0)) for p in parts]
        + [_resident((None, k_total, d), lambda i: (j, 0, 0))],
        out_specs=pl.BlockSpec((tm, d), lambda i: (i, 0)),
        compiler_params=pltpu.CompilerParams(
            dimension_semantics=("parallel",), vmem_limit_bytes=VMEM_LIMIT),
        name="outproj",
    )(x2, norms, *parts, w_out)


def _lane_tiles(x):
    return [x[:, i:i + LANES] for i in range(0, x.shape[1], LANES)]


def _attn_kernel(lam_ref, subln_ref, dbias_ref, q_ref, k_ref, v_ref, o_ref,
                 s_ref, m_ref, l_ref, acc_ref, *, tq, tk, lam_init):
    head = pl.program_id(1)
    qi = pl.program_id(2)
    n_diag = tq // tk
    first_diag = qi * n_diag
    log2e = 1.0 / math.log(2.0)
    slope2 = log2e * jnp.exp(
        (head + 1).astype(F32) * jnp.full((1, 1), -8.0 / A_HEADS * math.log(2.0), F32))

    lane = lax.broadcasted_iota(jnp.int32, (1, A_V_DIM), 1)
    q = q_ref[...] * jnp.asarray(A_QK_DIM ** -0.5, BF16)
    zero = jnp.zeros_like(q)
    q_maps = (jnp.where(lane < A_QK_DIM, q, zero), jnp.where(lane >= A_QK_DIM, q, zero))
    n_maps = len(q_maps)
    col_row = lax.broadcasted_iota(jnp.int32, (1, tk), 1)

    def lane_max(m, s):
        for t in _lane_tiles(s):
            m = t if m is None else jnp.maximum(m, t)
        return m

    def pass1(blocks, first):
        m_new = [None if first else m_ref[mi] for mi in range(n_maps)]
        for kb, diag_slot in blocks:
            k = k_ref[pl.ds(pl.multiple_of(kb * tk, tk), tk), :]
            if diag_slot is None:
                bias = slope2 * (col_row + (kb * tk - qi * tq)).astype(F32)
            else:
                bias = dbias_ref[:, diag_slot * tk:(diag_slot + 1) * tk]
            for mi, qm in enumerate(q_maps):
                s = _dot_nt(qm, k) * log2e + bias
                s_ref[mi, kb] = s
                m_new[mi] = lane_max(m_new[mi], s)
        for mi in range(n_maps):
            m_ref[mi] = m_new[mi]

    def pass2(blocks, first):
        for mi in range(n_maps):
            m_row = m_ref[mi]
            l_new = None if first else l_ref[mi]
            acc_new = None if first else acc_ref[mi]
            for kb in blocks:
                tiles = [jnp.exp2(t - m_row) for t in _lane_tiles(s_ref[mi, kb])]
                for t in tiles:
                    l_new = t if l_new is None else l_new + t
                pv = _dot(jnp.concatenate(tiles, axis=1).astype(BF16),
                          v_ref[pl.ds(pl.multiple_of(kb * tk, tk), tk), :])
                acc_new = pv if acc_new is None else acc_new + pv
            l_ref[mi] = l_new
            acc_ref[mi] = acc_new

    def visible_loops(fn):
        done = 0
        for size in (4, 2, 1):
            trips = (first_diag - done) // size
            base = done

            def body(it, carry, size=size, base=base):
                fn([base + it * size + g for g in range(size)])
                return carry

            lax.fori_loop(0, trips, body, 0)
            done = done + trips * size

    pass1([(first_diag + d, d) for d in range(n_diag)], True)
    visible_loops(lambda kbs: pass1([(kb, None) for kb in kbs], False))
    for mi in range(n_maps):
        m_ref[mi] = jnp.broadcast_to(jnp.max(m_ref[mi], axis=-1, keepdims=True), (tq, LANES))
    pass2([first_diag + d for d in range(n_diag)], True)
    visible_loops(lambda kbs: pass2(kbs, False))

    lv = lam_ref[...]
    lam = (jnp.exp(jnp.sum(lv[0:1] * lv[1:2], axis=-1, keepdims=True))
           - jnp.exp(jnp.sum(lv[2:3] * lv[3:4], axis=-1, keepdims=True)) + lam_init)
    outs = [acc_ref[mi] * (1.0 / jnp.sum(l_ref[mi], axis=-1, keepdims=True))
            for mi in range(n_maps)]
    o = outs[0] - lam * outs[1]
    o_ref[...] = (_rms(o, subln_ref[...]) * (1.0 - lam_init)).astype(BF16)


def _attn_diag_bias(tq):
    slopes = 2.0 ** (-8.0 / A_HEADS * jnp.arange(1, A_HEADS + 1, dtype=F32))
    i = jnp.arange(tq)
    rel = (i[:, None] - jnp.abs(i[:, None] - i[None, :])).astype(F32)
    visible = (i[None, :] // CHUNK) <= (i[:, None] // CHUNK)
    bias = slopes[:, None, None] * rel[None] / math.log(2.0)
    return jnp.where(visible[None], bias, NEG_BIG)


def _diff_attention(za, lam_vecs, subln, lam_init, j):
    b, s, _ = za.shape
    tq = ATTN_Q_TILE if s % ATTN_Q_TILE == 0 else s
    tk = ATTN_K_TILE if tq % ATTN_K_TILE == 0 else tq
    kern = functools.partial(_attn_kernel, tq=tq, tk=tk, lam_init=lam_init)
    n_maps = 2
    return pl.pallas_call(
        kern,
        out_shape=jax.ShapeDtypeStruct((b, s, A_WIDTH), BF16),
        grid=(b, A_HEADS, s // tq),
        in_specs=[
            _resident((None, 4, A_QK_DIM), lambda bi, h, qi: (j, 0, 0)),
            _resident((None, 1, A_V_DIM), lambda bi, h, qi: (j, 0, 0)),
            pl.BlockSpec((None, tq, tq), lambda bi, h, qi: (h, 0, 0)),
            pl.BlockSpec((None, tq, A_V_DIM), lambda bi, h, qi: (bi, qi, h)),
            pl.BlockSpec((None, s, A_V_DIM), lambda bi, h, qi: (bi, 0, A_HEADS + h)),
            pl.BlockSpec((None, s, A_V_DIM), lambda bi, h, qi: (bi, 0, 2 * A_HEADS + h)),
        ],
        out_specs=pl.BlockSpec((None, tq, A_V_DIM), lambda bi, h, qi: (bi, qi, h)),
        scratch_shapes=[pltpu.VMEM((n_maps, s // tk, tq, tk), F32),
                        pltpu.VMEM((n_maps, tq, LANES), F32),
                        pltpu.VMEM((n_maps, tq, LANES), F32),
                        pltpu.VMEM((n_maps, tq, A_V_DIM), F32)],
        compiler_params=pltpu.CompilerParams(
            dimension_semantics=("parallel", "parallel", "parallel"),
            vmem_limit_bytes=VMEM_LIMIT),
        name="diff_attention",
    )(lam_vecs, subln, _attn_diag_bias(tq), za, za, za)


def _rwkv_kernel(*refs, tile, has_vres):
    (zb_ref, mu_ref, vec_ref, wup_ref, aup_ref, gup_ref, hsum_ref, ltri_ref, lall_ref) = refs[:9]
    if has_vres:
        vfirst_ref, v0_ref, vdown_ref, vup_ref, ob_ref = refs[9:14]
        scratch = refs[14:]
    else:
        ob_ref, vb_ref = refs[9:11]
        scratch = refs[11:]
    (state_ref, prev_ref, at_s, rt_s, bt_s, kt_s, be_s, ke_s, v_s, gam_s, y_s) = scratch

    @pl.when(pl.program_id(1) == 0)
    def _():
        state_ref[...] = jnp.zeros_like(state_ref)
        prev_ref[...] = jnp.zeros_like(prev_ref)

    z = zb_ref[...]
    row = lax.broadcasted_iota(jnp.int32, z.shape, 0)
    z_prev = jnp.where(row == 0, prev_ref[...], pltpu.roll(z, 1, axis=0))
    prev_ref[...] = z[tile - 1:tile, :]
    zs = z + (z_prev - z) * mu_ref[...]

    w3 = B_WIDTH
    r = zs[:, 0:w3]
    k = zs[:, w3:2 * w3]
    v = zs[:, 2 * w3:3 * w3]
    x_wa = zs[:, 3 * w3:3 * w3 + B_DECAY_RANK + B_A_RANK]
    x_g = zs[:, 3 * w3 + B_DECAY_RANK + B_A_RANK:]
    w0, a0, k_k, k_a, r_k, ln_w, ln_b = (vec_ref[i:i + 1, :] for i in range(7))

    d_in = -(w0 + _dot(jnp.tanh(x_wa).astype(BF16), wup_ref[...]))
    softplus = jnp.maximum(d_in, 0.0) + jnp.log(1.0 + jnp.exp(-jnp.abs(d_in)))
    log_decay = -jnp.exp(-softplus - 0.5)

    if has_vres:
        low = _dot(v.astype(BF16), vdown_ref[...])
        mixed = _dot(low.astype(BF16), vup_ref[...])
        v = v + (vfirst_ref[...] - v) * _sigmoid(v0_ref[...] + mixed)
    else:
        vb_ref[...] = v
    gate_a = _sigmoid(a0 + _dot(x_wa.astype(BF16), aup_ref[...]))
    gate_out = _dot(_sigmoid(x_g).astype(BF16), gup_ref[...])

    hsum = hsum_ref[...]
    kk = k * k_k
    kk = kk / jnp.maximum(jnp.sqrt(_head_sums(kk * kk, hsum)), 1e-12)
    k = k * (1.0 + (gate_a - 1.0) * k_a)
    b_vec = kk * gate_a

    cum = _ones_matmul_left(ltri_ref[...], log_decay)
    total = _ones_matmul_left(lall_ref[...], log_decay)
    g_inc = jnp.exp(cum)
    g_inv = jnp.exp(-cum)
    g_tail = jnp.exp(total - cum)
    rt_s[...] = r * g_inc
    kt_s[...] = k * g_inv
    bt_s[...] = b_vec * g_inv
    at_s[...] = -kk * jnp.exp(cum - log_decay)
    ke_s[...] = k * g_tail
    be_s[...] = b_vec * g_tail
    gam_s[...] = jnp.exp(total)
    v_s[...] = v

    row_c = lax.broadcasted_iota(jnp.int32, (CHUNK, 2 * CHUNK), 0)
    col_c = lax.broadcasted_iota(jnp.int32, (CHUNK, 2 * CHUNK), 1)
    k_half = col_c >= CHUNK
    col_local = jnp.where(k_half, col_c - CHUNK, col_c)
    strict = col_local < row_c
    incl = col_local <= row_c
    eye = jnp.where(lax.broadcasted_iota(jnp.int32, (CHUNK, CHUNK), 0)
                    == lax.broadcasted_iota(jnp.int32, (CHUNK, CHUNK), 1), 1.0, 0.0)
    zeros_half = jnp.zeros((CHUNK, B_HEAD_DIM), F32)

    def group_body(g, carry):
        pairs = [(ci, h) for ci in range(RWKV_GROUP) for h in range(B_HEADS)]
        rows = [pl.ds(pl.multiple_of((g * RWKV_GROUP + ci) * CHUNK, CHUNK), CHUNK)
                for ci in range(RWKV_GROUP)]
        load = lambda ref: [ref[rw, :] for rw in rows]
        a_all, r_all, bt_all, kt_all, be_all, ke_all, v_all = (
            load(ref) for ref in (at_s, rt_s, bt_s, kt_s, be_s, ke_s, v_s))
        gam_all = [gam_s[pl.ds(pl.multiple_of((g * RWKV_GROUP + ci) * CHUNK, CHUNK), 1), :]
                   for ci in range(RWKV_GROUP)]
        head = lambda arrs, ci, h: arrs[ci][:, h * B_HEAD_DIM:(h + 1) * B_HEAD_DIM]

        a_t, r_t, v_h, scores = {}, {}, {}, {}
        for p in pairs:
            a_t[p] = head(a_all, *p)
            r_t[p] = head(r_all, *p)
            v_h[p] = head(v_all, *p)
            lhs = jnp.concatenate([a_t[p], r_t[p]], axis=0).astype(BF16)
            rhs = jnp.concatenate([head(bt_all, *p), head(kt_all, *p)], axis=0).astype(BF16)
            scores[p] = _dot_nt(lhs, rhs)
        a_sc = {p: jnp.where(strict, scores[p][:CHUNK], 0.0) for p in pairs}
        r_sc = {p: jnp.where(incl, scores[p][CHUNK:], 0.0).astype(BF16) for p in pairs}

        power, inv, akv = {}, {}, {}
        for p in pairs:
            n = a_sc[p][:, :CHUNK]
            nb = n.astype(BF16)
            inv[p] = eye + n
            power[p] = _dot(nb, nb)
            ak_only = jnp.where(k_half, a_sc[p], 0.0).astype(BF16)
            akv[p] = _dot(ak_only, jnp.concatenate([v_h[p], v_h[p]], axis=0).astype(BF16))
        for _ in range(int(math.log2(CHUNK)) - 2):
            for p in pairs:
                pb = power[p].astype(BF16)
                both = _dot(jnp.concatenate([inv[p].astype(BF16), pb], axis=0), pb)
                inv[p] = inv[p] + both[:CHUNK]
                power[p] = both[CHUNK:]
        for p in pairs:
            inv[p] = inv[p] + _dot(inv[p].astype(BF16), power[p].astype(BF16))

        z_mat, g_mat, h_mat = {}, {}, {}
        for p in pairs:
            wu = _dot(inv[p].astype(BF16),
                      jnp.concatenate([a_t[p], akv[p]], axis=1).astype(BF16))
            z_mat[p] = jnp.concatenate(
                [wu, jnp.concatenate([zeros_half, v_h[p]], axis=1)], axis=0).astype(BF16)
        for p in pairs:
            g_mat[p] = _dot(r_sc[p], z_mat[p])
            bk_e = jnp.concatenate([head(be_all, *p), head(ke_all, *p)], axis=0).astype(BF16)
            h_mat[p] = _dot_tn(z_mat[p], bk_e)

        state = [state_ref[h] for h in range(B_HEADS)]
        y_rows = []
        for ci in range(RWKV_GROUP):
            y_heads = []
            for h in range(B_HEADS):
                p = (ci, h)
                s_b = state[h].astype(BF16)
                r_eff = (r_t[p] + g_mat[p][:, :B_HEAD_DIM]).astype(BF16)
                y_heads.append(_dot_nt(r_eff, s_b) + g_mat[p][:, B_HEAD_DIM:])
                state[h] = (state[h] * head(gam_all, ci, h)
                            + _dot(s_b, h_mat[p][:B_HEAD_DIM].astype(BF16)) + h_mat[p][B_HEAD_DIM:])
            y_rows.append(jnp.concatenate(y_heads, axis=1))
        for ci in range(RWKV_GROUP):
            y_s[rows[ci], :] = y_rows[ci]
        for h in range(B_HEADS):
            state_ref[h] = state[h]
        return carry

    lax.fori_loop(0, tile // (CHUNK * RWKV_GROUP), group_body, 0)

    y = y_s[...]
    inv_n = 1.0 / B_HEAD_DIM
    mean = _head_sums(y, hsum) * inv_n
    yc = y - mean
    var = _head_sums(yc * yc, hsum) * inv_n
    yn = yc * lax.rsqrt(var + B_LN_EPS) * ln_w + ln_b
    bonus = _head_sums(r * k * r_k, hsum) * v
    ob_ref[...] = ((yn + bonus) * gate_out).astype(BF16)


def _block_ones(n, block, lower):
    i = jnp.arange(n)
    same = (i[:, None] // block) == (i[None, :] // block)
    if lower:
        same = same & (i[None, :] <= i[:, None])
    return same.astype(BF16)


def _pad_rows(w, lo, total):
    return jnp.zeros((total, w.shape[1]), w.dtype).at[lo:lo + w.shape[0]].set(w)


def _rwkv(zb, mu, vec, w_up, a_up, g_up, v_first, vres):
    b, s, _ = zb.shape
    tile = _seq_tile(s)
    lora_in = B_DECAY_RANK + B_A_RANK
    wup_p = _pad_rows(w_up, 0, lora_in).astype(BF16)
    aup_p = _pad_rows(a_up, B_DECAY_RANK, lora_in).astype(BF16)
    consts = [mu.reshape(1, B_COLS), vec, wup_p, aup_p, g_up.astype(BF16),
              _block_ones(MXU_TILE, B_HEAD_DIM, False), _block_ones(tile, CHUNK, True),
              _block_ones(tile, CHUNK, False)]
    full = lambda a: _resident(a.shape, lambda bi, t: (0,) * a.ndim)
    seq = lambda width: pl.BlockSpec((None, tile, width), lambda bi, t: (bi, t, 0))
    in_specs = [seq(B_COLS)] + [full(a) for a in consts]
    args = [zb] + consts
    out_shape = [jax.ShapeDtypeStruct((b, s, B_WIDTH), BF16)]
    out_specs = [seq(B_WIDTH)]
    if vres is not None:
        v0, v_down, v_up = vres
        vdown_p = jnp.zeros((B_WIDTH, LANES), F32).at[:, :B_VRES_RANK].set(v_down).astype(BF16)
        vup_p = _pad_rows(v_up, 0, LANES).astype(BF16)
        extra = [v0.reshape(1, B_WIDTH), vdown_p, vup_p]
        in_specs += [seq(B_WIDTH)] + [full(a) for a in extra]
        args += [v_first] + extra
    else:
        out_shape.append(jax.ShapeDtypeStruct((b, s, B_WIDTH), F32))
        out_specs.append(seq(B_WIDTH))
    wide = lambda: pltpu.VMEM((tile, B_WIDTH), F32)
    kern = functools.partial(_rwkv_kernel, tile=tile, has_vres=vres is not None)
    outs = pl.pallas_call(
        kern,
        out_shape=tuple(out_shape),
        grid=(b, s // tile),
        in_specs=in_specs,
        out_specs=tuple(out_specs),
        scratch_shapes=[pltpu.VMEM((B_HEADS, B_HEAD_DIM, B_HEAD_DIM), F32),
                        pltpu.VMEM((1, B_COLS), F32)] + [wide() for _ in range(9)],
        compiler_params=pltpu.CompilerParams(
            dimension_semantics=("parallel", "arbitrary"), vmem_limit_bytes=VMEM_LIMIT),
        name="rwkv7",
    )(*args)
    return outs if vres is None else (outs[0], None)


def _retention_kernel(mask_ref, qdec_ref, kdec_ref, cdec_ref, q_ref, k_ref, v_ref, g_ref,
                      o_ref, state_ref):
    @pl.when(pl.program_id(1) == 0)
    def _():
        state_ref[...] = jnp.zeros_like(state_ref)

    heads = range(C_HEADS)
    qk_cols = lambda h: slice(h * C_QK_DIM, (h + 1) * C_QK_DIM)
    v_cols = lambda h: slice(h * C_V_DIM, (h + 1) * C_V_DIM)
    q = [q_ref[:, qk_cols(h)] for h in heads]
    k = [k_ref[:, qk_cols(h)] for h in heads]
    v = [v_ref[:, v_cols(h)] for h in heads]
    state = [state_ref[h] for h in heads]
    scores = [(_dot_nt(q[h], k[h]) * mask_ref[h]).astype(BF16) for h in heads]
    q_dec = [(q[h].astype(F32) * qdec_ref[h]).astype(BF16) for h in heads]
    y = [_dot(scores[h], v[h]) + _dot(q_dec[h], state[h].astype(BF16)) for h in heads]
    k_dec = [(k[h].astype(F32) * kdec_ref[h]).astype(BF16) for h in heads]
    for h in heads:
        state_ref[h] = state[h] * cdec_ref[h] + _dot_tn(k_dec[h], v[h])
    for h in heads:
        yn = y[h] * lax.rsqrt(jnp.mean(y[h] * y[h], axis=-1, keepdims=True) + NORM_EPS)
        g = g_ref[:, v_cols(h)]
        o_ref[:, v_cols(h)] = (g * _sigmoid(g) * yn).astype(BF16)


def _retention_tables(tile):
    log_g = jnp.log(1.0 - 2.0 ** (-5.0 - jnp.arange(C_HEADS, dtype=F32)))[:, None, None]
    i = jnp.arange(tile)
    dist = jnp.abs(i[:, None] - i[None, :]).astype(F32)
    visible = (i[None, :] // CHUNK) <= (i[:, None] // CHUNK)
    scale = C_QK_DIM ** -0.5
    mask = jnp.where(visible[None], jnp.exp(log_g * dist[None]), 0.0) * scale
    idx = i.astype(F32)[None, :, None]
    qdec = jnp.broadcast_to(jnp.exp(log_g * (idx + 1.0)), (C_HEADS, tile, C_QK_DIM))
    kdec = jnp.broadcast_to(jnp.exp(log_g * (tile - 1.0 - idx)) * scale, (C_HEADS, tile, C_QK_DIM))
    cdec = jnp.broadcast_to(jnp.exp(log_g * tile), (C_HEADS, 1, C_V_DIM))
    return mask, qdec, kdec, cdec


def _retention(qk, v, gate):
    b, s, _ = v.shape
    tile = _seq_tile(s)
    mask, qdec, kdec, cdec = _retention_tables(tile)
    return pl.pallas_call(
        _retention_kernel,
        out_shape=jax.ShapeDtypeStruct((b, s, C_V_WIDTH), BF16),
        grid=(b, s // tile),
        in_specs=[
            _resident(mask.shape, lambda bi, t: (0, 0, 0)),
            _resident(qdec.shape, lambda bi, t: (0, 0, 0)),
            _resident(kdec.shape, lambda bi, t: (0, 0, 0)),
            _resident(cdec.shape, lambda bi, t: (0, 0, 0)),
            pl.BlockSpec((None, tile, C_QK_WIDTH), lambda bi, t: (bi, t, 0)),
            pl.BlockSpec((None, tile, C_QK_WIDTH), lambda bi, t: (bi, t, 1)),
            pl.BlockSpec((None, tile, C_V_WIDTH), lambda bi, t: (bi, t, 0)),
            pl.BlockSpec((None, tile, C_V_WIDTH), lambda bi, t: (bi, t, 0)),
        ],
        out_specs=pl.BlockSpec((None, tile, C_V_WIDTH), lambda bi, t: (bi, t, 0)),
        scratch_shapes=[pltpu.VMEM((C_HEADS, C_QK_DIM, C_V_DIM), F32)],
        compiler_params=pltpu.CompilerParams(
            dimension_semantics=("parallel", "arbitrary"),
            vmem_limit_bytes=VMEM_LIMIT),
        name="retention",
    )(mask, qdec, kdec, cdec, qk, qk, v, gate)


def kernel(x, norms, ffn_wgu, ffn_wd, even_w_in, even_w_out, diff_lam, diff_subln, rwkv_mu, rwkv_vec, rwkv_w_up, rwkv_a_up, rwkv_g_up, rwkv_v0, rwkv_v_down, rwkv_v_up, odd_w_in, odd_w_out):
    b, s, d = x.shape
    m = b * s
    wgu = ffn_wgu.astype(BF16)
    wd = ffn_wd.astype(BF16)
    even_in = even_w_in.astype(BF16)
    even_out = even_w_out.astype(BF16)
    odd_in = odd_w_in.astype(BF16)
    odd_out = odd_w_out.astype(BF16)
    subln = diff_subln.reshape(-1, 1, A_V_DIM)

    x2 = x.reshape(m, d)
    v_first = None
    for i in range(DEPTH):
        j = i // 2
        x2 = _ffn(x2, norms, wgu, wd, i, 0)
        if i % 2 == 0:
            lam_init = 0.8 - 0.6 * math.exp(-0.3 * i)
            za, zb = _inproj_even(x2, norms, even_in, i, j)
            oa = _diff_attention(za.reshape(b, s, A_COLS), diff_lam, subln, lam_init, j)
            vres = None if j == 0 else (rwkv_v0[j - 1], rwkv_v_down[j - 1], rwkv_v_up[j - 1])
            ob, v_b = _rwkv(zb.reshape(b, s, B_COLS), rwkv_mu[j], rwkv_vec[j], rwkv_w_up[j],
                            rwkv_a_up[j], rwkv_g_up[j], v_first, vres)
            if j == 0:
                v_first = v_b
            parts = [oa.reshape(m, A_WIDTH), ob.reshape(m, B_WIDTH)]
            x2 = _outproj(x2, norms, parts, even_out, i, j)
        else:
            qk, v, gate = _inproj_odd(x2, norms, odd_in, i, j)
            y = _retention(qk.reshape(b, s, -1), v.reshape(b, s, -1), gate.reshape(b, s, -1))
            x2 = _outproj(x2, norms, [y.reshape(m, C_V_WIDTH)], odd_out, i, j)
        x2 = _ffn(x2, norms, wgu, wd, i, 1)
    return x2.reshape(b, s, d)
```

```python
import functools
import math

import jax
import jax.numpy as jnp
from jax import lax
from jax.experimental import pallas as pl
from jax.experimental.pallas import tpu as pltpu

F32 = jnp.float32
BF16 = jnp.bfloat16

D_MODEL = 1024
DEPTH = 4
CHUNK = 64
D_FF = 2816
NORM_EPS = 1e-6

A_HEADS = 4
A_QK_DIM = 64
A_V_DIM = 2 * A_QK_DIM
A_WIDTH = A_HEADS * A_V_DIM
A_COLS = 3 * A_WIDTH

B_HEAD_DIM = 64
B_WIDTH = D_MODEL - A_WIDTH
B_HEADS = B_WIDTH // B_HEAD_DIM
B_DECAY_RANK = 64
B_A_RANK = 64
B_GATE_RANK = 128
B_VRES_RANK = 32
B_LN_EPS = 64e-5
B_COLS = 3 * B_WIDTH + B_DECAY_RANK + B_A_RANK + B_GATE_RANK
EVEN_IN = A_COLS + B_COLS

C_HEADS = 4
C_QK_DIM = D_MODEL // C_HEADS
C_V_DIM = 2 * C_QK_DIM
C_QK_WIDTH = C_HEADS * C_QK_DIM
C_V_WIDTH = C_HEADS * C_V_DIM
ODD_IN = 2 * C_QK_WIDTH + 2 * C_V_WIDTH

LANES = 128
MXU_TILE = 256
ROW_TILE = 512
FF_TILE = 256
SEQ_TILE = 256
ATTN_Q_TILE = 512
ATTN_K_TILE = 256
RWKV_GROUP = 4
VMEM_LIMIT = 52 * 1024 * 1024
NEG_BIG = -1e30


def _dot(a, b):
    return jnp.dot(a, b, preferred_element_type=F32)


def _dot_nt(a, b):
    return lax.dot_general(a, b, (((1,), (1,)), ((), ())), preferred_element_type=F32)


def _dot_tn(a, b):
    return lax.dot_general(a, b, (((0,), (0,)), ((), ())), preferred_element_type=F32)


def _split2(x):
    hi = x.astype(BF16)
    lo = (x - hi.astype(F32)).astype(BF16)
    return hi, lo


def _head_sums(x, ones_block, split):
    width = ones_block.shape[0]
    out = []
    for lo_col in range(0, x.shape[1], width):
        xs = x[:, lo_col:lo_col + width]
        if split:
            hi, lo = _split2(xs)
            out.append(_dot(hi, ones_block) + _dot(lo, ones_block))
        else:
            out.append(_dot(xs.astype(BF16), ones_block))
    return jnp.concatenate(out, axis=1)


def _ones_matmul_left(m, x):
    hi, lo = _split2(x)
    return _dot(m, hi) + _dot(m, lo)


def _rms(x, g, eps=NORM_EPS):
    return x * lax.rsqrt(jnp.mean(x * x, axis=-1, keepdims=True) + eps) * g


def _sigmoid(x):
    return 1.0 / (1.0 + jnp.exp(-x))


def _row_tile(m):
    return ROW_TILE if m % ROW_TILE == 0 else m


def _seq_tile(s):
    return SEQ_TILE if s % SEQ_TILE == 0 else s


def _resident(block_shape, index_map):
    return pl.BlockSpec(block_shape, index_map, pipeline_mode=pl.Buffered(1))


def _ffn_kernel(*refs, g_pre, g_post, n_parts):
    x_ref, g_ref, wgu_ref, wd_ref = refs[:4]
    part_refs = refs[4:4 + n_parts]
    o_ref, act_ref = refs[-2:]
    x = x_ref[...]
    if n_parts:
        wout_ref = refs[4 + n_parts]
        mix = None
        lo = 0
        for p in part_refs:
            width = p.shape[-1]
            term = _dot(p[...], wout_ref[lo:lo + width, :])
            mix = term if mix is None else mix + term
            lo += width
        x = x + _rms(mix, g_ref[3:4, :])
    h = _rms(x, g_ref[g_pre:g_pre + 1, :]).astype(BF16)
    for j in range(D_FF // FF_TILE):
        lo = j * FF_TILE
        gate = _dot(h, wgu_ref[:, lo:lo + FF_TILE])
        up = _dot(h, wgu_ref[:, D_FF + lo:D_FF + lo + FF_TILE])
        act_ref[:, lo:lo + FF_TILE] = (gate * _sigmoid(gate) * up).astype(BF16)
    y = _dot(act_ref[...], wd_ref[...])
    o_ref[...] = x + 0.5 * _rms(y, g_ref[g_post:g_post + 1, :])


def _ffn(x2, norms, wgu, wd, layer, which, parts=(), w_out=None, j=0):
    m, d = x2.shape
    tm = _row_tile(m)
    kern = functools.partial(_ffn_kernel, g_pre=4 * which, g_post=4 * which + 1,
                             n_parts=len(parts))
    in_specs = [
        pl.BlockSpec((tm, d), lambda i: (i, 0)),
        _resident((None, 6, d), lambda i: (layer, 0, 0)),
        _resident((None, None, d, 2 * D_FF), lambda i: (layer, which, 0, 0)),
        _resident((None, None, D_FF, d), lambda i: (layer, which, 0, 0)),
    ]
    args = [x2, norms, wgu, wd]
    if parts:
        k_total = sum(p.shape[-1] for p in parts)
        in_specs += [pl.BlockSpec((tm, p.shape[-1]), lambda i: (i, 0)) for p in parts]
        in_specs.append(_resident((None, k_total, d), lambda i: (j, 0, 0)))
        args += list(parts) + [w_out]
    return pl.pallas_call(
        kern,
        out_shape=jax.ShapeDtypeStruct((m, d), F32),
        grid=(m // tm,),
        in_specs=in_specs,
        out_specs=pl.BlockSpec((tm, d), lambda i: (i, 0)),
        scratch_shapes=[pltpu.VMEM((tm, D_FF), BF16)],
        compiler_params=pltpu.CompilerParams(
            dimension_semantics=("parallel",), vmem_limit_bytes=VMEM_LIMIT),
        name="ffn_post" if parts else "ffn",
    )(*args)


def _inproj_even_kernel(x_ref, g_ref, w_ref, za_ref, zb_ref):
    h = _rms(x_ref[...], g_ref[2:3, :]).astype(BF16)
    za_ref[...] = _dot(h, w_ref[:, :A_COLS]).astype(BF16)
    zb_ref[...] = _dot(h, w_ref[:, A_COLS:])


def _inproj_even(x2, norms, w_in, layer, j):
    m, d = x2.shape
    tm = _row_tile(m)
    return pl.pallas_call(
        _inproj_even_kernel,
        out_shape=(jax.ShapeDtypeStruct((m, A_COLS), BF16),
                   jax.ShapeDtypeStruct((m, B_COLS), F32)),
        grid=(m // tm,),
        in_specs=[
            pl.BlockSpec((tm, d), lambda i: (i, 0)),
            _resident((None, 6, d), lambda i: (layer, 0, 0)),
            _resident((None, d, EVEN_IN), lambda i: (j, 0, 0)),
        ],
        out_specs=(pl.BlockSpec((tm, A_COLS), lambda i: (i, 0)),
                   pl.BlockSpec((tm, B_COLS), lambda i: (i, 0))),
        compiler_params=pltpu.CompilerParams(
            dimension_semantics=("parallel",), vmem_limit_bytes=VMEM_LIMIT),
        name="inproj_even",
    )(x2, norms, w_in)


def _inproj_odd_kernel(x_ref, g_ref, w_ref, qk_ref, v_ref, gate_ref):
    h = _rms(x_ref[...], g_ref[2:3, :]).astype(BF16)
    qk_ref[...] = _dot(h, w_ref[:, :2 * C_QK_WIDTH]).astype(BF16)
    v_ref[...] = _dot(h, w_ref[:, 2 * C_QK_WIDTH:2 * C_QK_WIDTH + C_V_WIDTH]).astype(BF16)
    gate_ref[...] = _dot(h, w_ref[:, 2 * C_QK_WIDTH + C_V_WIDTH:])


def _inproj_odd(x2, norms, w_in, layer, j):
    m, d = x2.shape
    tm = _row_tile(m)
    return pl.pallas_call(
        _inproj_odd_kernel,
        out_shape=(jax.ShapeDtypeStruct((m, 2 * C_QK_WIDTH), BF16),
                   jax.ShapeDtypeStruct((m, C_V_WIDTH), BF16),
                   jax.ShapeDtypeStruct((m, C_V_WIDTH), F32)),
        grid=(m // tm,),
        in_specs=[
            pl.BlockSpec((tm, d), lambda i: (i, 0)),
            _resident((None, 6, d), lambda i: (layer, 0, 0)),
            _resident((None, d, ODD_IN), lambda i: (j, 0, 0)),
        ],
        out_specs=(pl.BlockSpec((tm, 2 * C_QK_WIDTH), lambda i: (i, 0)),
                   pl.BlockSpec((tm, C_V_WIDTH), lambda i: (i, 0)),
                   pl.BlockSpec((tm, C_V_WIDTH), lambda i: (i, 0))),
        compiler_params=pltpu.CompilerParams(
            dimension_semantics=("parallel",), vmem_limit_bytes=VMEM_LIMIT),
        name="inproj_odd",
    )(x2, norms, w_in)


def _lane_tiles(x):
    return [x[:, i:i + LANES] for i in range(0, x.shape[1], LANES)]


def _attn_kernel(lam_ref, subln_ref, dbias_ref, q_ref, k_ref, v_ref, o_ref,
                 s_ref, m_ref, l_ref, acc_ref, *, tq, tk, lam_init):
    head = pl.program_id(1)
    qi = pl.program_id(2)
    n_diag = tq // tk
    first_diag = qi * n_diag
    log2e = 1.0 / math.log(2.0)
    slope2 = log2e * jnp.exp(
        (head + 1).astype(F32) * jnp.full((1, 1), -8.0 / A_HEADS * math.log(2.0), F32))

    lane = lax.broadcasted_iota(jnp.int32, (1, A_V_DIM), 1)
    q = q_ref[...] * jnp.asarray(A_QK_DIM ** -0.5, BF16)
    zero = jnp.zeros_like(q)
    q_maps = (jnp.where(lane < A_QK_DIM, q, zero), jnp.where(lane >= A_QK_DIM, q, zero))
    n_maps = len(q_maps)
    col_row = lax.broadcasted_iota(jnp.int32, (1, tk), 1)

    def lane_max(m, s):
        for t in _lane_tiles(s):
            m = t if m is None else jnp.maximum(m, t)
        return m

    def pass1(blocks, first):
        m_new = [None if first else m_ref[mi] for mi in range(n_maps)]
        for kb, diag_slot in blocks:
            k = k_ref[pl.ds(pl.multiple_of(kb * tk, tk), tk), :]
            if diag_slot is None:
                bias = slope2 * (col_row + (kb * tk - qi * tq)).astype(F32)
            else:
                bias = dbias_ref[:, diag_slot * tk:(diag_slot + 1) * tk]
            for mi, qm in enumerate(q_maps):
                s = _dot_nt(qm, k) * log2e + bias
                s_ref[mi, kb] = s
                m_new[mi] = lane_max(m_new[mi], s)
        for mi in range(n_maps):
            m_ref[mi] = m_new[mi]

    def pass2(blocks, first):
        for mi in range(n_maps):
            m_row = m_ref[mi]
            l_new = None if first else l_ref[mi]
            acc_new = None if first else acc_ref[mi]
            for kb in blocks:
                tiles = [jnp.exp2(t - m_row) for t in _lane_tiles(s_ref[mi, kb])]
                for t in tiles:
                    l_new = t if l_new is None else l_new + t
                pv = _dot(jnp.concatenate(tiles, axis=1).astype(BF16),
                          v_ref[pl.ds(pl.multiple_of(kb * tk, tk), tk), :])
                acc_new = pv if acc_new is None else acc_new + pv
            l_ref[mi] = l_new
            acc_ref[mi] = acc_new

    def visible_loops(fn):
        done = 0
        for size in (4, 2, 1):
            trips = (first_diag - done) // size
            base = done

            def body(it, carry, size=size, base=base):
                fn([base + it * size + g for g in range(size)])
                return carry

            lax.fori_loop(0, trips, body, 0)
            done = done + trips * size

    pass1([(first_diag + d, d) for d in range(n_diag)], True)
    visible_loops(lambda kbs: pass1([(kb, None) for kb in kbs], False))
    for mi in range(n_maps):
        m_ref[mi] = jnp.broadcast_to(jnp.max(m_ref[mi], axis=-1, keepdims=True), (tq, LANES))
    pass2([first_diag + d for d in range(n_diag)], True)
    visible_loops(lambda kbs: pass2(kbs, False))

    lv = lam_ref[...]
    lam = (jnp.exp(jnp.sum(lv[0:1] * lv[1:2], axis=-1, keepdims=True))
           - jnp.exp(jnp.sum(lv[2:3] * lv[3:4], axis=-1, keepdims=True)) + lam_init)
    outs = [acc_ref[mi] * (1.0 / jnp.sum(l_ref[mi], axis=-1, keepdims=True))
            for mi in range(n_maps)]
    o = outs[0] - lam * outs[1]
    o_ref[...] = (_rms(o, subln_ref[...]) * (1.0 - lam_init)).astype(BF16)


def _attn_diag_bias(tq):
    slopes = 2.0 ** (-8.0 / A_HEADS * jnp.arange(1, A_HEADS + 1, dtype=F32))
    i = jnp.arange(tq)
    rel = (i[:, None] - jnp.abs(i[:, None] - i[None, :])).astype(F32)
    visible = (i[None, :] // CHUNK) <= (i[:, None] // CHUNK)
    bias = slopes[:, None, None] * rel[None] / math.log(2.0)
    return jnp.where(visible[None], bias, NEG_BIG)


def _diff_attention(za, lam_vecs, subln, lam_init, j):
    b, s, _ = za.shape
    tq = ATTN_Q_TILE if s % ATTN_Q_TILE == 0 else s
    tk = ATTN_K_TILE if tq % ATTN_K_TILE == 0 else tq
    kern = functools.partial(_attn_kernel, tq=tq, tk=tk, lam_init=lam_init)
    n_maps = 2
    return pl.pallas_call(
        kern,
        out_shape=jax.ShapeDtypeStruct((b, s, A_WIDTH), BF16),
        grid=(b, A_HEADS, s // tq),
        in_specs=[
            _resident((None, 4, A_QK_DIM), lambda bi, h, qi: (j, 0, 0)),
            _resident((None, 1, A_V_DIM), lambda bi, h, qi: (j, 0, 0)),
            pl.BlockSpec((None, tq, tq), lambda bi, h, qi: (h, 0, 0)),
            pl.BlockSpec((None, tq, A_V_DIM), lambda bi, h, qi: (bi, qi, h)),
            pl.BlockSpec((None, s, A_V_DIM), lambda bi, h, qi: (bi, 0, A_HEADS + h)),
            pl.BlockSpec((None, s, A_V_DIM), lambda bi, h, qi: (bi, 0, 2 * A_HEADS + h)),
        ],
        out_specs=pl.BlockSpec((None, tq, A_V_DIM), lambda bi, h, qi: (bi, qi, h)),
        scratch_shapes=[pltpu.VMEM((n_maps, s // tk, tq, tk), F32),
                        pltpu.VMEM((n_maps, tq, LANES), F32),
                        pltpu.VMEM((n_maps, tq, LANES), F32),
                        pltpu.VMEM((n_maps, tq, A_V_DIM), F32)],
        compiler_params=pltpu.CompilerParams(
            dimension_semantics=("parallel", "parallel", "parallel"),
            vmem_limit_bytes=VMEM_LIMIT),
        name="diff_attention",
    )(lam_vecs, subln, _attn_diag_bias(tq), za, za, za)


def _rwkv_kernel(*refs, tile, has_vres):
    (zb_ref, mu_ref, vec_ref, wup_ref, aup_ref, gup_ref, hsum_ref, ltri_ref, lall_ref) = refs[:9]
    if has_vres:
        vfirst_ref, v0_ref, vdown_ref, vup_ref, ob_ref = refs[9:14]
        scratch = refs[14:]
    else:
        ob_ref, vb_ref = refs[9:11]
        scratch = refs[11:]
    (state_ref, prev_ref, at_s, rt_s, bt_s, kt_s, be_s, ke_s, v_s, gam_s, y_s) = scratch

    @pl.when(pl.program_id(1) == 0)
    def _():
        state_ref[...] = jnp.zeros_like(state_ref)
        prev_ref[...] = jnp.zeros_like(prev_ref)

    z = zb_ref[...]
    row = lax.broadcasted_iota(jnp.int32, z.shape, 0)
    z_prev = jnp.where(row == 0, prev_ref[...], pltpu.roll(z, 1, axis=0))
    prev_ref[...] = z[tile - 1:tile, :]
    zs = z + (z_prev - z) * mu_ref[...]

    w3 = B_WIDTH
    r = zs[:, 0:w3]
    k = zs[:, w3:2 * w3]
    v = zs[:, 2 * w3:3 * w3]
    x_wa = zs[:, 3 * w3:3 * w3 + B_DECAY_RANK + B_A_RANK]
    x_g = zs[:, 3 * w3 + B_DECAY_RANK + B_A_RANK:]
    w0, a0, k_k, k_a, r_k, ln_w, ln_b = (vec_ref[i:i + 1, :] for i in range(7))

    d_in = -(w0 + _dot(jnp.tanh(x_wa).astype(BF16), wup_ref[...]))
    softplus = jnp.maximum(d_in, 0.0) + jnp.log(1.0 + jnp.exp(-jnp.abs(d_in)))
    log_decay = -jnp.exp(-softplus - 0.5)

    if has_vres:
        low = _dot(v.astype(BF16), vdown_ref[...])
        mixed = _dot(low.astype(BF16), vup_ref[...])
        v = v + (vfirst_ref[...] - v) * _sigmoid(v0_ref[...] + mixed)
    else:
        vb_ref[...] = v
    gate_a = _sigmoid(a0 + _dot(x_wa.astype(BF16), aup_ref[...]))
    gate_out = _dot(_sigmoid(x_g).astype(BF16), gup_ref[...])

    hsum = hsum_ref[...]
    kk = k * k_k
    kk = kk / jnp.maximum(jnp.sqrt(_head_sums(kk * kk, hsum, split=True)), 1e-12)
    k = k * (1.0 + (gate_a - 1.0) * k_a)
    b_vec = kk * gate_a

    cum = _ones_matmul_left(ltri_ref[...], log_decay)
    total = _ones_matmul_left(lall_ref[...], log_decay)
    g_inc = jnp.exp(cum)
    g_inv = jnp.exp(-cum)
    g_tail = jnp.exp(total - cum)
    rt_s[...] = r * g_inc
    kt_s[...] = k * g_inv
    bt_s[...] = b_vec * g_inv
    at_s[...] = -kk * jnp.exp(cum - log_decay)
    ke_s[...] = k * g_tail
    be_s[...] = b_vec * g_tail
    gam_s[...] = jnp.exp(total)
    v_s[...] = v

    row_c = lax.broadcasted_iota(jnp.int32, (CHUNK, 2 * CHUNK), 0)
    col_c = lax.broadcasted_iota(jnp.int32, (CHUNK, 2 * CHUNK), 1)
    k_half = col_c >= CHUNK
    col_local = jnp.where(k_half, col_c - CHUNK, col_c)
    strict = col_local < row_c
    incl = col_local <= row_c
    eye = jnp.where(lax.broadcasted_iota(jnp.int32, (CHUNK, CHUNK), 0)
                    == lax.broadcasted_iota(jnp.int32, (CHUNK, CHUNK), 1), 1.0, 0.0)
    zeros_half = jnp.zeros((CHUNK, B_HEAD_DIM), F32)

    def group_body(g, carry):
        pairs = [(ci, h) for ci in range(RWKV_GROUP) for h in range(B_HEADS)]
        rows = [pl.ds(pl.multiple_of((g * RWKV_GROUP + ci) * CHUNK, CHUNK), CHUNK)
                for ci in range(RWKV_GROUP)]
        load = lambda ref: [ref[rw, :] for rw in rows]
        a_all, r_all, bt_all, kt_all, be_all, ke_all, v_all = (
            load(ref) for ref in (at_s, rt_s, bt_s, kt_s, be_s, ke_s, v_s))
        gam_all = [gam_s[pl.ds(pl.multiple_of((g * RWKV_GROUP + ci) * CHUNK, CHUNK), 1), :]
                   for ci in range(RWKV_GROUP)]
        head = lambda arrs, ci, h: arrs[ci][:, h * B_HEAD_DIM:(h + 1) * B_HEAD_DIM]

        a_t, r_t, v_h, scores = {}, {}, {}, {}
        for p in pairs:
            a_t[p] = head(a_all, *p)
            r_t[p] = head(r_all, *p)
            v_h[p] = head(v_all, *p)
            lhs = jnp.concatenate([a_t[p], r_t[p]], axis=0).astype(BF16)
            rhs = jnp.concatenate([head(bt_all, *p), head(kt_all, *p)], axis=0).astype(BF16)
            scores[p] = _dot_nt(lhs, rhs)
        a_sc = {p: jnp.where(strict, scores[p][:CHUNK], 0.0) for p in pairs}
        r_sc = {p: jnp.where(incl, scores[p][CHUNK:], 0.0).astype(BF16) for p in pairs}

        power, inv, akv = {}, {}, {}
        for p in pairs:
            n = a_sc[p][:, :CHUNK]
            nb = n.astype(BF16)
            inv[p] = eye + n
            power[p] = _dot(nb, nb)
            ak_only = jnp.where(k_half, a_sc[p], 0.0).astype(BF16)
            akv[p] = _dot(ak_only, jnp.concatenate([v_h[p], v_h[p]], axis=0).astype(BF16))
        for _ in range(int(math.log2(CHUNK)) - 2):
            for p in pairs:
                pb = power[p].astype(BF16)
                both = _dot(jnp.concatenate([inv[p].astype(BF16), pb], axis=0), pb)
                inv[p] = inv[p] + both[:CHUNK]
                power[p] = both[CHUNK:]
        for p in pairs:
            inv[p] = inv[p] + _dot(inv[p].astype(BF16), power[p].astype(BF16))

        z_mat, g_mat, h_mat = {}, {}, {}
        for p in pairs:
            wu = _dot(inv[p].astype(BF16),
                      jnp.concatenate([a_t[p], akv[p]], axis=1).astype(BF16))
            z_mat[p] = jnp.concatenate(
                [wu, jnp.concatenate([zeros_half, v_h[p]], axis=1)], axis=0).astype(BF16)
        for p in pairs:
            g_mat[p] = _dot(r_sc[p], z_mat[p])
            bk_e = jnp.concatenate([head(be_all, *p), head(ke_all, *p)], axis=0).astype(BF16)
            h_mat[p] = _dot_tn(z_mat[p], bk_e)

        state = [state_ref[h] for h in range(B_HEADS)]
        y_rows = []
        for ci in range(RWKV_GROUP):
            y_heads = []
            for h in range(B_HEADS):
                p = (ci, h)
                s_b = state[h].astype(BF16)
                r_eff = (r_t[p] + g_mat[p][:, :B_HEAD_DIM]).astype(BF16)
                y_heads.append(_dot_nt(r_eff, s_b) + g_mat[p][:, B_HEAD_DIM:])
                state[h] = (state[h] * head(gam_all, ci, h)
                            + _dot(s_b, h_mat[p][:B_HEAD_DIM].astype(BF16)) + h_mat[p][B_HEAD_DIM:])
            y_rows.append(jnp.concatenate(y_heads, axis=1))
        for ci in range(RWKV_GROUP):
            y_s[rows[ci], :] = y_rows[ci]
        for h in range(B_HEADS):
            state_ref[h] = state[h]
        return carry

    lax.fori_loop(0, tile // (CHUNK * RWKV_GROUP), group_body, 0)

    y = y_s[...]
    inv_n = 1.0 / B_HEAD_DIM
    mean = _head_sums(y, hsum, split=True) * inv_n
    yc = y - mean
    var = _head_sums(yc * yc, hsum, split=False) * inv_n
    yn = yc * lax.rsqrt(var + B_LN_EPS) * ln_w + ln_b
    bonus = _head_sums(r * k * r_k, hsum, split=False) * v
    ob_ref[...] = ((yn + bonus) * gate_out).astype(BF16)


def _block_ones(n, block, lower):
    i = jnp.arange(n)
    same = (i[:, None] // block) == (i[None, :] // block)
    if lower:
        same = same & (i[None, :] <= i[:, None])
    return same.astype(BF16)


def _pad_rows(w, lo, total):
    return jnp.zeros((total, w.shape[1]), w.dtype).at[lo:lo + w.shape[0]].set(w)


def _rwkv(zb, mu, vec, w_up, a_up, g_up, v_first, vres):
    b, s, _ = zb.shape
    tile = _seq_tile(s)
    lora_in = B_DECAY_RANK + B_A_RANK
    wup_p = _pad_rows(w_up, 0, lora_in).astype(BF16)
    aup_p = _pad_rows(a_up, B_DECAY_RANK, lora_in).astype(BF16)
    consts = [mu.reshape(1, B_COLS), vec, wup_p, aup_p, g_up.astype(BF16),
              _block_ones(MXU_TILE, B_HEAD_DIM, False), _block_ones(tile, CHUNK, True),
              _block_ones(tile, CHUNK, False)]
    full = lambda a: _resident(a.shape, lambda bi, t: (0,) * a.ndim)
    seq = lambda width: pl.BlockSpec((None, tile, width), lambda bi, t: (bi, t, 0))
    in_specs = [seq(B_COLS)] + [full(a) for a in consts]
    args = [zb] + consts
    out_shape = [jax.ShapeDtypeStruct((b, s, B_WIDTH), BF16)]
    out_specs = [seq(B_WIDTH)]
    if vres is not None:
        v0, v_down, v_up = vres
        vdown_p = jnp.zeros((B_WIDTH, LANES), F32).at[:, :B_VRES_RANK].set(v_down).astype(BF16)
        vup_p = _pad_rows(v_up, 0, LANES).astype(BF16)
        extra = [v0.reshape(1, B_WIDTH), vdown_p, vup_p]
        in_specs += [seq(B_WIDTH)] + [full(a) for a in extra]
        args += [v_first] + extra
    else:
        out_shape.append(jax.ShapeDtypeStruct((b, s, B_WIDTH), F32))
        out_specs.append(seq(B_WIDTH))
    wide = lambda: pltpu.VMEM((tile, B_WIDTH), F32)
    kern = functools.partial(_rwkv_kernel, tile=tile, has_vres=vres is not None)
    outs = pl.pallas_call(
        kern,
        out_shape=tuple(out_shape),
        grid=(b, s // tile),
        in_specs=in_specs,
        out_specs=tuple(out_specs),
        scratch_shapes=[pltpu.VMEM((B_HEADS, B_HEAD_DIM, B_HEAD_DIM), F32),
                        pltpu.VMEM((1, B_COLS), F32)] + [wide() for _ in range(9)],
        compiler_params=pltpu.CompilerParams(
            dimension_semantics=("parallel", "arbitrary"), vmem_limit_bytes=VMEM_LIMIT),
        name="rwkv7",
    )(*args)
    return outs if vres is None else (outs[0], None)


def _retention_kernel(mask_ref, qdec_ref, kdec_ref, cdec_ref, q_ref, k_ref, v_ref, g_ref,
                      o_ref, state_ref):
    @pl.when(pl.program_id(1) == 0)
    def _():
        state_ref[...] = jnp.zeros_like(state_ref)

    heads = range(C_HEADS)
    qk_cols = lambda h: slice(h * C_QK_DIM, (h + 1) * C_QK_DIM)
    v_cols = lambda h: slice(h * C_V_DIM, (h + 1) * C_V_DIM)
    q = [q_ref[:, qk_cols(h)] for h in heads]
    k = [k_ref[:, qk_cols(h)] for h in heads]
    v = [v_ref[:, v_cols(h)] for h in heads]
    state = [state_ref[h] for h in heads]
    scores = [(_dot_nt(q[h], k[h]) * mask_ref[h]).astype(BF16) for h in heads]
    q_dec = [(q[h].astype(F32) * qdec_ref[h]).astype(BF16) for h in heads]
    y = [_dot(scores[h], v[h]) + _dot(q_dec[h], state[h].astype(BF16)) for h in heads]
    k_dec = [(k[h].astype(F32) * kdec_ref[h]).astype(BF16) for h in heads]
    for h in heads:
        state_ref[h] = state[h] * cdec_ref[h] + _dot_tn(k_dec[h], v[h])
    for h in heads:
        yn = y[h] * lax.rsqrt(jnp.mean(y[h] * y[h], axis=-1, keepdims=True) + NORM_EPS)
        g = g_ref[:, v_cols(h)]
        o_ref[:, v_cols(h)] = (g * _sigmoid(g) * yn).astype(BF16)


def _retention_tables(tile):
    log_g = jnp.log(1.0 - 2.0 ** (-5.0 - jnp.arange(C_HEADS, dtype=F32)))[:, None, None]
    i = jnp.arange(tile)
    dist = jnp.abs(i[:, None] - i[None, :]).astype(F32)
    visible = (i[None, :] // CHUNK) <= (i[:, None] // CHUNK)
    scale = C_QK_DIM ** -0.5
    mask = jnp.where(visible[None], jnp.exp(log_g * dist[None]), 0.0) * scale
    idx = i.astype(F32)[None, :, None]
    qdec = jnp.broadcast_to(jnp.exp(log_g * (idx + 1.0)), (C_HEADS, tile, C_QK_DIM))
    kdec = jnp.broadcast_to(jnp.exp(log_g * (tile - 1.0 - idx)) * scale, (C_HEADS, tile, C_QK_DIM))
    cdec = jnp.broadcast_to(jnp.exp(log_g * tile), (C_HEADS, 1, C_V_DIM))
    return mask, qdec, kdec, cdec


def _retention(qk, v, gate):
    b, s, _ = v.shape
    tile = _seq_tile(s)
    mask, qdec, kdec, cdec = _retention_tables(tile)
    return pl.pallas_call(
        _retention_kernel,
        out_shape=jax.ShapeDtypeStruct((b, s, C_V_WIDTH), BF16),
        grid=(b, s // tile),
        in_specs=[
            _resident(mask.shape, lambda bi, t: (0, 0, 0)),
            _resident(qdec.shape, lambda bi, t: (0, 0, 0)),
            _resident(kdec.shape, lambda bi, t: (0, 0, 0)),
            _resident(cdec.shape, lambda bi, t: (0, 0, 0)),
            pl.BlockSpec((None, tile, C_QK_WIDTH), lambda bi, t: (bi, t, 0)),
            pl.BlockSpec((None, tile, C_QK_WIDTH), lambda bi, t: (bi, t, 1)),
            pl.BlockSpec((None, tile, C_V_WIDTH), lambda bi, t: (bi, t, 0)),
            pl.BlockSpec((None, tile, C_V_WIDTH), lambda bi, t: (bi, t, 0)),
        ],
        out_specs=pl.BlockSpec((None, tile, C_V_WIDTH), lambda bi, t: (bi, t, 0)),
        scratch_shapes=[pltpu.VMEM((C_HEADS, C_QK_DIM, C_V_DIM), F32)],
        compiler_params=pltpu.CompilerParams(
            dimension_semantics=("parallel", "arbitrary"),
            vmem_limit_bytes=VMEM_LIMIT),
        name="retention",
    )(mask, qdec, kdec, cdec, qk, qk, v, gate)


def kernel(x, norms, ffn_wgu, ffn_wd, even_w_in, even_w_out, diff_lam, diff_subln, rwkv_mu, rwkv_vec, rwkv_w_up, rwkv_a_up, rwkv_g_up, rwkv_v0, rwkv_v_down, rwkv_v_up, odd_w_in, odd_w_out):
    b, s, d = x.shape
    m = b * s
    wgu = ffn_wgu.astype(BF16)
    wd = ffn_wd.astype(BF16)
    even_in = even_w_in.astype(BF16)
    even_out = even_w_out.astype(BF16)
    odd_in = odd_w_in.astype(BF16)
    odd_out = odd_w_out.astype(BF16)
    subln = diff_subln.reshape(-1, 1, A_V_DIM)

    x2 = x.reshape(m, d)
    v_first = None
    for i in range(DEPTH):
        j = i // 2
        x2 = _ffn(x2, norms, wgu, wd, i, 0)
        if i % 2 == 0:
            lam_init = 0.8 - 0.6 * math.exp(-0.3 * i)
            za, zb = _inproj_even(x2, norms, even_in, i, j)
            oa = _diff_attention(za.reshape(b, s, A_COLS), diff_lam, subln, lam_init, j)
            vres = None if j == 0 else (rwkv_v0[j - 1], rwkv_v_down[j - 1], rwkv_v_up[j - 1])
            ob, v_b = _rwkv(zb.reshape(b, s, B_COLS), rwkv_mu[j], rwkv_vec[j], rwkv_w_up[j],
                            rwkv_a_up[j], rwkv_g_up[j], v_first, vres)
            if j == 0:
                v_first = v_b
            parts, w_out = [oa.reshape(m, A_WIDTH), ob.reshape(m, B_WIDTH)], even_out
        else:
            qk, v, gate = _inproj_odd(x2, norms, odd_in, i, j)
            y = _retention(qk.reshape(b, s, -1), v.reshape(b, s, -1), gate.reshape(b, s, -1))
            parts, w_out = [y.reshape(m, C_V_WIDTH)], odd_out
        x2 = _ffn(x2, norms, wgu, wd, i, 1, parts, w_out, j)
    return x2.reshape(b, s, d)
```

```python
import functools
import math

import jax
import jax.numpy as jnp
from jax import lax
from jax.experimental import pallas as pl
from jax.experimental.pallas import tpu as pltpu

F32 = jnp.float32
BF16 = jnp.bfloat16

D_MODEL = 1024
DEPTH = 4
CHUNK = 64
D_FF = 2816
NORM_EPS = 1e-6

A_HEADS = 4
A_QK_DIM = 64
A_V_DIM = 2 * A_QK_DIM
A_WIDTH = A_HEADS * A_V_DIM
A_COLS = 3 * A_WIDTH

B_HEAD_DIM = 64
B_WIDTH = D_MODEL - A_WIDTH
B_HEADS = B_WIDTH // B_HEAD_DIM
B_DECAY_RANK = 64
B_A_RANK = 64
B_GATE_RANK = 128
B_VRES_RANK = 32
B_LN_EPS = 64e-5
B_COLS = 3 * B_WIDTH + B_DECAY_RANK + B_A_RANK + B_GATE_RANK
EVEN_IN = A_COLS + B_COLS

C_HEADS = 4
C_QK_DIM = D_MODEL // C_HEADS
C_V_DIM = 2 * C_QK_DIM
C_QK_WIDTH = C_HEADS * C_QK_DIM
C_V_WIDTH = C_HEADS * C_V_DIM
ODD_IN = 2 * C_QK_WIDTH + 2 * C_V_WIDTH

LANES = 128
MXU_TILE = 256
ROW_TILE = 512
FF_TILE = 256
SEQ_TILE = 256
ATTN_Q_TILE = 512
ATTN_K_TILE = 256
RWKV_GROUP = 4
VMEM_LIMIT = 52 * 1024 * 1024
NEG_BIG = -1e30


def _dot(a, b):
    return jnp.dot(a, b, preferred_element_type=F32)


def _dot_nt(a, b):
    return lax.dot_general(a, b, (((1,), (1,)), ((), ())), preferred_element_type=F32)


def _dot_tn(a, b):
    return lax.dot_general(a, b, (((0,), (0,)), ((), ())), preferred_element_type=F32)


def _split2(x):
    hi = x.astype(BF16)
    lo = (x - hi.astype(F32)).astype(BF16)
    return hi, lo


def _head_sums(x, ones_block, split):
    width = ones_block.shape[0]
    out = []
    for lo_col in range(0, x.shape[1], width):
        xs = x[:, lo_col:lo_col + width]
        if split:
            hi, lo = _split2(xs)
            out.append(_dot(hi, ones_block) + _dot(lo, ones_block))
        else:
            out.append(_dot(xs.astype(BF16), ones_block))
    return jnp.concatenate(out, axis=1)


def _ones_matmul_left(m, x):
    hi, lo = _split2(x)
    return _dot(m, hi) + _dot(m, lo)


def _rms(x, g, eps=NORM_EPS):
    return x * lax.rsqrt(jnp.mean(x * x, axis=-1, keepdims=True) + eps) * g


def _sigmoid(x):
    return 1.0 / (1.0 + jnp.exp(-x))


def _row_tile(m):
    return ROW_TILE if m % ROW_TILE == 0 else m


def _seq_tile(s):
    return SEQ_TILE if s % SEQ_TILE == 0 else s


def _resident(block_shape, index_map):
    return pl.BlockSpec(block_shape, index_map, pipeline_mode=pl.Buffered(1))


def _ffn_kernel(*refs, g_pre, g_post, n_parts):
    x_ref, g_ref, wgu_ref, wd_ref = refs[:4]
    part_refs = refs[4:4 + n_parts]
    o_ref, act_ref = refs[-2:]
    x = x_ref[...]
    if n_parts:
        wout_ref = refs[4 + n_parts]
        mix = None
        lo = 0
        for p in part_refs:
            width = p.shape[-1]
            term = _dot(p[...], wout_ref[lo:lo + width, :])
            mix = term if mix is None else mix + term
            lo += width
        x = x + _rms(mix, g_ref[3:4, :])
    h = _rms(x, g_ref[g_pre:g_pre + 1, :]).astype(BF16)
    for j in range(D_FF // FF_TILE):
        lo = j * FF_TILE
        gate = _dot(h, wgu_ref[:, lo:lo + FF_TILE])
        up = _dot(h, wgu_ref[:, D_FF + lo:D_FF + lo + FF_TILE])
        act_ref[:, lo:lo + FF_TILE] = (gate * _sigmoid(gate) * up).astype(BF16)
    y = _dot(act_ref[...], wd_ref[...])
    o_ref[...] = x + 0.5 * _rms(y, g_ref[g_post:g_post + 1, :])


def _ffn(x2, norms, wgu, wd, layer, which, parts=(), w_out=None, j=0):
    m, d = x2.shape
    tm = _row_tile(m)
    kern = functools.partial(_ffn_kernel, g_pre=4 * which, g_post=4 * which + 1,
                             n_parts=len(parts))
    in_specs = [
        pl.BlockSpec((tm, d), lambda i: (i, 0)),
        _resident((None, 6, d), lambda i: (layer, 0, 0)),
        _resident((None, None, d, 2 * D_FF), lambda i: (layer, which, 0, 0)),
        _resident((None, None, D_FF, d), lambda i: (layer, which, 0, 0)),
    ]
    args = [x2, norms, wgu, wd]
    if parts:
        k_total = sum(p.shape[-1] for p in parts)
        in_specs += [pl.BlockSpec((tm, p.shape[-1]), lambda i: (i, 0)) for p in parts]
        in_specs.append(_resident((None, k_total, d), lambda i: (j, 0, 0)))
        args += list(parts) + [w_out]
    return pl.pallas_call(
        kern,
        out_shape=jax.ShapeDtypeStruct((m, d), F32),
        grid=(m // tm,),
        in_specs=in_specs,
        out_specs=pl.BlockSpec((tm, d), lambda i: (i, 0)),
        scratch_shapes=[pltpu.VMEM((tm, D_FF), BF16)],
        compiler_params=pltpu.CompilerParams(
            dimension_semantics=("parallel",), vmem_limit_bytes=VMEM_LIMIT),
        name="ffn_post" if parts else "ffn",
    )(*args)


def _inproj_even_kernel(x_ref, g_ref, w_ref, za_ref, zb_ref):
    h = _rms(x_ref[...], g_ref[2:3, :]).astype(BF16)
    za_ref[...] = _dot(h, w_ref[:, :A_COLS]).astype(BF16)
    zb_ref[...] = _dot(h, w_ref[:, A_COLS:])


def _inproj_even(x2, norms, w_in, layer, j):
    m, d = x2.shape
    tm = _row_tile(m)
    return pl.pallas_call(
        _inproj_even_kernel,
        out_shape=(jax.ShapeDtypeStruct((m, A_COLS), BF16),
                   jax.ShapeDtypeStruct((m, B_COLS), F32)),
        grid=(m // tm,),
        in_specs=[
            pl.BlockSpec((tm, d), lambda i: (i, 0)),
            _resident((None, 6, d), lambda i: (layer, 0, 0)),
            _resident((None, d, EVEN_IN), lambda i: (j, 0, 0)),
        ],
        out_specs=(pl.BlockSpec((tm, A_COLS), lambda i: (i, 0)),
                   pl.BlockSpec((tm, B_COLS), lambda i: (i, 0))),
        compiler_params=pltpu.CompilerParams(
            dimension_semantics=("parallel",), vmem_limit_bytes=VMEM_LIMIT),
        name="inproj_even",
    )(x2, norms, w_in)


def _inproj_odd_kernel(x_ref, g_ref, w_ref, qk_ref, v_ref, gate_ref):
    h = _rms(x_ref[...], g_ref[2:3, :]).astype(BF16)
    qk_ref[...] = _dot(h, w_ref[:, :2 * C_QK_WIDTH]).astype(BF16)
    v_ref[...] = _dot(h, w_ref[:, 2 * C_QK_WIDTH:2 * C_QK_WIDTH + C_V_WIDTH]).astype(BF16)
    gate_ref[...] = _dot(h, w_ref[:, 2 * C_QK_WIDTH + C_V_WIDTH:])


def _inproj_odd(x2, norms, w_in, layer, j):
    m, d = x2.shape
    tm = _row_tile(m)
    return pl.pallas_call(
        _inproj_odd_kernel,
        out_shape=(jax.ShapeDtypeStruct((m, 2 * C_QK_WIDTH), BF16),
                   jax.ShapeDtypeStruct((m, C_V_WIDTH), BF16),
                   jax.ShapeDtypeStruct((m, C_V_WIDTH), F32)),
        grid=(m // tm,),
        in_specs=[
            pl.BlockSpec((tm, d), lambda i: (i, 0)),
            _resident((None, 6, d), lambda i: (layer, 0, 0)),
            _resident((None, d, ODD_IN), lambda i: (j, 0, 0)),
        ],
        out_specs=(pl.BlockSpec((tm, 2 * C_QK_WIDTH), lambda i: (i, 0)),
                   pl.BlockSpec((tm, C_V_WIDTH), lambda i: (i, 0)),
                   pl.BlockSpec((tm, C_V_WIDTH), lambda i: (i, 0))),
        compiler_params=pltpu.CompilerParams(
            dimension_semantics=("parallel",), vmem_limit_bytes=VMEM_LIMIT),
        name="inproj_odd",
    )(x2, norms, w_in)


def _lane_tiles(x):
    return [x[:, i:i + LANES] for i in range(0, x.shape[1], LANES)]


def _attn_kernel(lam_ref, subln_ref, dbias_ref, q_ref, k_ref, v_ref, o_ref,
                 s_ref, m_ref, l_ref, acc_ref, *, tq, tk, lam_init):
    head = pl.program_id(1)
    qi = pl.program_id(2)
    n_diag = tq // tk
    first_diag = qi * n_diag
    log2e = 1.0 / math.log(2.0)
    slope2 = log2e * jnp.exp(
        (head + 1).astype(F32) * jnp.full((1, 1), -8.0 / A_HEADS * math.log(2.0), F32))

    lane = lax.broadcasted_iota(jnp.int32, (1, A_V_DIM), 1)
    q = q_ref[...] * jnp.asarray(A_QK_DIM ** -0.5, BF16)
    zero = jnp.zeros_like(q)
    q_maps = (jnp.where(lane < A_QK_DIM, q, zero), jnp.where(lane >= A_QK_DIM, q, zero))
    n_maps = len(q_maps)
    col_row = lax.broadcasted_iota(jnp.int32, (1, tk), 1)

    def lane_max(m, s):
        for t in _lane_tiles(s):
            m = t if m is None else jnp.maximum(m, t)
        return m

    def pass1(blocks, first):
        m_new = [None if first else m_ref[mi] for mi in range(n_maps)]
        for kb, diag_slot in blocks:
            k = k_ref[pl.ds(pl.multiple_of(kb * tk, tk), tk), :]
            if diag_slot is None:
                bias = slope2 * (col_row + (kb * tk - qi * tq)).astype(F32)
            else:
                bias = dbias_ref[:, diag_slot * tk:(diag_slot + 1) * tk]
            for mi, qm in enumerate(q_maps):
                s = _dot_nt(qm, k) * log2e + bias
                s_ref[mi, kb] = s
                m_new[mi] = lane_max(m_new[mi], s)
        for mi in range(n_maps):
            m_ref[mi] = m_new[mi]

    def pass2(blocks, first):
        for mi in range(n_maps):
            m_row = m_ref[mi]
            l_new = None if first else l_ref[mi]
            acc_new = None if first else acc_ref[mi]
            for kb in blocks:
                tiles = [jnp.exp2(t - m_row) for t in _lane_tiles(s_ref[mi, kb])]
                for t in tiles:
                    l_new = t if l_new is None else l_new + t
                pv = _dot(jnp.concatenate(tiles, axis=1).astype(BF16),
                          v_ref[pl.ds(pl.multiple_of(kb * tk, tk), tk), :])
                acc_new = pv if acc_new is None else acc_new + pv
            l_ref[mi] = l_new
            acc_ref[mi] = acc_new

    def visible_loops(fn):
        done = 0
        for size in (4, 2, 1):
            trips = (first_diag - done) // size
            base = done

            def body(it, carry, size=size, base=base):
                fn([base + it * size + g for g in range(size)])
                return carry

            lax.fori_loop(0, trips, body, 0)
            done = done + trips * size

    pass1([(first_diag + d, d) for d in range(n_diag)], True)
    visible_loops(lambda kbs: pass1([(kb, None) for kb in kbs], False))
    for mi in range(n_maps):
        m_ref[mi] = jnp.broadcast_to(jnp.max(m_ref[mi], axis=-1, keepdims=True), (tq, LANES))
    pass2([first_diag + d for d in range(n_diag)], True)
    visible_loops(lambda kbs: pass2(kbs, False))

    lv = lam_ref[...]
    lam = (jnp.exp(jnp.sum(lv[0:1] * lv[1:2], axis=-1, keepdims=True))
           - jnp.exp(jnp.sum(lv[2:3] * lv[3:4], axis=-1, keepdims=True)) + lam_init)
    outs = [acc_ref[mi] * (1.0 / jnp.sum(l_ref[mi], axis=-1, keepdims=True))
            for mi in range(n_maps)]
    o = outs[0] - lam * outs[1]
    o_ref[...] = (_rms(o, subln_ref[...]) * (1.0 - lam_init)).astype(BF16)


def _attn_diag_bias(tq):
    slopes = 2.0 ** (-8.0 / A_HEADS * jnp.arange(1, A_HEADS + 1, dtype=F32))
    i = jnp.arange(tq)
    rel = (i[:, None] - jnp.abs(i[:, None] - i[None, :])).astype(F32)
    visible = (i[None, :] // CHUNK) <= (i[:, None] // CHUNK)
    bias = slopes[:, None, None] * rel[None] / math.log(2.0)
    return jnp.where(visible[None], bias, NEG_BIG)


def _diff_attention(za, lam_vecs, subln, lam_init, j):
    b, s, _ = za.shape
    tq = ATTN_Q_TILE if s % ATTN_Q_TILE == 0 else s
    tk = ATTN_K_TILE if tq % ATTN_K_TILE == 0 else tq
    kern = functools.partial(_attn_kernel, tq=tq, tk=tk, lam_init=lam_init)
    n_maps = 2
    return pl.pallas_call(
        kern,
        out_shape=jax.ShapeDtypeStruct((b, s, A_WIDTH), BF16),
        grid=(b, A_HEADS, s // tq),
        in_specs=[
            _resident((None, 4, A_QK_DIM), lambda bi, h, qi: (j, 0, 0)),
            _resident((None, 1, A_V_DIM), lambda bi, h, qi: (j, 0, 0)),
            pl.BlockSpec((None, tq, tq), lambda bi, h, qi: (h, 0, 0)),
            pl.BlockSpec((None, tq, A_V_DIM), lambda bi, h, qi: (bi, qi, h)),
            pl.BlockSpec((None, s, A_V_DIM), lambda bi, h, qi: (bi, 0, A_HEADS + h)),
            pl.BlockSpec((None, s, A_V_DIM), lambda bi, h, qi: (bi, 0, 2 * A_HEADS + h)),
        ],
        out_specs=pl.BlockSpec((None, tq, A_V_DIM), lambda bi, h, qi: (bi, qi, h)),
        scratch_shapes=[pltpu.VMEM((n_maps, s // tk, tq, tk), F32),
                        pltpu.VMEM((n_maps, tq, LANES), F32),
                        pltpu.VMEM((n_maps, tq, LANES), F32),
                        pltpu.VMEM((n_maps, tq, A_V_DIM), F32)],
        compiler_params=pltpu.CompilerParams(
            dimension_semantics=("parallel", "parallel", "parallel"),
            vmem_limit_bytes=VMEM_LIMIT),
        name="diff_attention",
    )(lam_vecs, subln, _attn_diag_bias(tq), za, za, za)


def _rwkv_kernel(*refs, tile, has_vres):
    (zb_ref, mu_ref, vec_ref, wup_ref, aup_ref, gup_ref, hsum_ref, ltri_ref, lall_ref) = refs[:9]
    if has_vres:
        vfirst_ref, v0_ref, vdown_ref, vup_ref, ob_ref = refs[9:14]
        scratch = refs[14:]
    else:
        ob_ref, vb_ref = refs[9:11]
        scratch = refs[11:]
    (state_ref, prev_ref, at_s, rt_s, bt_s, kt_s, be_s, ke_s, v_s, gam_s, y_s) = scratch

    @pl.when(pl.program_id(1) == 0)
    def _():
        state_ref[...] = jnp.zeros_like(state_ref)
        prev_ref[...] = jnp.zeros_like(prev_ref)

    z = zb_ref[...]
    row = lax.broadcasted_iota(jnp.int32, z.shape, 0)
    z_prev = jnp.where(row == 0, prev_ref[...], pltpu.roll(z, 1, axis=0))
    prev_ref[...] = z[tile - 1:tile, :]
    zs = z + (z_prev - z) * mu_ref[...]

    w3 = B_WIDTH
    r = zs[:, 0:w3]
    k = zs[:, w3:2 * w3]
    v = zs[:, 2 * w3:3 * w3]
    x_wa = zs[:, 3 * w3:3 * w3 + B_DECAY_RANK + B_A_RANK]
    x_g = zs[:, 3 * w3 + B_DECAY_RANK + B_A_RANK:]
    w0, a0, k_k, k_a, r_k, ln_w, ln_b = (vec_ref[i:i + 1, :] for i in range(7))

    d_in = -(w0 + _dot(jnp.tanh(x_wa).astype(BF16), wup_ref[...]))
    softplus = jnp.maximum(d_in, 0.0) + jnp.log(1.0 + jnp.exp(-jnp.abs(d_in)))
    log_decay = -jnp.exp(-softplus - 0.5)

    if has_vres:
        low = _dot(v.astype(BF16), vdown_ref[...])
        mixed = _dot(low.astype(BF16), vup_ref[...])
        v = v + (vfirst_ref[...] - v) * _sigmoid(v0_ref[...] + mixed)
    else:
        vb_ref[...] = v
    gate_a = _sigmoid(a0 + _dot(x_wa.astype(BF16), aup_ref[...]))
    gate_out = _dot(_sigmoid(x_g).astype(BF16), gup_ref[...])

    hsum = hsum_ref[...]
    kk = k * k_k
    kk = kk / jnp.maximum(jnp.sqrt(_head_sums(kk * kk, hsum, split=True)), 1e-12)
    k = k * (1.0 + (gate_a - 1.0) * k_a)
    b_vec = kk * gate_a

    cum = _ones_matmul_left(ltri_ref[...], log_decay)
    total = _ones_matmul_left(lall_ref[...], log_decay)
    g_inc = jnp.exp(cum)
    g_inv = jnp.exp(-cum)
    g_tail = jnp.exp(total - cum)
    rt_s[...] = r * g_inc
    kt_s[...] = k * g_inv
    bt_s[...] = b_vec * g_inv
    at_s[...] = -kk * jnp.exp(cum - log_decay)
    ke_s[...] = k * g_tail
    be_s[...] = b_vec * g_tail
    gam_s[...] = jnp.exp(total)
    v_s[...] = v

    assert B_HEAD_DIM == CHUNK and 2 * B_HEAD_DIM == LANES
    pw = 2 * B_HEAD_DIM
    n_pairs = B_HEADS // 2
    lane_p = lax.broadcasted_iota(jnp.int32, (CHUNK, pw), 1)
    row_p = lax.broadcasted_iota(jnp.int32, (CHUNK, pw), 0)
    first_head = lane_p < B_HEAD_DIM
    eye = jnp.where(row_p == (lane_p & (B_HEAD_DIM - 1)), 1.0, 0.0)
    lane_2 = lax.broadcasted_iota(jnp.int32, (CHUNK, 2 * pw), 1)
    row_2 = lax.broadcasted_iota(jnp.int32, (CHUNK, 2 * pw), 0)
    strict = (lane_2 & (B_HEAD_DIM - 1)) < row_2
    incl = (lane_2 & (B_HEAD_DIM - 1)) <= row_2
    same_head = ((lax.broadcasted_iota(jnp.int32, (pw, pw), 0) < B_HEAD_DIM)
                 == (lax.broadcasted_iota(jnp.int32, (pw, pw), 1) < B_HEAD_DIM))
    zeros_pair = jnp.zeros((CHUNK, pw), F32)
    zeros_bd = jnp.zeros((pw, pw), BF16)

    def block_diag(x):
        return jnp.concatenate(
            [jnp.where(first_head, x, 0.0), jnp.where(first_head, 0.0, x)], axis=0).astype(BF16)

    def group_body(g, carry):
        pairs = [(ci, pr) for ci in range(RWKV_GROUP) for pr in range(n_pairs)]
        rows = [pl.ds(pl.multiple_of((g * RWKV_GROUP + ci) * CHUNK, CHUNK), CHUNK)
                for ci in range(RWKV_GROUP)]
        tile_of = lambda ref, p: ref[rows[p[0]], p[1] * pw:(p[1] + 1) * pw]

        a_t, r_t, v_p, a_sc, r_sc = {}, {}, {}, {}, {}
        for p in pairs:
            a_t[p] = tile_of(at_s, p)
            r_t[p] = tile_of(rt_s, p)
            v_p[p] = tile_of(v_s, p)
            lhs = jnp.concatenate([a_t[p], r_t[p]], axis=0).astype(BF16)
            rhs = jnp.concatenate([block_diag(tile_of(bt_s, p)), block_diag(tile_of(kt_s, p))],
                                  axis=0)
            scores = _dot_nt(lhs, rhs)
            a_sc[p] = jnp.where(strict, scores[:CHUNK], 0.0)
            r_sc[p] = jnp.where(incl, scores[CHUNK:], 0.0).astype(BF16)

        power, inv, akv, v_bd = {}, {}, {}, {}
        for p in pairs:
            n = a_sc[p][:, :pw]
            inv[p] = eye + n
            power[p] = _dot(n.astype(BF16), block_diag(n))
            v_bd[p] = block_diag(v_p[p])
            akv[p] = _dot(a_sc[p][:, pw:].astype(BF16), v_bd[p])
        for _ in range(int(math.log2(CHUNK)) - 2):
            for p in pairs:
                both = _dot(jnp.concatenate([inv[p], power[p]], axis=0).astype(BF16),
                            block_diag(power[p]))
                inv[p] = inv[p] + both[:CHUNK]
                power[p] = both[CHUNK:]
        for p in pairs:
            inv[p] = inv[p] + _dot(inv[p].astype(BF16), block_diag(power[p]))

        wu, g_mat, h_mat = {}, {}, {}
        for p in pairs:
            wu[p] = _dot(inv[p].astype(BF16),
                         jnp.concatenate([block_diag(a_t[p]), block_diag(akv[p])], axis=1))
        for p in pairs:
            z_bd = jnp.concatenate(
                [jnp.concatenate([block_diag(wu[p][:, :pw]), block_diag(wu[p][:, pw:])], axis=1),
                 jnp.concatenate([zeros_bd, v_bd[p]], axis=1)], axis=0)
            g_mat[p] = _dot(r_sc[p], z_bd)
        for p in pairs:
            z_rows = jnp.concatenate(
                [wu[p], jnp.concatenate([zeros_pair, v_p[p]], axis=1)], axis=0).astype(BF16)
            bk_e = jnp.concatenate([tile_of(be_s, p), tile_of(ke_s, p)], axis=0).astype(BF16)
            h_mat[p] = _dot_tn(z_rows, bk_e)

        state = [state_ref[pr] for pr in range(n_pairs)]
        y_rows = []
        for ci in range(RWKV_GROUP):
            y_pairs = []
            for pr in range(n_pairs):
                p = (ci, pr)
                r_eff = (r_t[p] + g_mat[p][:, :pw]).astype(BF16)
                y_pairs.append(_dot_nt(r_eff, block_diag(state[pr])) + g_mat[p][:, pw:])
                trans = jnp.where(same_head, h_mat[p][:pw], 0.0).astype(BF16)
                q_full = h_mat[p][pw:]
                gam = gam_s[pl.ds(pl.multiple_of((g * RWKV_GROUP + ci) * CHUNK, CHUNK), 1),
                            pr * pw:(pr + 1) * pw]
                state[pr] = (state[pr] * gam + _dot(state[pr].astype(BF16), trans)
                             + jnp.where(first_head, q_full[:B_HEAD_DIM], q_full[B_HEAD_DIM:]))
            y_rows.append(jnp.concatenate(y_pairs, axis=1))
        for ci in range(RWKV_GROUP):
            y_s[rows[ci], :] = y_rows[ci]
        for pr in range(n_pairs):
            state_ref[pr] = state[pr]
        return carry

    lax.fori_loop(0, tile // (CHUNK * RWKV_GROUP), group_body, 0)

    y = y_s[...]
    inv_n = 1.0 / B_HEAD_DIM
    mean = _head_sums(y, hsum, split=True) * inv_n
    yc = y - mean
    var = _head_sums(yc * yc, hsum, split=False) * inv_n
    yn = yc * lax.rsqrt(var + B_LN_EPS) * ln_w + ln_b
    bonus = _head_sums(r * k * r_k, hsum, split=False) * v
    ob_ref[...] = ((yn + bonus) * gate_out).astype(BF16)


def _block_ones(n, block, lower):
    i = jnp.arange(n)
    same = (i[:, None] // block) == (i[None, :] // block)
    if lower:
        same = same & (i[None, :] <= i[:, None])
    return same.astype(BF16)


def _pad_rows(w, lo, total):
    return jnp.zeros((total, w.shape[1]), w.dtype).at[lo:lo + w.shape[0]].set(w)


def _rwkv(zb, mu, vec, w_up, a_up, g_up, v_first, vres):
    b, s, _ = zb.shape
    tile = _seq_tile(s)
    lora_in = B_DECAY_RANK + B_A_RANK
    wup_p = _pad_rows(w_up, 0, lora_in).astype(BF16)
    aup_p = _pad_rows(a_up, B_DECAY_RANK, lora_in).astype(BF16)
    consts = [mu.reshape(1, B_COLS), vec, wup_p, aup_p, g_up.astype(BF16),
              _block_ones(MXU_TILE, B_HEAD_DIM, False), _block_ones(tile, CHUNK, True),
              _block_ones(tile, CHUNK, False)]
    full = lambda a: _resident(a.shape, lambda bi, t: (0,) * a.ndim)
    seq = lambda width: pl.BlockSpec((None, tile, width), lambda bi, t: (bi, t, 0))
    in_specs = [seq(B_COLS)] + [full(a) for a in consts]
    args = [zb] + consts
    out_shape = [jax.ShapeDtypeStruct((b, s, B_WIDTH), BF16)]
    out_specs = [seq(B_WIDTH)]
    if vres is not None:
        v0, v_down, v_up = vres
        vdown_p = jnp.zeros((B_WIDTH, LANES), F32).at[:, :B_VRES_RANK].set(v_down).astype(BF16)
        vup_p = _pad_rows(v_up, 0, LANES).astype(BF16)
        extra = [v0.reshape(1, B_WIDTH), vdown_p, vup_p]
        in_specs += [seq(B_WIDTH)] + [full(a) for a in extra]
        args += [v_first] + extra
    else:
        out_shape.append(jax.ShapeDtypeStruct((b, s, B_WIDTH), F32))
        out_specs.append(seq(B_WIDTH))
    wide = lambda: pltpu.VMEM((tile, B_WIDTH), F32)
    kern = functools.partial(_rwkv_kernel, tile=tile, has_vres=vres is not None)
    outs = pl.pallas_call(
        kern,
        out_shape=tuple(out_shape),
        grid=(b, s // tile),
        in_specs=in_specs,
        out_specs=tuple(out_specs),
        scratch_shapes=[pltpu.VMEM((B_HEADS // 2, B_HEAD_DIM, 2 * B_HEAD_DIM), F32),
                        pltpu.VMEM((1, B_COLS), F32)] + [wide() for _ in range(9)],
        compiler_params=pltpu.CompilerParams(
            dimension_semantics=("parallel", "arbitrary"), vmem_limit_bytes=VMEM_LIMIT),
        name="rwkv7",
    )(*args)
    return outs if vres is None else (outs[0], None)


def _retention_kernel(mask_ref, qdec_ref, kdec_ref, cdec_ref, q_ref, k_ref, v_ref, g_ref,
                      o_ref, state_ref):
    @pl.when(pl.program_id(1) == 0)
    def _():
        state_ref[...] = jnp.zeros_like(state_ref)

    heads = range(C_HEADS)
    qk_cols = lambda h: slice(h * C_QK_DIM, (h + 1) * C_QK_DIM)
    v_cols = lambda h: slice(h * C_V_DIM, (h + 1) * C_V_DIM)
    q = [q_ref[:, qk_cols(h)] for h in heads]
    k = [k_ref[:, qk_cols(h)] for h in heads]
    v = [v_ref[:, v_cols(h)] for h in heads]
    state = [state_ref[h] for h in heads]
    scores = [(_dot_nt(q[h], k[h]) * mask_ref[h]).astype(BF16) for h in heads]
    q_dec = [(q[h].astype(F32) * qdec_ref[h]).astype(BF16) for h in heads]
    y = [_dot(scores[h], v[h]) + _dot(q_dec[h], state[h].astype(BF16)) for h in heads]
    k_dec = [(k[h].astype(F32) * kdec_ref[h]).astype(BF16) for h in heads]
    for h in heads:
        state_ref[h] = state[h] * cdec_ref[h] + _dot_tn(k_dec[h], v[h])
    for h in heads:
        yn = y[h] * lax.rsqrt(jnp.mean(y[h] * y[h], axis=-1, keepdims=True) + NORM_EPS)
        g = g_ref[:, v_cols(h)]
        o_ref[:, v_cols(h)] = (g * _sigmoid(g) * yn).astype(BF16)


def _retention_tables(tile):
    log_g = jnp.log(1.0 - 2.0 ** (-5.0 - jnp.arange(C_HEADS, dtype=F32)))[:, None, None]
    i = jnp.arange(tile)
    dist = jnp.abs(i[:, None] - i[None, :]).astype(F32)
    visible = (i[None, :] // CHUNK) <= (i[:, None] // CHUNK)
    scale = C_QK_DIM ** -0.5
    mask = jnp.where(visible[None], jnp.exp(log_g * dist[None]), 0.0) * scale
    idx = i.astype(F32)[None, :, None]
    qdec = jnp.broadcast_to(jnp.exp(log_g * (idx + 1.0)), (C_HEADS, tile, C_QK_DIM))
    kdec = jnp.broadcast_to(jnp.exp(log_g * (tile - 1.0 - idx)) * scale, (C_HEADS, tile, C_QK_DIM))
    cdec = jnp.broadcast_to(jnp.exp(log_g * tile), (C_HEADS, 1, C_V_DIM))
    return mask, qdec, kdec, cdec


def _retention(qk, v, gate):
    b, s, _ = v.shape
    tile = _seq_tile(s)
    mask, qdec, kdec, cdec = _retention_tables(tile)
    return pl.pallas_call(
        _retention_kernel,
        out_shape=jax.ShapeDtypeStruct((b, s, C_V_WIDTH), BF16),
        grid=(b, s // tile),
        in_specs=[
            _resident(mask.shape, lambda bi, t: (0, 0, 0)),
            _resident(qdec.shape, lambda bi, t: (0, 0, 0)),
            _resident(kdec.shape, lambda bi, t: (0, 0, 0)),
            _resident(cdec.shape, lambda bi, t: (0, 0, 0)),
            pl.BlockSpec((None, tile, C_QK_WIDTH), lambda bi, t: (bi, t, 0)),
            pl.BlockSpec((None, tile, C_QK_WIDTH), lambda bi, t: (bi, t, 1)),
            pl.BlockSpec((None, tile, C_V_WIDTH), lambda bi, t: (bi, t, 0)),
            pl.BlockSpec((None, tile, C_V_WIDTH), lambda bi, t: (bi, t, 0)),
        ],
        out_specs=pl.BlockSpec((None, tile, C_V_WIDTH), lambda bi, t: (bi, t, 0)),
        scratch_shapes=[pltpu.VMEM((C_HEADS, C_QK_DIM, C_V_DIM), F32)],
        compiler_params=pltpu.CompilerParams(
            dimension_semantics=("parallel", "arbitrary"),
            vmem_limit_bytes=VMEM_LIMIT),
        name="retention",
    )(mask, qdec, kdec, cdec, qk, qk, v, gate)


def kernel(x, norms, ffn_wgu, ffn_wd, even_w_in, even_w_out, diff_lam, diff_subln, rwkv_mu, rwkv_vec, rwkv_w_up, rwkv_a_up, rwkv_g_up, rwkv_v0, rwkv_v_down, rwkv_v_up, odd_w_in, odd_w_out):
    b, s, d = x.shape
    m = b * s
    wgu = ffn_wgu.astype(BF16)
    wd = ffn_wd.astype(BF16)
    even_in = even_w_in.astype(BF16)
    even_out = even_w_out.astype(BF16)
    odd_in = odd_w_in.astype(BF16)
    odd_out = odd_w_out.astype(BF16)
    subln = diff_subln.reshape(-1, 1, A_V_DIM)

    x2 = x.reshape(m, d)
    v_first = None
    for i in range(DEPTH):
        j = i // 2
        x2 = _ffn(x2, norms, wgu, wd, i, 0)
        if i % 2 == 0:
            lam_init = 0.8 - 0.6 * math.exp(-0.3 * i)
            za, zb = _inproj_even(x2, norms, even_in, i, j)
            oa = _diff_attention(za.reshape(b, s, A_COLS), diff_lam, subln, lam_init, j)
            vres = None if j == 0 else (rwkv_v0[j - 1], rwkv_v_down[j - 1], rwkv_v_up[j - 1])
            ob, v_b = _rwkv(zb.reshape(b, s, B_COLS), rwkv_mu[j], rwkv_vec[j], rwkv_w_up[j],
                            rwkv_a_up[j], rwkv_g_up[j], v_first, vres)
            if j == 0:
                v_first = v_b
            parts, w_out = [oa.reshape(m, A_WIDTH), ob.reshape(m, B_WIDTH)], even_out
        else:
            qk, v, gate = _inproj_odd(x2, norms, odd_in, i, j)
            y = _retention(qk.reshape(b, s, -1), v.reshape(b, s, -1), gate.reshape(b, s, -1))
            parts, w_out = [y.reshape(m, C_V_WIDTH)], odd_out
        x2 = _ffn(x2, norms, wgu, wd, i, 1, parts, w_out, j)
    return x2.reshape(b, s, d)
```

```python
import functools
import math

import jax
import jax.numpy as jnp
from jax import lax
from jax.experimental import pallas as pl
from jax.experimental.pallas import tpu as pltpu

F32 = jnp.float32
BF16 = jnp.bfloat16

D_MODEL = 1024
DEPTH = 4
CHUNK = 64
D_FF = 2816
NORM_EPS = 1e-6

A_HEADS = 4
A_QK_DIM = 64
A_V_DIM = 2 * A_QK_DIM
A_WIDTH = A_HEADS * A_V_DIM
A_COLS = 3 * A_WIDTH

B_HEAD_DIM = 64
B_WIDTH = D_MODEL - A_WIDTH
B_HEADS = B_WIDTH // B_HEAD_DIM
B_DECAY_RANK = 64
B_A_RANK = 64
B_GATE_RANK = 128
B_VRES_RANK = 32
B_LN_EPS = 64e-5
B_COLS = 3 * B_WIDTH + B_DECAY_RANK + B_A_RANK + B_GATE_RANK
EVEN_IN = A_COLS + B_COLS

C_HEADS = 4
C_QK_DIM = D_MODEL // C_HEADS
C_V_DIM = 2 * C_QK_DIM
C_QK_WIDTH = C_HEADS * C_QK_DIM
C_V_WIDTH = C_HEADS * C_V_DIM
ODD_IN = 2 * C_QK_WIDTH + 2 * C_V_WIDTH

LANES = 128
MXU_TILE = 256
ROW_TILE = 512
FFN_ROW_TILE = 1024
FFN_POST_ROW_TILE = 512
FFN_SUBTILES = 2
FF_TILE = 256
SEQ_TILE = 256
ATTN_Q_TILE = 512
ATTN_K_TILE = 256
RWKV_GROUP = 4
VMEM_LIMIT = 52 * 1024 * 1024
NEG_BIG = -1e30


def _dot(a, b):
    return jnp.dot(a, b, preferred_element_type=F32)


def _dot_nt(a, b):
    return lax.dot_general(a, b, (((1,), (1,)), ((), ())), preferred_element_type=F32)


def _dot_tn(a, b):
    return lax.dot_general(a, b, (((0,), (0,)), ((), ())), preferred_element_type=F32)


def _split2(x):
    hi = x.astype(BF16)
    lo = (x - hi.astype(F32)).astype(BF16)
    return hi, lo


def _head_sums(x, ones_block, split):
    width = ones_block.shape[0]
    out = []
    for lo_col in range(0, x.shape[1], width):
        xs = x[:, lo_col:lo_col + width]
        if split:
            hi, lo = _split2(xs)
            out.append(_dot(hi, ones_block) + _dot(lo, ones_block))
        else:
            out.append(_dot(xs.astype(BF16), ones_block))
    return jnp.concatenate(out, axis=1)


def _ones_matmul_left(m, x):
    hi, lo = _split2(x)
    return _dot(m, hi) + _dot(m, lo)


def _rms(x, g, eps=NORM_EPS):
    return x * lax.rsqrt(jnp.mean(x * x, axis=-1, keepdims=True) + eps) * g


def _sigmoid(x):
    return 1.0 / (1.0 + jnp.exp(-x))


def _row_tile(m):
    return ROW_TILE if m % ROW_TILE == 0 else m


def _seq_tile(s):
    return SEQ_TILE if s % SEQ_TILE == 0 else s


def _resident(block_shape, index_map):
    return pl.BlockSpec(block_shape, index_map, pipeline_mode=pl.Buffered(1))


def _ffn_kernel(*refs, g_pre, g_post, n_parts):
    x_ref, g_ref, wgu_ref, wd_ref = refs[:4]
    part_refs = refs[4:4 + n_parts]
    o_ref, act_ref = refs[-2:]
    sub = x_ref.shape[0] // FFN_SUBTILES
    row_sets = [slice(i * sub, (i + 1) * sub) for i in range(FFN_SUBTILES)]
    hs = []
    for rs in row_sets:
        x = x_ref[rs, :]
        if n_parts:
            wout_ref = refs[4 + n_parts]
            mix = None
            lo = 0
            for p in part_refs:
                width = p.shape[-1]
                term = _dot(p[rs, :], wout_ref[lo:lo + width, :])
                mix = term if mix is None else mix + term
                lo += width
            x = x + _rms(mix, g_ref[3:4, :])
        o_ref[rs, :] = x
        hs.append(_rms(x, g_ref[g_pre:g_pre + 1, :]).astype(BF16))
    for j in range(D_FF // FF_TILE):
        lo = j * FF_TILE
        for rs, h in zip(row_sets, hs):
            gate = _dot(h, wgu_ref[:, lo:lo + FF_TILE])
            up = _dot(h, wgu_ref[:, D_FF + lo:D_FF + lo + FF_TILE])
            act_ref[rs, lo:lo + FF_TILE] = (gate * _sigmoid(gate) * up).astype(BF16)
    ys = [_dot(act_ref[rs, :], wd_ref[...]) for rs in row_sets]
    for rs, y in zip(row_sets, ys):
        o_ref[rs, :] = o_ref[rs, :] + 0.5 * _rms(y, g_ref[g_post:g_post + 1, :])


def _ffn(x2, norms, wgu, wd, layer, which, parts=(), w_out=None, j=0):
    m, d = x2.shape
    tile = FFN_POST_ROW_TILE if parts else FFN_ROW_TILE
    tm = tile if m % tile == 0 else m
    kern = functools.partial(_ffn_kernel, g_pre=4 * which, g_post=4 * which + 1,
                             n_parts=len(parts))
    in_specs = [
        pl.BlockSpec((tm, d), lambda i: (i, 0)),
        _resident((None, 6, d), lambda i: (layer, 0, 0)),
        _resident((None, None, d, 2 * D_FF), lambda i: (layer, which, 0, 0)),
        _resident((None, None, D_FF, d), lambda i: (layer, which, 0, 0)),
    ]
    args = [x2, norms, wgu, wd]
    if parts:
        k_total = sum(p.shape[-1] for p in parts)
        in_specs += [pl.BlockSpec((tm, p.shape[-1]), lambda i: (i, 0)) for p in parts]
        in_specs.append(_resident((None, k_total, d), lambda i: (j, 0, 0)))
        args += list(parts) + [w_out]
    return pl.pallas_call(
        kern,
        out_shape=jax.ShapeDtypeStruct((m, d), F32),
        grid=(m // tm,),
        in_specs=in_specs,
        out_specs=pl.BlockSpec((tm, d), lambda i: (i, 0)),
        scratch_shapes=[pltpu.VMEM((tm, D_FF), BF16)],
        compiler_params=pltpu.CompilerParams(
            dimension_semantics=("parallel",), vmem_limit_bytes=VMEM_LIMIT),
        name="ffn_post" if parts else "ffn",
    )(*args)


def _inproj_even_kernel(x_ref, g_ref, w_ref, za_ref, zb_ref):
    h = _rms(x_ref[...], g_ref[2:3, :]).astype(BF16)
    za_ref[...] = _dot(h, w_ref[:, :A_COLS]).astype(BF16)
    zb_ref[...] = _dot(h, w_ref[:, A_COLS:])


def _inproj_even(x2, norms, w_in, layer, j):
    m, d = x2.shape
    tm = _row_tile(m)
    return pl.pallas_call(
        _inproj_even_kernel,
        out_shape=(jax.ShapeDtypeStruct((m, A_COLS), BF16),
                   jax.ShapeDtypeStruct((m, B_COLS), F32)),
        grid=(m // tm,),
        in_specs=[
            pl.BlockSpec((tm, d), lambda i: (i, 0)),
            _resident((None, 6, d), lambda i: (layer, 0, 0)),
            _resident((None, d, EVEN_IN), lambda i: (j, 0, 0)),
        ],
        out_specs=(pl.BlockSpec((tm, A_COLS), lambda i: (i, 0)),
                   pl.BlockSpec((tm, B_COLS), lambda i: (i, 0))),
        compiler_params=pltpu.CompilerParams(
            dimension_semantics=("parallel",), vmem_limit_bytes=VMEM_LIMIT),
        name="inproj_even",
    )(x2, norms, w_in)


def _inproj_odd_kernel(x_ref, g_ref, w_ref, qk_ref, v_ref, gate_ref):
    h = _rms(x_ref[...], g_ref[2:3, :]).astype(BF16)
    qk_ref[...] = _dot(h, w_ref[:, :2 * C_QK_WIDTH]).astype(BF16)
    v_ref[...] = _dot(h, w_ref[:, 2 * C_QK_WIDTH:2 * C_QK_WIDTH + C_V_WIDTH]).astype(BF16)
    gate_ref[...] = _dot(h, w_ref[:, 2 * C_QK_WIDTH + C_V_WIDTH:])


def _inproj_odd(x2, norms, w_in, layer, j):
    m, d = x2.shape
    tm = _row_tile(m)
    return pl.pallas_call(
        _inproj_odd_kernel,
        out_shape=(jax.ShapeDtypeStruct((m, 2 * C_QK_WIDTH), BF16),
                   jax.ShapeDtypeStruct((m, C_V_WIDTH), BF16),
                   jax.ShapeDtypeStruct((m, C_V_WIDTH), F32)),
        grid=(m // tm,),
        in_specs=[
            pl.BlockSpec((tm, d), lambda i: (i, 0)),
            _resident((None, 6, d), lambda i: (layer, 0, 0)),
            _resident((None, d, ODD_IN), lambda i: (j, 0, 0)),
        ],
        out_specs=(pl.BlockSpec((tm, 2 * C_QK_WIDTH), lambda i: (i, 0)),
                   pl.BlockSpec((tm, C_V_WIDTH), lambda i: (i, 0)),
                   pl.BlockSpec((tm, C_V_WIDTH), lambda i: (i, 0))),
        compiler_params=pltpu.CompilerParams(
            dimension_semantics=("parallel",), vmem_limit_bytes=VMEM_LIMIT),
        name="inproj_odd",
    )(x2, norms, w_in)


def _lane_tiles(x):
    return [x[:, i:i + LANES] for i in range(0, x.shape[1], LANES)]


def _attn_kernel(lam_ref, subln_ref, dbias_ref, q_ref, k_ref, v_ref, o_ref,
                 s_ref, m_ref, l_ref, acc_ref, *, tq, tk, lam_init):
    head = pl.program_id(1)
    qi = pl.program_id(2)
    n_diag = tq // tk
    first_diag = qi * n_diag
    log2e = 1.0 / math.log(2.0)
    slope2 = log2e * jnp.exp(
        (head + 1).astype(F32) * jnp.full((1, 1), -8.0 / A_HEADS * math.log(2.0), F32))

    lane = lax.broadcasted_iota(jnp.int32, (1, A_V_DIM), 1)
    q = q_ref[...] * jnp.asarray(A_QK_DIM ** -0.5, BF16)
    zero = jnp.zeros_like(q)
    q_maps = (jnp.where(lane < A_QK_DIM, q, zero), jnp.where(lane >= A_QK_DIM, q, zero))
    n_maps = len(q_maps)
    col_row = lax.broadcasted_iota(jnp.int32, (1, tk), 1)

    def lane_max(m, s):
        for t in _lane_tiles(s):
            m = t if m is None else jnp.maximum(m, t)
        return m

    def pass1(blocks, first):
        m_new = [None if first else m_ref[mi] for mi in range(n_maps)]
        for kb, diag_slot in blocks:
            k = k_ref[pl.ds(pl.multiple_of(kb * tk, tk), tk), :]
            if diag_slot is None:
                bias = slope2 * (col_row + (kb * tk - qi * tq)).astype(F32)
            else:
                bias = dbias_ref[:, diag_slot * tk:(diag_slot + 1) * tk]
            for mi, qm in enumerate(q_maps):
                s = _dot_nt(qm, k) * log2e + bias
                s_ref[mi, kb] = s
                m_new[mi] = lane_max(m_new[mi], s)
        for mi in range(n_maps):
            m_ref[mi] = m_new[mi]

    def pass2(blocks, first):
        for mi in range(n_maps):
            m_row = m_ref[mi]
            l_new = None if first else l_ref[mi]
            acc_new = None if first else acc_ref[mi]
            for kb in blocks:
                tiles = [jnp.exp2(t - m_row) for t in _lane_tiles(s_ref[mi, kb])]
                for t in tiles:
                    l_new = t if l_new is None else l_new + t
                pv = _dot(jnp.concatenate(tiles, axis=1).astype(BF16),
                          v_ref[pl.ds(pl.multiple_of(kb * tk, tk), tk), :])
                acc_new = pv if acc_new is None else acc_new + pv
            l_ref[mi] = l_new
            acc_ref[mi] = acc_new

    def visible_loops(fn):
        done = 0
        for size in (4, 2, 1):
            trips = (first_diag - done) // size
            base = done

            def body(it, carry, size=size, base=base):
                fn([base + it * size + g for g in range(size)])
                return carry

            lax.fori_loop(0, trips, body, 0)
            done = done + trips * size

    pass1([(first_diag + d, d) for d in range(n_diag)], True)
    visible_loops(lambda kbs: pass1([(kb, None) for kb in kbs], False))
    for mi in range(n_maps):
        m_ref[mi] = jnp.broadcast_to(jnp.max(m_ref[mi], axis=-1, keepdims=True), (tq, LANES))
    pass2([first_diag + d for d in range(n_diag)], True)
    visible_loops(lambda kbs: pass2(kbs, False))

    lv = lam_ref[...]
    lam = (jnp.exp(jnp.sum(lv[0:1] * lv[1:2], axis=-1, keepdims=True))
           - jnp.exp(jnp.sum(lv[2:3] * lv[3:4], axis=-1, keepdims=True)) + lam_init)
    outs = [acc_ref[mi] * (1.0 / jnp.sum(l_ref[mi], axis=-1, keepdims=True))
            for mi in range(n_maps)]
    o = outs[0] - lam * outs[1]
    o_ref[...] = (_rms(o, subln_ref[...]) * (1.0 - lam_init)).astype(BF16)


def _attn_diag_bias(tq):
    slopes = 2.0 ** (-8.0 / A_HEADS * jnp.arange(1, A_HEADS + 1, dtype=F32))
    i = jnp.arange(tq)
    rel = (i[:, None] - jnp.abs(i[:, None] - i[None, :])).astype(F32)
    visible = (i[None, :] // CHUNK) <= (i[:, None] // CHUNK)
    bias = slopes[:, None, None] * rel[None] / math.log(2.0)
    return jnp.where(visible[None], bias, NEG_BIG)


def _diff_attention(za, lam_vecs, subln, lam_init, j):
    b, s, _ = za.shape
    tq = ATTN_Q_TILE if s % ATTN_Q_TILE == 0 else s
    tk = ATTN_K_TILE if tq % ATTN_K_TILE == 0 else tq
    kern = functools.partial(_attn_kernel, tq=tq, tk=tk, lam_init=lam_init)
    n_maps = 2
    return pl.pallas_call(
        kern,
        out_shape=jax.ShapeDtypeStruct((b, s, A_WIDTH), BF16),
        grid=(b, A_HEADS, s // tq),
        in_specs=[
            _resident((None, 4, A_QK_DIM), lambda bi, h, qi: (j, 0, 0)),
            _resident((None, 1, A_V_DIM), lambda bi, h, qi: (j, 0, 0)),
            pl.BlockSpec((None, tq, tq), lambda bi, h, qi: (h, 0, 0)),
            pl.BlockSpec((None, tq, A_V_DIM), lambda bi, h, qi: (bi, qi, h)),
            pl.BlockSpec((None, s, A_V_DIM), lambda bi, h, qi: (bi, 0, A_HEADS + h)),
            pl.BlockSpec((None, s, A_V_DIM), lambda bi, h, qi: (bi, 0, 2 * A_HEADS + h)),
        ],
        out_specs=pl.BlockSpec((None, tq, A_V_DIM), lambda bi, h, qi: (bi, qi, h)),
        scratch_shapes=[pltpu.VMEM((n_maps, s // tk, tq, tk), F32),
                        pltpu.VMEM((n_maps, tq, LANES), F32),
                        pltpu.VMEM((n_maps, tq, LANES), F32),
                        pltpu.VMEM((n_maps, tq, A_V_DIM), F32)],
        compiler_params=pltpu.CompilerParams(
            dimension_semantics=("parallel", "parallel", "parallel"),
            vmem_limit_bytes=VMEM_LIMIT),
        name="diff_attention",
    )(lam_vecs, subln, _attn_diag_bias(tq), za, za, za)


def _rwkv_kernel(*refs, tile, has_vres):
    (zb_ref, mu_ref, vec_ref, wup_ref, aup_ref, gup_ref, hsum_ref, ltri_ref, lall_ref) = refs[:9]
    if has_vres:
        vfirst_ref, v0_ref, vdown_ref, vup_ref, ob_ref = refs[9:14]
        scratch = refs[14:]
    else:
        ob_ref, vb_ref = refs[9:11]
        scratch = refs[11:]
    (state_ref, prev_ref, at_s, rt_s, bt_s, kt_s, be_s, ke_s, v_s, gam_s, y_s) = scratch

    @pl.when(pl.program_id(1) == 0)
    def _():
        state_ref[...] = jnp.zeros_like(state_ref)
        prev_ref[...] = jnp.zeros_like(prev_ref)

    z = zb_ref[...]
    row = lax.broadcasted_iota(jnp.int32, z.shape, 0)
    z_prev = jnp.where(row == 0, prev_ref[...], pltpu.roll(z, 1, axis=0))
    prev_ref[...] = z[tile - 1:tile, :]
    zs = z + (z_prev - z) * mu_ref[...]

    w3 = B_WIDTH
    r = zs[:, 0:w3]
    k = zs[:, w3:2 * w3]
    v = zs[:, 2 * w3:3 * w3]
    x_wa = zs[:, 3 * w3:3 * w3 + B_DECAY_RANK + B_A_RANK]
    x_g = zs[:, 3 * w3 + B_DECAY_RANK + B_A_RANK:]
    w0, a0, k_k, k_a, r_k, ln_w, ln_b = (vec_ref[i:i + 1, :] for i in range(7))

    d_in = -(w0 + _dot(jnp.tanh(x_wa).astype(BF16), wup_ref[...]))
    softplus = jnp.maximum(d_in, 0.0) + jnp.log(1.0 + jnp.exp(-jnp.abs(d_in)))
    log_decay = -jnp.exp(-softplus - 0.5)

    if has_vres:
        low = _dot(v.astype(BF16), vdown_ref[...])
        mixed = _dot(low.astype(BF16), vup_ref[...])
        v = v + (vfirst_ref[...] - v) * _sigmoid(v0_ref[...] + mixed)
    else:
        vb_ref[...] = v
    gate_a = _sigmoid(a0 + _dot(x_wa.astype(BF16), aup_ref[...]))
    gate_out = _dot(_sigmoid(x_g).astype(BF16), gup_ref[...])

    hsum = hsum_ref[...]
    kk = k * k_k
    kk = kk / jnp.maximum(jnp.sqrt(_head_sums(kk * kk, hsum, split=True)), 1e-12)
    k = k * (1.0 + (gate_a - 1.0) * k_a)
    b_vec = kk * gate_a

    cum = _ones_matmul_left(ltri_ref[...], log_decay)
    total = _ones_matmul_left(lall_ref[...], log_decay)
    g_inc = jnp.exp(cum)
    g_inv = jnp.exp(-cum)
    g_tail = jnp.exp(total - cum)
    rt_s[...] = r * g_inc
    kt_s[...] = k * g_inv
    bt_s[...] = b_vec * g_inv
    at_s[...] = -kk * jnp.exp(cum - log_decay)
    ke_s[...] = k * g_tail
    be_s[...] = b_vec * g_tail
    gam_s[...] = jnp.exp(total)
    v_s[...] = v

    assert B_HEAD_DIM == CHUNK and 2 * B_HEAD_DIM == LANES
    pw = 2 * B_HEAD_DIM
    n_pairs = B_HEADS // 2
    lane_p = lax.broadcasted_iota(jnp.int32, (CHUNK, pw), 1)
    row_p = lax.broadcasted_iota(jnp.int32, (CHUNK, pw), 0)
    first_head = lane_p < B_HEAD_DIM
    eye = jnp.where(row_p == (lane_p & (B_HEAD_DIM - 1)), 1.0, 0.0)
    lane_2 = lax.broadcasted_iota(jnp.int32, (CHUNK, 2 * pw), 1)
    row_2 = lax.broadcasted_iota(jnp.int32, (CHUNK, 2 * pw), 0)
    strict = (lane_2 & (B_HEAD_DIM - 1)) < row_2
    incl = (lane_2 & (B_HEAD_DIM - 1)) <= row_2
    same_head = ((lax.broadcasted_iota(jnp.int32, (pw, pw), 0) < B_HEAD_DIM)
                 == (lax.broadcasted_iota(jnp.int32, (pw, pw), 1) < B_HEAD_DIM))
    zeros_pair = jnp.zeros((CHUNK, pw), F32)
    zeros_bd = jnp.zeros((pw, pw), BF16)

    def block_diag(x):
        return jnp.concatenate(
            [jnp.where(first_head, x, 0.0), jnp.where(first_head, 0.0, x)], axis=0).astype(BF16)

    def group_body(g, carry):
        pairs = [(ci, pr) for ci in range(RWKV_GROUP) for pr in range(n_pairs)]
        rows = [pl.ds(pl.multiple_of((g * RWKV_GROUP + ci) * CHUNK, CHUNK), CHUNK)
                for ci in range(RWKV_GROUP)]
        tile_of = lambda ref, p: ref[rows[p[0]], p[1] * pw:(p[1] + 1) * pw]

        a_t, r_t, v_p, a_sc, r_sc = {}, {}, {}, {}, {}
        for p in pairs:
            a_t[p] = tile_of(at_s, p)
            r_t[p] = tile_of(rt_s, p)
            v_p[p] = tile_of(v_s, p)
            lhs = jnp.concatenate([a_t[p], r_t[p]], axis=0).astype(BF16)
            rhs = jnp.concatenate([block_diag(tile_of(bt_s, p)), block_diag(tile_of(kt_s, p))],
                                  axis=0)
            scores = _dot_nt(lhs, rhs)
            a_sc[p] = jnp.where(strict, scores[:CHUNK], 0.0)
            r_sc[p] = jnp.where(incl, scores[CHUNK:], 0.0).astype(BF16)

        power, inv, akv, v_bd = {}, {}, {}, {}
        for p in pairs:
            n = a_sc[p][:, :pw]
            inv[p] = eye + n
            power[p] = _dot(n.astype(BF16), block_diag(n))
            v_bd[p] = block_diag(v_p[p])
            akv[p] = _dot(a_sc[p][:, pw:].astype(BF16), v_bd[p])
        for _ in range(int(math.log2(CHUNK)) - 2):
            for p in pairs:
                both = _dot(jnp.concatenate([inv[p], power[p]], axis=0).astype(BF16),
                            block_diag(power[p]))
                inv[p] = inv[p] + both[:CHUNK]
                power[p] = both[CHUNK:]
        for p in pairs:
            inv[p] = inv[p] + _dot(inv[p].astype(BF16), block_diag(power[p]))

        wu, g_mat, h_mat = {}, {}, {}
        for p in pairs:
            wu[p] = _dot(inv[p].astype(BF16),
                         jnp.concatenate([block_diag(a_t[p]), block_diag(akv[p])], axis=1))
        for p in pairs:
            z_bd = jnp.concatenate(
                [jnp.concatenate([block_diag(wu[p][:, :pw]), block_diag(wu[p][:, pw:])], axis=1),
                 jnp.concatenate([zeros_bd, v_bd[p]], axis=1)], axis=0)
            g_mat[p] = _dot(r_sc[p], z_bd)
        for p in pairs:
            z_rows = jnp.concatenate(
                [wu[p], jnp.concatenate([zeros_pair, v_p[p]], axis=1)], axis=0).astype(BF16)
            bk_e = jnp.concatenate([tile_of(be_s, p), tile_of(ke_s, p)], axis=0).astype(BF16)
            h_mat[p] = _dot_tn(z_rows, bk_e)

        state = [state_ref[pr] for pr in range(n_pairs)]
        y_rows = []
        for ci in range(RWKV_GROUP):
            y_pairs = []
            for pr in range(n_pairs):
                p = (ci, pr)
                r_eff = (r_t[p] + g_mat[p][:, :pw]).astype(BF16)
                y_pairs.append(_dot_nt(r_eff, block_diag(state[pr])) + g_mat[p][:, pw:])
                trans = jnp.where(same_head, h_mat[p][:pw], 0.0).astype(BF16)
                q_full = h_mat[p][pw:]
                gam = gam_s[pl.ds(pl.multiple_of((g * RWKV_GROUP + ci) * CHUNK, CHUNK), 1),
                            pr * pw:(pr + 1) * pw]
                state[pr] = (state[pr] * gam + _dot(state[pr].astype(BF16), trans)
                             + jnp.where(first_head, q_full[:B_HEAD_DIM], q_full[B_HEAD_DIM:]))
            y_rows.append(jnp.concatenate(y_pairs, axis=1))
        for ci in range(RWKV_GROUP):
            y_s[rows[ci], :] = y_rows[ci]
        for pr in range(n_pairs):
            state_ref[pr] = state[pr]
        return carry

    lax.fori_loop(0, tile // (CHUNK * RWKV_GROUP), group_body, 0)

    y = y_s[...]
    inv_n = 1.0 / B_HEAD_DIM
    mean = _head_sums(y, hsum, split=True) * inv_n
    yc = y - mean
    var = _head_sums(yc * yc, hsum, split=False) * inv_n
    yn = yc * lax.rsqrt(var + B_LN_EPS) * ln_w + ln_b
    bonus = _head_sums(r * k * r_k, hsum, split=False) * v
    ob_ref[...] = ((yn + bonus) * gate_out).astype(BF16)


def _block_ones(n, block, lower):
    i = jnp.arange(n)
    same = (i[:, None] // block) == (i[None, :] // block)
    if lower:
        same = same & (i[None, :] <= i[:, None])
    return same.astype(BF16)


def _pad_rows(w, lo, total):
    return jnp.zeros((total, w.shape[1]), w.dtype).at[lo:lo + w.shape[0]].set(w)


def _rwkv(zb, mu, vec, w_up, a_up, g_up, v_first, vres):
    b, s, _ = zb.shape
    tile = _seq_tile(s)
    lora_in = B_DECAY_RANK + B_A_RANK
    wup_p = _pad_rows(w_up, 0, lora_in).astype(BF16)
    aup_p = _pad_rows(a_up, B_DECAY_RANK, lora_in).astype(BF16)
    consts = [mu.reshape(1, B_COLS), vec, wup_p, aup_p, g_up.astype(BF16),
              _block_ones(MXU_TILE, B_HEAD_DIM, False), _block_ones(tile, CHUNK, True),
              _block_ones(tile, CHUNK, False)]
    full = lambda a: _resident(a.shape, lambda bi, t: (0,) * a.ndim)
    seq = lambda width: pl.BlockSpec((None, tile, width), lambda bi, t: (bi, t, 0))
    in_specs = [seq(B_COLS)] + [full(a) for a in consts]
    args = [zb] + consts
    out_shape = [jax.ShapeDtypeStruct((b, s, B_WIDTH), BF16)]
    out_specs = [seq(B_WIDTH)]
    if vres is not None:
        v0, v_down, v_up = vres
        vdown_p = jnp.zeros((B_WIDTH, LANES), F32).at[:, :B_VRES_RANK].set(v_down).astype(BF16)
        vup_p = _pad_rows(v_up, 0, LANES).astype(BF16)
        extra = [v0.reshape(1, B_WIDTH), vdown_p, vup_p]
        in_specs += [seq(B_WIDTH)] + [full(a) for a in extra]
        args += [v_first] + extra
    else:
        out_shape.append(jax.ShapeDtypeStruct((b, s, B_WIDTH), F32))
        out_specs.append(seq(B_WIDTH))
    wide = lambda: pltpu.VMEM((tile, B_WIDTH), F32)
    kern = functools.partial(_rwkv_kernel, tile=tile, has_vres=vres is not None)
    outs = pl.pallas_call(
        kern,
        out_shape=tuple(out_shape),
        grid=(b, s // tile),
        in_specs=in_specs,
        out_specs=tuple(out_specs),
        scratch_shapes=[pltpu.VMEM((B_HEADS // 2, B_HEAD_DIM, 2 * B_HEAD_DIM), F32),
                        pltpu.VMEM((1, B_COLS), F32)] + [wide() for _ in range(9)],
        compiler_params=pltpu.CompilerParams(
            dimension_semantics=("parallel", "arbitrary"), vmem_limit_bytes=VMEM_LIMIT),
        name="rwkv7",
    )(*args)
    return outs if vres is None else (outs[0], None)


def _retention_kernel(mask_ref, qdec_ref, kdec_ref, cdec_ref, q_ref, k_ref, v_ref, g_ref,
                      o_ref, state_ref):
    @pl.when(pl.program_id(1) == 0)
    def _():
        state_ref[...] = jnp.zeros_like(state_ref)

    heads = range(C_HEADS)
    qk_cols = lambda h: slice(h * C_QK_DIM, (h + 1) * C_QK_DIM)
    v_cols = lambda h: slice(h * C_V_DIM, (h + 1) * C_V_DIM)
    q = [q_ref[:, qk_cols(h)] for h in heads]
    k = [k_ref[:, qk_cols(h)] for h in heads]
    v = [v_ref[:, v_cols(h)] for h in heads]
    state = [state_ref[h] for h in heads]
    scores = [(_dot_nt(q[h], k[h]) * mask_ref[h]).astype(BF16) for h in heads]
    q_dec = [(q[h].astype(F32) * qdec_ref[h]).astype(BF16) for h in heads]
    y = [_dot(scores[h], v[h]) + _dot(q_dec[h], state[h].astype(BF16)) for h in heads]
    k_dec = [(k[h].astype(F32) * kdec_ref[h]).astype(BF16) for h in heads]
    for h in heads:
        state_ref[h] = state[h] * cdec_ref[h] + _dot_tn(k_dec[h], v[h])
    for h in heads:
        yn = y[h] * lax.rsqrt(jnp.mean(y[h] * y[h], axis=-1, keepdims=True) + NORM_EPS)
        g = g_ref[:, v_cols(h)]
        o_ref[:, v_cols(h)] = (g * _sigmoid(g) * yn).astype(BF16)


def _retention_tables(tile):
    log_g = jnp.log(1.0 - 2.0 ** (-5.0 - jnp.arange(C_HEADS, dtype=F32)))[:, None, None]
    i = jnp.arange(tile)
    dist = jnp.abs(i[:, None] - i[None, :]).astype(F32)
    visible = (i[None, :] // CHUNK) <= (i[:, None] // CHUNK)
    scale = C_QK_DIM ** -0.5
    mask = jnp.where(visible[None], jnp.exp(log_g * dist[None]), 0.0) * scale
    idx = i.astype(F32)[None, :, None]
    qdec = jnp.broadcast_to(jnp.exp(log_g * (idx + 1.0)), (C_HEADS, tile, C_QK_DIM))
    kdec = jnp.broadcast_to(jnp.exp(log_g * (tile - 1.0 - idx)) * scale, (C_HEADS, tile, C_QK_DIM))
    cdec = jnp.broadcast_to(jnp.exp(log_g * tile), (C_HEADS, 1, C_V_DIM))
    return mask, qdec, kdec, cdec


def _retention(qk, v, gate):
    b, s, _ = v.shape
    tile = _seq_tile(s)
    mask, qdec, kdec, cdec = _retention_tables(tile)
    return pl.pallas_call(
        _retention_kernel,
        out_shape=jax.ShapeDtypeStruct((b, s, C_V_WIDTH), BF16),
        grid=(b, s // tile),
        in_specs=[
            _resident(mask.shape, lambda bi, t: (0, 0, 0)),
            _resident(qdec.shape, lambda bi, t: (0, 0, 0)),
            _resident(kdec.shape, lambda bi, t: (0, 0, 0)),
            _resident(cdec.shape, lambda bi, t: (0, 0, 0)),
            pl.BlockSpec((None, tile, C_QK_WIDTH), lambda bi, t: (bi, t, 0)),
            pl.BlockSpec((None, tile, C_QK_WIDTH), lambda bi, t: (bi, t, 1)),
            pl.BlockSpec((None, tile, C_V_WIDTH), lambda bi, t: (bi, t, 0)),
            pl.BlockSpec((None, tile, C_V_WIDTH), lambda bi, t: (bi, t, 0)),
        ],
        out_specs=pl.BlockSpec((None, tile, C_V_WIDTH), lambda bi, t: (bi, t, 0)),
        scratch_shapes=[pltpu.VMEM((C_HEADS, C_QK_DIM, C_V_DIM), F32)],
        compiler_params=pltpu.CompilerParams(
            dimension_semantics=("parallel", "arbitrary"),
            vmem_limit_bytes=VMEM_LIMIT),
        name="retention",
    )(mask, qdec, kdec, cdec, qk, qk, v, gate)


def kernel(x, norms, ffn_wgu, ffn_wd, even_w_in, even_w_out, diff_lam, diff_subln, rwkv_mu, rwkv_vec, rwkv_w_up, rwkv_a_up, rwkv_g_up, rwkv_v0, rwkv_v_down, rwkv_v_up, odd_w_in, odd_w_out):
    b, s, d = x.shape
    m = b * s
    wgu = ffn_wgu.astype(BF16)
    wd = ffn_wd.astype(BF16)
    even_in = even_w_in.astype(BF16)
    even_out = even_w_out.astype(BF16)
    odd_in = odd_w_in.astype(BF16)
    odd_out = odd_w_out.astype(BF16)
    subln = diff_subln.reshape(-1, 1, A_V_DIM)

    x2 = x.reshape(m, d)
    v_first = None
    for i in range(DEPTH):
        j = i // 2
        x2 = _ffn(x2, norms, wgu, wd, i, 0)
        if i % 2 == 0:
            lam_init = 0.8 - 0.6 * math.exp(-0.3 * i)
            za, zb = _inproj_even(x2, norms, even_in, i, j)
            oa = _diff_attention(za.reshape(b, s, A_COLS), diff_lam, subln, lam_init, j)
            vres = None if j == 0 else (rwkv_v0[j - 1], rwkv_v_down[j - 1], rwkv_v_up[j - 1])
            ob, v_b = _rwkv(zb.reshape(b, s, B_COLS), rwkv_mu[j], rwkv_vec[j], rwkv_w_up[j],
                            rwkv_a_up[j], rwkv_g_up[j], v_first, vres)
            if j == 0:
                v_first = v_b
            parts, w_out = [oa.reshape(m, A_WIDTH), ob.reshape(m, B_WIDTH)], even_out
        else:
            qk, v, gate = _inproj_odd(x2, norms, odd_in, i, j)
            y = _retention(qk.reshape(b, s, -1), v.reshape(b, s, -1), gate.reshape(b, s, -1))
            parts, w_out = [y.reshape(m, C_V_WIDTH)], odd_out
        x2 = _ffn(x2, norms, wgu, wd, i, 1, parts, w_out, j)
    return x2.reshape(b, s, d)
```

```python
import functools
import math

import jax
import jax.numpy as jnp
from jax import lax
from jax.experimental import pallas as pl
from jax.experimental.pallas import tpu as pltpu

F32 = jnp.float32
BF16 = jnp.bfloat16

D_MODEL = 1024
DEPTH = 4
CHUNK = 64
D_FF = 2816
NORM_EPS = 1e-6

A_HEADS = 4
A_QK_DIM = 64
A_V_DIM = 2 * A_QK_DIM
A_WIDTH = A_HEADS * A_V_DIM
A_COLS = 3 * A_WIDTH

B_HEAD_DIM = 64
B_WIDTH = D_MODEL - A_WIDTH
B_HEADS = B_WIDTH // B_HEAD_DIM
B_DECAY_RANK = 64
B_A_RANK = 64
B_GATE_RANK = 128
B_VRES_RANK = 32
B_LN_EPS = 64e-5
B_COLS = 3 * B_WIDTH + B_DECAY_RANK + B_A_RANK + B_GATE_RANK
EVEN_IN = A_COLS + B_COLS

C_HEADS = 4
C_QK_DIM = D_MODEL // C_HEADS
C_V_DIM = 2 * C_QK_DIM
C_QK_WIDTH = C_HEADS * C_QK_DIM
C_V_WIDTH = C_HEADS * C_V_DIM
ODD_IN = 2 * C_QK_WIDTH + 2 * C_V_WIDTH

LANES = 128
MXU_TILE = 256
ROW_TILE = 512
FFN_ROW_TILE = 1024
FFN_POST_ROW_TILE = 512
FFN_SUBTILES = 2
FF_TILE = 256
SEQ_TILE = 256
ATTN_Q_TILE = 512
ATTN_K_TILE = 256
RWKV_GROUP = 4
VMEM_LIMIT = 52 * 1024 * 1024
NEG_BIG = -1e30


def _dot(a, b):
    return jnp.dot(a, b, preferred_element_type=F32)


def _dot_nt(a, b):
    return lax.dot_general(a, b, (((1,), (1,)), ((), ())), preferred_element_type=F32)


def _dot_tn(a, b):
    return lax.dot_general(a, b, (((0,), (0,)), ((), ())), preferred_element_type=F32)


def _split2(x):
    hi = x.astype(BF16)
    lo = (x - hi.astype(F32)).astype(BF16)
    return hi, lo


def _head_sums(x, ones_block, split):
    width = ones_block.shape[0]
    out = []
    for lo_col in range(0, x.shape[1], width):
        xs = x[:, lo_col:lo_col + width]
        if split:
            hi, lo = _split2(xs)
            out.append(_dot(hi, ones_block) + _dot(lo, ones_block))
        else:
            out.append(_dot(xs.astype(BF16), ones_block))
    return jnp.concatenate(out, axis=1)


def _ones_matmul_left(m, x):
    hi, lo = _split2(x)
    return _dot(m, hi) + _dot(m, lo)


def _rms(x, g, eps=NORM_EPS):
    return x * lax.rsqrt(jnp.mean(x * x, axis=-1, keepdims=True) + eps) * g


def _sigmoid(x):
    return 1.0 / (1.0 + jnp.exp(-x))


def _row_tile(m):
    return ROW_TILE if m % ROW_TILE == 0 else m


def _seq_tile(s):
    return SEQ_TILE if s % SEQ_TILE == 0 else s


def _resident(block_shape, index_map):
    return pl.BlockSpec(block_shape, index_map, pipeline_mode=pl.Buffered(1))


def _ffn_kernel(*refs, g_pre, g_post, n_parts):
    x_ref, g_ref, wgu_ref, wd_ref = refs[:4]
    part_refs = refs[4:4 + n_parts]
    o_ref, act_ref = refs[-2:]
    sub = x_ref.shape[0] // FFN_SUBTILES
    row_sets = [slice(i * sub, (i + 1) * sub) for i in range(FFN_SUBTILES)]
    hs = []
    for rs in row_sets:
        x = x_ref[rs, :]
        if n_parts:
            wout_ref = refs[4 + n_parts]
            mix = None
            lo = 0
            for p in part_refs:
                width = p.shape[-1]
                term = _dot(p[rs, :], wout_ref[lo:lo + width, :])
                mix = term if mix is None else mix + term
                lo += width
            x = x + _rms(mix, g_ref[3:4, :])
        o_ref[rs, :] = x
        hs.append(_rms(x, g_ref[g_pre:g_pre + 1, :]).astype(BF16))
    for j in range(D_FF // FF_TILE):
        lo = j * FF_TILE
        for rs, h in zip(row_sets, hs):
            gate = _dot(h, wgu_ref[:, lo:lo + FF_TILE])
            up = _dot(h, wgu_ref[:, D_FF + lo:D_FF + lo + FF_TILE])
            act_ref[rs, lo:lo + FF_TILE] = (gate * _sigmoid(gate) * up).astype(BF16)
    ys = [_dot(act_ref[rs, :], wd_ref[...]) for rs in row_sets]
    for rs, y in zip(row_sets, ys):
        o_ref[rs, :] = o_ref[rs, :] + 0.5 * _rms(y, g_ref[g_post:g_post + 1, :])


def _ffn(x2, norms, wgu, wd, layer, which, parts=(), w_out=None, j=0):
    m, d = x2.shape
    tile = FFN_POST_ROW_TILE if parts else FFN_ROW_TILE
    tm = tile if m % tile == 0 else m
    kern = functools.partial(_ffn_kernel, g_pre=4 * which, g_post=4 * which + 1,
                             n_parts=len(parts))
    in_specs = [
        pl.BlockSpec((tm, d), lambda i: (i, 0)),
        _resident((None, 6, d), lambda i: (layer, 0, 0)),
        _resident((None, None, d, 2 * D_FF), lambda i: (layer, which, 0, 0)),
        _resident((None, None, D_FF, d), lambda i: (layer, which, 0, 0)),
    ]
    args = [x2, norms, wgu, wd]
    if parts:
        k_total = sum(p.shape[-1] for p in parts)
        in_specs += [pl.BlockSpec((tm, p.shape[-1]), lambda i: (i, 0)) for p in parts]
        in_specs.append(_resident((None, k_total, d), lambda i: (j, 0, 0)))
        args += list(parts) + [w_out]
    return pl.pallas_call(
        kern,
        out_shape=jax.ShapeDtypeStruct((m, d), F32),
        grid=(m // tm,),
        in_specs=in_specs,
        out_specs=pl.BlockSpec((tm, d), lambda i: (i, 0)),
        scratch_shapes=[pltpu.VMEM((tm, D_FF), BF16)],
        compiler_params=pltpu.CompilerParams(
            dimension_semantics=("parallel",), vmem_limit_bytes=VMEM_LIMIT),
        name="ffn_post" if parts else "ffn",
    )(*args)


def _inproj_even_kernel(x_ref, g_ref, w_ref, za_ref, zb_ref):
    h = _rms(x_ref[...], g_ref[2:3, :]).astype(BF16)
    za_ref[...] = _dot(h, w_ref[:, :A_COLS]).astype(BF16)
    zb_ref[...] = _dot(h, w_ref[:, A_COLS:])


def _inproj_even(x2, norms, w_in, layer, j):
    m, d = x2.shape
    tm = _row_tile(m)
    return pl.pallas_call(
        _inproj_even_kernel,
        out_shape=(jax.ShapeDtypeStruct((m, A_COLS), BF16),
                   jax.ShapeDtypeStruct((m, B_COLS), F32)),
        grid=(m // tm,),
        in_specs=[
            pl.BlockSpec((tm, d), lambda i: (i, 0)),
            _resident((None, 6, d), lambda i: (layer, 0, 0)),
            _resident((None, d, EVEN_IN), lambda i: (j, 0, 0)),
        ],
        out_specs=(pl.BlockSpec((tm, A_COLS), lambda i: (i, 0)),
                   pl.BlockSpec((tm, B_COLS), lambda i: (i, 0))),
        compiler_params=pltpu.CompilerParams(
            dimension_semantics=("parallel",), vmem_limit_bytes=VMEM_LIMIT),
        name="inproj_even",
    )(x2, norms, w_in)


def _inproj_odd_kernel(x_ref, g_ref, w_ref, qk_ref, v_ref, gate_ref):
    h = _rms(x_ref[...], g_ref[2:3, :]).astype(BF16)
    qk_ref[...] = _dot(h, w_ref[:, :2 * C_QK_WIDTH]).astype(BF16)
    v_ref[...] = _dot(h, w_ref[:, 2 * C_QK_WIDTH:2 * C_QK_WIDTH + C_V_WIDTH]).astype(BF16)
    gate_ref[...] = _dot(h, w_ref[:, 2 * C_QK_WIDTH + C_V_WIDTH:])


def _inproj_odd(x2, norms, w_in, layer, j):
    m, d = x2.shape
    tm = _row_tile(m)
    return pl.pallas_call(
        _inproj_odd_kernel,
        out_shape=(jax.ShapeDtypeStruct((m, 2 * C_QK_WIDTH), BF16),
                   jax.ShapeDtypeStruct((m, C_V_WIDTH), BF16),
                   jax.ShapeDtypeStruct((m, C_V_WIDTH), F32)),
        grid=(m // tm,),
        in_specs=[
            pl.BlockSpec((tm, d), lambda i: (i, 0)),
            _resident((None, 6, d), lambda i: (layer, 0, 0)),
            _resident((None, d, ODD_IN), lambda i: (j, 0, 0)),
        ],
        out_specs=(pl.BlockSpec((tm, 2 * C_QK_WIDTH), lambda i: (i, 0)),
                   pl.BlockSpec((tm, C_V_WIDTH), lambda i: (i, 0)),
                   pl.BlockSpec((tm, C_V_WIDTH), lambda i: (i, 0))),
        compiler_params=pltpu.CompilerParams(
            dimension_semantics=("parallel",), vmem_limit_bytes=VMEM_LIMIT),
        name="inproj_odd",
    )(x2, norms, w_in)


def _lane_tiles(x):
    return [x[:, i:i + LANES] for i in range(0, x.shape[1], LANES)]


def _attn_kernel(lam_ref, subln_ref, dbias_ref, q_ref, k_ref, v_ref, o_ref, *scratch,
                 tq, tk, lam_init):
    def tile_body(qi, carry):
        rows = pl.ds(pl.multiple_of(qi * tq, tq), tq)
        _attn_tile(qi, lam_ref, subln_ref, dbias_ref, q_ref.at[rows], k_ref, v_ref, o_ref.at[rows],
                   *scratch, tq=tq, tk=tk, lam_init=lam_init)
        return carry

    lax.fori_loop(0, q_ref.shape[0] // tq, tile_body, 0)


def _attn_tile(qi, lam_ref, subln_ref, dbias_ref, q_ref, k_ref, v_ref, o_ref,
               s_ref, m_ref, l_ref, acc_ref, *, tq, tk, lam_init):
    head = pl.program_id(1)
    n_diag = tq // tk
    first_diag = qi * n_diag
    log2e = 1.0 / math.log(2.0)
    slope2 = log2e * jnp.exp(
        (head + 1).astype(F32) * jnp.full((1, 1), -8.0 / A_HEADS * math.log(2.0), F32))

    lane = lax.broadcasted_iota(jnp.int32, (1, A_V_DIM), 1)
    q = q_ref[...] * jnp.asarray(A_QK_DIM ** -0.5, BF16)
    zero = jnp.zeros_like(q)
    q_maps = (jnp.where(lane < A_QK_DIM, q, zero), jnp.where(lane >= A_QK_DIM, q, zero))
    n_maps = len(q_maps)
    col_row = lax.broadcasted_iota(jnp.int32, (1, tk), 1)

    def lane_max(m, s):
        for t in _lane_tiles(s):
            m = t if m is None else jnp.maximum(m, t)
        return m

    def pass1(blocks, first):
        m_new = [None if first else m_ref[mi] for mi in range(n_maps)]
        for kb, diag_slot in blocks:
            k = k_ref[pl.ds(pl.multiple_of(kb * tk, tk), tk), :]
            if diag_slot is None:
                bias = slope2 * (col_row + (kb * tk - qi * tq)).astype(F32)
            else:
                bias = dbias_ref[:, diag_slot * tk:(diag_slot + 1) * tk]
            for mi, qm in enumerate(q_maps):
                s = _dot_nt(qm, k) * log2e + bias
                s_ref[mi, kb] = s
                m_new[mi] = lane_max(m_new[mi], s)
        for mi in range(n_maps):
            m_ref[mi] = m_new[mi]

    def pass2(blocks, first):
        for mi in range(n_maps):
            m_row = m_ref[mi]
            l_new = None if first else l_ref[mi]
            acc_new = None if first else acc_ref[mi]
            for kb in blocks:
                tiles = [jnp.exp2(t - m_row) for t in _lane_tiles(s_ref[mi, kb])]
                for t in tiles:
                    l_new = t if l_new is None else l_new + t
                pv = _dot(jnp.concatenate(tiles, axis=1).astype(BF16),
                          v_ref[pl.ds(pl.multiple_of(kb * tk, tk), tk), :])
                acc_new = pv if acc_new is None else acc_new + pv
            l_ref[mi] = l_new
            acc_ref[mi] = acc_new

    def visible_loops(fn):
        done = 0
        for size in tuple(sz for sz in (8, 4, 2, 1) if sz >= n_diag):
            trips = (first_diag - done) // size
            base = done

            def body(it, carry, size=size, base=base):
                fn([base + it * size + g for g in range(size)])
                return carry

            lax.fori_loop(0, trips, body, 0)
            done = done + trips * size

    pass1([(first_diag + d, d) for d in range(n_diag)], True)
    visible_loops(lambda kbs: pass1([(kb, None) for kb in kbs], False))
    for mi in range(n_maps):
        m_ref[mi] = jnp.broadcast_to(jnp.max(m_ref[mi], axis=-1, keepdims=True), (tq, LANES))
    pass2([first_diag + d for d in range(n_diag)], True)
    visible_loops(lambda kbs: pass2(kbs, False))

    lv = lam_ref[...]
    lam = (jnp.exp(jnp.sum(lv[0:1] * lv[1:2], axis=-1, keepdims=True))
           - jnp.exp(jnp.sum(lv[2:3] * lv[3:4], axis=-1, keepdims=True)) + lam_init)
    outs = [acc_ref[mi] * (1.0 / jnp.sum(l_ref[mi], axis=-1, keepdims=True))
            for mi in range(n_maps)]
    o = outs[0] - lam * outs[1]
    o_ref[...] = (_rms(o, subln_ref[...]) * (1.0 - lam_init)).astype(BF16)


def _attn_diag_bias(tq):
    slopes = 2.0 ** (-8.0 / A_HEADS * jnp.arange(1, A_HEADS + 1, dtype=F32))
    i = jnp.arange(tq)
    rel = (i[:, None] - jnp.abs(i[:, None] - i[None, :])).astype(F32)
    visible = (i[None, :] // CHUNK) <= (i[:, None] // CHUNK)
    bias = slopes[:, None, None] * rel[None] / math.log(2.0)
    return jnp.where(visible[None], bias, NEG_BIG)


def _diff_attention(za, lam_vecs, subln, lam_init, j):
    b, s, _ = za.shape
    tq = ATTN_Q_TILE if s % ATTN_Q_TILE == 0 else s
    tk = ATTN_K_TILE if tq % ATTN_K_TILE == 0 else tq
    kern = functools.partial(_attn_kernel, tq=tq, tk=tk, lam_init=lam_init)
    n_maps = 2
    return pl.pallas_call(
        kern,
        out_shape=jax.ShapeDtypeStruct((b, s, A_WIDTH), BF16),
        grid=(b, A_HEADS),
        in_specs=[
            _resident((None, 4, A_QK_DIM), lambda bi, h: (j, 0, 0)),
            _resident((None, 1, A_V_DIM), lambda bi, h: (j, 0, 0)),
            pl.BlockSpec((None, tq, tq), lambda bi, h: (h, 0, 0)),
            pl.BlockSpec((None, s, A_V_DIM), lambda bi, h: (bi, 0, h)),
            pl.BlockSpec((None, s, A_V_DIM), lambda bi, h: (bi, 0, A_HEADS + h)),
            pl.BlockSpec((None, s, A_V_DIM), lambda bi, h: (bi, 0, 2 * A_HEADS + h)),
        ],
        out_specs=pl.BlockSpec((None, s, A_V_DIM), lambda bi, h: (bi, 0, h)),
        scratch_shapes=[pltpu.VMEM((n_maps, s // tk, tq, tk), F32),
                        pltpu.VMEM((n_maps, tq, LANES), F32),
                        pltpu.VMEM((n_maps, tq, LANES), F32),
                        pltpu.VMEM((n_maps, tq, A_V_DIM), F32)],
        compiler_params=pltpu.CompilerParams(
            dimension_semantics=("parallel", "parallel"),
            vmem_limit_bytes=VMEM_LIMIT),
        name="diff_attention",
    )(lam_vecs, subln, _attn_diag_bias(tq), za, za, za)


def _rwkv_kernel(*refs, tile, has_vres):
    (zb_ref, mu_ref, vec_ref, wup_ref, aup_ref, gup_ref, hsum_ref, ltri_ref, lall_ref) = refs[:9]
    if has_vres:
        vfirst_ref, v0_ref, vdown_ref, vup_ref, ob_ref = refs[9:14]
        scratch = refs[14:]
    else:
        ob_ref, vb_ref = refs[9:11]
        scratch = refs[11:]
    (state_ref, prev_ref, at_s, rt_s, bt_s, kt_s, be_s, ke_s, v_s, gam_s, y_s) = scratch

    @pl.when(pl.program_id(1) == 0)
    def _():
        state_ref[...] = jnp.zeros_like(state_ref)
        prev_ref[...] = jnp.zeros_like(prev_ref)

    z = zb_ref[...]
    row = lax.broadcasted_iota(jnp.int32, z.shape, 0)
    z_prev = jnp.where(row == 0, prev_ref[...], pltpu.roll(z, 1, axis=0))
    prev_ref[...] = z[tile - 1:tile, :]
    zs = z + (z_prev - z) * mu_ref[...]

    w3 = B_WIDTH
    r = zs[:, 0:w3]
    k = zs[:, w3:2 * w3]
    v = zs[:, 2 * w3:3 * w3]
    x_wa = zs[:, 3 * w3:3 * w3 + B_DECAY_RANK + B_A_RANK]
    x_g = zs[:, 3 * w3 + B_DECAY_RANK + B_A_RANK:]
    w0, a0, k_k, k_a, r_k, ln_w, ln_b = (vec_ref[i:i + 1, :] for i in range(7))

    d_in = -(w0 + _dot(jnp.tanh(x_wa).astype(BF16), wup_ref[...]))
    softplus = jnp.maximum(d_in, 0.0) + jnp.log(1.0 + jnp.exp(-jnp.abs(d_in)))
    log_decay = -jnp.exp(-softplus - 0.5)

    if has_vres:
        low = _dot(v.astype(BF16), vdown_ref[...])
        mixed = _dot(low.astype(BF16), vup_ref[...])
        v = v + (vfirst_ref[...] - v) * _sigmoid(v0_ref[...] + mixed)
    else:
        vb_ref[...] = v
    gate_a = _sigmoid(a0 + _dot(x_wa.astype(BF16), aup_ref[...]))
    gate_out = _dot(_sigmoid(x_g).astype(BF16), gup_ref[...])

    hsum = hsum_ref[...]
    kk = k * k_k
    kk = kk / jnp.maximum(jnp.sqrt(_head_sums(kk * kk, hsum, split=True)), 1e-12)
    k = k * (1.0 + (gate_a - 1.0) * k_a)
    b_vec = kk * gate_a

    cum = _ones_matmul_left(ltri_ref[...], log_decay)
    total = _ones_matmul_left(lall_ref[...], log_decay)
    g_inc = jnp.exp(cum)
    g_inv = jnp.exp(-cum)
    g_tail = jnp.exp(total - cum)
    rt_s[...] = r * g_inc
    kt_s[...] = k * g_inv
    bt_s[...] = b_vec * g_inv
    at_s[...] = -kk * jnp.exp(cum - log_decay)
    ke_s[...] = k * g_tail
    be_s[...] = b_vec * g_tail
    gam_s[...] = jnp.exp(total)
    v_s[...] = v

    assert B_HEAD_DIM == CHUNK and 2 * B_HEAD_DIM == LANES
    pw = 2 * B_HEAD_DIM
    n_pairs = B_HEADS // 2
    lane_p = lax.broadcasted_iota(jnp.int32, (CHUNK, pw), 1)
    row_p = lax.broadcasted_iota(jnp.int32, (CHUNK, pw), 0)
    first_head = lane_p < B_HEAD_DIM
    eye = jnp.where(row_p == (lane_p & (B_HEAD_DIM - 1)), 1.0, 0.0)
    lane_2 = lax.broadcasted_iota(jnp.int32, (CHUNK, 2 * pw), 1)
    row_2 = lax.broadcasted_iota(jnp.int32, (CHUNK, 2 * pw), 0)
    strict = (lane_2 & (B_HEAD_DIM - 1)) < row_2
    incl = (lane_2 & (B_HEAD_DIM - 1)) <= row_2
    same_head = ((lax.broadcasted_iota(jnp.int32, (pw, pw), 0) < B_HEAD_DIM)
                 == (lax.broadcasted_iota(jnp.int32, (pw, pw), 1) < B_HEAD_DIM))
    zeros_pair = jnp.zeros((CHUNK, pw), F32)
    zeros_bd = jnp.zeros((pw, pw), BF16)

    def block_diag(x):
        return jnp.concatenate(
            [jnp.where(first_head, x, 0.0), jnp.where(first_head, 0.0, x)], axis=0).astype(BF16)

    def group_body(g, carry):
        pairs = [(ci, pr) for ci in range(RWKV_GROUP) for pr in range(n_pairs)]
        rows = [pl.ds(pl.multiple_of((g * RWKV_GROUP + ci) * CHUNK, CHUNK), CHUNK)
                for ci in range(RWKV_GROUP)]
        tile_of = lambda ref, p: ref[rows[p[0]], p[1] * pw:(p[1] + 1) * pw]

        a_t, r_t, v_p, a_sc, r_sc = {}, {}, {}, {}, {}
        for p in pairs:
            a_t[p] = tile_of(at_s, p)
            r_t[p] = tile_of(rt_s, p)
            v_p[p] = tile_of(v_s, p)
            lhs = jnp.concatenate([a_t[p], r_t[p]], axis=0).astype(BF16)
            rhs = jnp.concatenate([block_diag(tile_of(bt_s, p)), block_diag(tile_of(kt_s, p))],
                                  axis=0)
            scores = _dot_nt(lhs, rhs)
            a_sc[p] = jnp.where(strict, scores[:CHUNK], 0.0)
            r_sc[p] = jnp.where(incl, scores[CHUNK:], 0.0).astype(BF16)

        power, inv, akv, v_bd = {}, {}, {}, {}
        for p in pairs:
            n = a_sc[p][:, :pw]
            inv[p] = eye + n
            power[p] = _dot(n.astype(BF16), block_diag(n))
            v_bd[p] = block_diag(v_p[p])
            akv[p] = _dot(a_sc[p][:, pw:].astype(BF16), v_bd[p])
        for _ in range(int(math.log2(CHUNK)) - 2):
            for p in pairs:
                both = _dot(jnp.concatenate([inv[p], power[p]], axis=0).astype(BF16),
                            block_diag(power[p]))
                inv[p] = inv[p] + both[:CHUNK]
                power[p] = both[CHUNK:]
        for p in pairs:
            inv[p] = inv[p] + _dot(inv[p].astype(BF16), block_diag(power[p]))

        wu, g_mat, h_mat = {}, {}, {}
        for p in pairs:
            wu[p] = _dot(inv[p].astype(BF16),
                         jnp.concatenate([block_diag(a_t[p]), block_diag(akv[p])], axis=1))
        for p in pairs:
            z_bd = jnp.concatenate(
                [jnp.concatenate([block_diag(wu[p][:, :pw]), block_diag(wu[p][:, pw:])], axis=1),
                 jnp.concatenate([zeros_bd, v_bd[p]], axis=1)], axis=0)
            g_mat[p] = _dot(r_sc[p], z_bd)
        for p in pairs:
            z_rows = jnp.concatenate(
                [wu[p], jnp.concatenate([zeros_pair, v_p[p]], axis=1)], axis=0).astype(BF16)
            bk_e = jnp.concatenate([tile_of(be_s, p), tile_of(ke_s, p)], axis=0).astype(BF16)
            h_mat[p] = _dot_tn(z_rows, bk_e)

        state = [state_ref[pr] for pr in range(n_pairs)]
        y_rows = []
        for ci in range(RWKV_GROUP):
            y_pairs = []
            for pr in range(n_pairs):
                p = (ci, pr)
                r_eff = (r_t[p] + g_mat[p][:, :pw]).astype(BF16)
                y_pairs.append(_dot_nt(r_eff, block_diag(state[pr])) + g_mat[p][:, pw:])
                trans = jnp.where(same_head, h_mat[p][:pw], 0.0).astype(BF16)
                q_full = h_mat[p][pw:]
                gam = gam_s[pl.ds(pl.multiple_of((g * RWKV_GROUP + ci) * CHUNK, CHUNK), 1),
                            pr * pw:(pr + 1) * pw]
                state[pr] = (state[pr] * gam + _dot(state[pr].astype(BF16), trans)
                             + jnp.where(first_head, q_full[:B_HEAD_DIM], q_full[B_HEAD_DIM:]))
            y_rows.append(jnp.concatenate(y_pairs, axis=1))
        for ci in range(RWKV_GROUP):
            y_s[rows[ci], :] = y_rows[ci]
        for pr in range(n_pairs):
            state_ref[pr] = state[pr]
        return carry

    lax.fori_loop(0, tile // (CHUNK * RWKV_GROUP), group_body, 0)

    y = y_s[...]
    inv_n = 1.0 / B_HEAD_DIM
    mean = _head_sums(y, hsum, split=True) * inv_n
    yc = y - mean
    var = _head_sums(yc * yc, hsum, split=False) * inv_n
    yn = yc * lax.rsqrt(var + B_LN_EPS) * ln_w + ln_b
    bonus = _head_sums(r * k * r_k, hsum, split=False) * v
    ob_ref[...] = ((yn + bonus) * gate_out).astype(BF16)


def _block_ones(n, block, lower):
    i = jnp.arange(n)
    same = (i[:, None] // block) == (i[None, :] // block)
    if lower:
        same = same & (i[None, :] <= i[:, None])
    return same.astype(BF16)


def _pad_rows(w, lo, total):
    return jnp.zeros((total, w.shape[1]), w.dtype).at[lo:lo + w.shape[0]].set(w)


def _rwkv(zb, mu, vec, w_up, a_up, g_up, v_first, vres):
    b, s, _ = zb.shape
    tile = _seq_tile(s)
    lora_in = B_DECAY_RANK + B_A_RANK
    wup_p = _pad_rows(w_up, 0, lora_in).astype(BF16)
    aup_p = _pad_rows(a_up, B_DECAY_RANK, lora_in).astype(BF16)
    consts = [mu.reshape(1, B_COLS), vec, wup_p, aup_p, g_up.astype(BF16),
              _block_ones(MXU_TILE, B_HEAD_DIM, False), _block_ones(tile, CHUNK, True),
              _block_ones(tile, CHUNK, False)]
    full = lambda a: _resident(a.shape, lambda bi, t: (0,) * a.ndim)
    seq = lambda width: pl.BlockSpec((None, tile, width), lambda bi, t: (bi, t, 0))
    in_specs = [seq(B_COLS)] + [full(a) for a in consts]
    args = [zb] + consts
    out_shape = [jax.ShapeDtypeStruct((b, s, B_WIDTH), BF16)]
    out_specs = [seq(B_WIDTH)]
    if vres is not None:
        v0, v_down, v_up = vres
        vdown_p = jnp.zeros((B_WIDTH, LANES), F32).at[:, :B_VRES_RANK].set(v_down).astype(BF16)
        vup_p = _pad_rows(v_up, 0, LANES).astype(BF16)
        extra = [v0.reshape(1, B_WIDTH), vdown_p, vup_p]
        in_specs += [seq(B_WIDTH)] + [full(a) for a in extra]
        args += [v_first] + extra
    else:
        out_shape.append(jax.ShapeDtypeStruct((b, s, B_WIDTH), F32))
        out_specs.append(seq(B_WIDTH))
    wide = lambda: pltpu.VMEM((tile, B_WIDTH), F32)
    kern = functools.partial(_rwkv_kernel, tile=tile, has_vres=vres is not None)
    outs = pl.pallas_call(
        kern,
        out_shape=tuple(out_shape),
        grid=(b, s // tile),
        in_specs=in_specs,
        out_specs=tuple(out_specs),
        scratch_shapes=[pltpu.VMEM((B_HEADS // 2, B_HEAD_DIM, 2 * B_HEAD_DIM), F32),
                        pltpu.VMEM((1, B_COLS), F32)] + [wide() for _ in range(9)],
        compiler_params=pltpu.CompilerParams(
            dimension_semantics=("parallel", "arbitrary"), vmem_limit_bytes=VMEM_LIMIT),
        name="rwkv7",
    )(*args)
    return outs if vres is None else (outs[0], None)


def _retention_kernel(mask_ref, qdec_ref, kdec_ref, cdec_ref, q_ref, k_ref, v_ref, g_ref,
                      o_ref, state_ref):
    @pl.when(pl.program_id(1) == 0)
    def _():
        state_ref[...] = jnp.zeros_like(state_ref)

    heads = range(C_HEADS)
    qk_cols = lambda h: slice(h * C_QK_DIM, (h + 1) * C_QK_DIM)
    v_cols = lambda h: slice(h * C_V_DIM, (h + 1) * C_V_DIM)
    q = [q_ref[:, qk_cols(h)] for h in heads]
    k = [k_ref[:, qk_cols(h)] for h in heads]
    v = [v_ref[:, v_cols(h)] for h in heads]
    state = [state_ref[h] for h in heads]
    scores = [(_dot_nt(q[h], k[h]) * mask_ref[h]).astype(BF16) for h in heads]
    q_dec = [(q[h].astype(F32) * qdec_ref[h]).astype(BF16) for h in heads]
    y = [_dot(scores[h], v[h]) + _dot(q_dec[h], state[h].astype(BF16)) for h in heads]
    k_dec = [(k[h].astype(F32) * kdec_ref[h]).astype(BF16) for h in heads]
    for h in heads:
        state_ref[h] = state[h] * cdec_ref[h] + _dot_tn(k_dec[h], v[h])
    for h in heads:
        yn = y[h] * lax.rsqrt(jnp.mean(y[h] * y[h], axis=-1, keepdims=True) + NORM_EPS)
        g = g_ref[:, v_cols(h)]
        o_ref[:, v_cols(h)] = (g * _sigmoid(g) * yn).astype(BF16)


def _retention_tables(tile):
    log_g = jnp.log(1.0 - 2.0 ** (-5.0 - jnp.arange(C_HEADS, dtype=F32)))[:, None, None]
    i = jnp.arange(tile)
    dist = jnp.abs(i[:, None] - i[None, :]).astype(F32)
    visible = (i[None, :] // CHUNK) <= (i[:, None] // CHUNK)
    scale = C_QK_DIM ** -0.5
    mask = jnp.where(visible[None], jnp.exp(log_g * dist[None]), 0.0) * scale
    idx = i.astype(F32)[None, :, None]
    qdec = jnp.broadcast_to(jnp.exp(log_g * (idx + 1.0)), (C_HEADS, tile, C_QK_DIM))
    kdec = jnp.broadcast_to(jnp.exp(log_g * (tile - 1.0 - idx)) * scale, (C_HEADS, tile, C_QK_DIM))
    cdec = jnp.broadcast_to(jnp.exp(log_g * tile), (C_HEADS, 1, C_V_DIM))
    return mask, qdec, kdec, cdec


def _retention(qk, v, gate):
    b, s, _ = v.shape
    tile = _seq_tile(s)
    mask, qdec, kdec, cdec = _retention_tables(tile)
    return pl.pallas_call(
        _retention_kernel,
        out_shape=jax.ShapeDtypeStruct((b, s, C_V_WIDTH), BF16),
        grid=(b, s // tile),
        in_specs=[
            _resident(mask.shape, lambda bi, t: (0, 0, 0)),
            _resident(qdec.shape, lambda bi, t: (0, 0, 0)),
            _resident(kdec.shape, lambda bi, t: (0, 0, 0)),
            _resident(cdec.shape, lambda bi, t: (0, 0, 0)),
            pl.BlockSpec((None, tile, C_QK_WIDTH), lambda bi, t: (bi, t, 0)),
            pl.BlockSpec((None, tile, C_QK_WIDTH), lambda bi, t: (bi, t, 1)),
            pl.BlockSpec((None, tile, C_V_WIDTH), lambda bi, t: (bi, t, 0)),
            pl.BlockSpec((None, tile, C_V_WIDTH), lambda bi, t: (bi, t, 0)),
        ],
        out_specs=pl.BlockSpec((None, tile, C_V_WIDTH), lambda bi, t: (bi, t, 0)),
        scratch_shapes=[pltpu.VMEM((C_HEADS, C_QK_DIM, C_V_DIM), F32)],
        compiler_params=pltpu.CompilerParams(
            dimension_semantics=("parallel", "arbitrary"),
            vmem_limit_bytes=VMEM_LIMIT),
        name="retention",
    )(mask, qdec, kdec, cdec, qk, qk, v, gate)


def kernel(x, norms, ffn_wgu, ffn_wd, even_w_in, even_w_out, diff_lam, diff_subln, rwkv_mu, rwkv_vec, rwkv_w_up, rwkv_a_up, rwkv_g_up, rwkv_v0, rwkv_v_down, rwkv_v_up, odd_w_in, odd_w_out):
    b, s, d = x.shape
    m = b * s
    wgu = ffn_wgu.astype(BF16)
    wd = ffn_wd.astype(BF16)
    even_in = even_w_in.astype(BF16)
    even_out = even_w_out.astype(BF16)
    odd_in = odd_w_in.astype(BF16)
    odd_out = odd_w_out.astype(BF16)
    subln = diff_subln.reshape(-1, 1, A_V_DIM)

    x2 = x.reshape(m, d)
    v_first = None
    for i in range(DEPTH):
        j = i // 2
        x2 = _ffn(x2, norms, wgu, wd, i, 0)
        if i % 2 == 0:
            lam_init = 0.8 - 0.6 * math.exp(-0.3 * i)
            za, zb = _inproj_even(x2, norms, even_in, i, j)
            oa = _diff_attention(za.reshape(b, s, A_COLS), diff_lam, subln, lam_init, j)
            vres = None if j == 0 else (rwkv_v0[j - 1], rwkv_v_down[j - 1], rwkv_v_up[j - 1])
            ob, v_b = _rwkv(zb.reshape(b, s, B_COLS), rwkv_mu[j], rwkv_vec[j], rwkv_w_up[j],
                            rwkv_a_up[j], rwkv_g_up[j], v_first, vres)
            if j == 0:
                v_first = v_b
            parts, w_out = [oa.reshape(m, A_WIDTH), ob.reshape(m, B_WIDTH)], even_out
        else:
            qk, v, gate = _inproj_odd(x2, norms, odd_in, i, j)
            y = _retention(qk.reshape(b, s, -1), v.reshape(b, s, -1), gate.reshape(b, s, -1))
            parts, w_out = [y.reshape(m, C_V_WIDTH)], odd_out
        x2 = _ffn(x2, norms, wgu, wd, i, 1, parts, w_out, j)
    return x2.reshape(b, s, d)
```

```python
import functools
import math

import jax
import jax.numpy as jnp
from jax import lax
from jax.experimental import pallas as pl
from jax.experimental.pallas import tpu as pltpu

F32 = jnp.float32
BF16 = jnp.bfloat16

D_MODEL = 1024
DEPTH = 4
CHUNK = 64
D_FF = 2816
NORM_EPS = 1e-6

A_HEADS = 4
A_QK_DIM = 64
A_V_DIM = 2 * A_QK_DIM
A_WIDTH = A_HEADS * A_V_DIM
A_COLS = 3 * A_WIDTH

B_HEAD_DIM = 64
B_WIDTH = D_MODEL - A_WIDTH
B_HEADS = B_WIDTH // B_HEAD_DIM
B_DECAY_RANK = 64
B_A_RANK = 64
B_GATE_RANK = 128
B_VRES_RANK = 32
B_LN_EPS = 64e-5
B_COLS = 3 * B_WIDTH + B_DECAY_RANK + B_A_RANK + B_GATE_RANK
EVEN_IN = A_COLS + B_COLS

C_HEADS = 4
C_QK_DIM = D_MODEL // C_HEADS
C_V_DIM = 2 * C_QK_DIM
C_QK_WIDTH = C_HEADS * C_QK_DIM
C_V_WIDTH = C_HEADS * C_V_DIM
ODD_IN = 2 * C_QK_WIDTH + 2 * C_V_WIDTH

LANES = 128
SUBLANES = 8
MXU_TILE = 256
ROW_TILE = 512
FFN_ROW_TILE = 1024
FFN_POST_ROW_TILE = 512
FFN_SUBTILES = 2
FF_TILE = 256
SEQ_TILE = 256
ATTN_Q_TILE = 512
ATTN_K_TILE = 256
RWKV_GROUP = 4
VMEM_LIMIT = 52 * 1024 * 1024
NEG_BIG = -1e30


def _dot(a, b):
    return jnp.dot(a, b, preferred_element_type=F32)


def _dot_nt(a, b):
    return lax.dot_general(a, b, (((1,), (1,)), ((), ())), preferred_element_type=F32)


def _dot_tn(a, b):
    return lax.dot_general(a, b, (((0,), (0,)), ((), ())), preferred_element_type=F32)


def _split2(x):
    hi = x.astype(BF16)
    lo = (x - hi.astype(F32)).astype(BF16)
    return hi, lo


def _head_sums(x, ones_block, split):
    width = ones_block.shape[0]
    out = []
    for lo_col in range(0, x.shape[1], width):
        xs = x[:, lo_col:lo_col + width]
        if split:
            hi, lo = _split2(xs)
            out.append(_dot(hi, ones_block) + _dot(lo, ones_block))
        else:
            out.append(_dot(xs.astype(BF16), ones_block))
    return jnp.concatenate(out, axis=1)


def _ones_matmul_left(m, x):
    hi, lo = _split2(x)
    return _dot(m, hi) + _dot(m, lo)


def _rms(x, g, eps=NORM_EPS):
    return x * lax.rsqrt(jnp.mean(x * x, axis=-1, keepdims=True) + eps) * g


def _sigmoid(x):
    return 1.0 / (1.0 + jnp.exp(-x))


def _row_tile(m):
    return ROW_TILE if m % ROW_TILE == 0 else m


def _seq_tile(s):
    return SEQ_TILE if s % SEQ_TILE == 0 else s


def _resident(block_shape, index_map):
    return pl.BlockSpec(block_shape, index_map, pipeline_mode=pl.Buffered(1))


def _ffn_kernel(*refs, g_pre, g_post, n_parts):
    x_ref, g_ref, wgu_ref, wd_ref = refs[:4]
    part_refs = refs[4:4 + n_parts]
    o_ref, act_ref = refs[-2:]
    sub = x_ref.shape[0] // FFN_SUBTILES
    row_sets = [slice(i * sub, (i + 1) * sub) for i in range(FFN_SUBTILES)]
    hs = []
    for rs in row_sets:
        x = x_ref[rs, :]
        if n_parts:
            wout_ref = refs[4 + n_parts]
            mix = None
            lo = 0
            for p in part_refs:
                width = p.shape[-1]
                term = _dot(p[rs, :], wout_ref[lo:lo + width, :])
                mix = term if mix is None else mix + term
                lo += width
            x = x + _rms(mix, g_ref[3:4, :])
        o_ref[rs, :] = x
        hs.append(_rms(x, g_ref[g_pre:g_pre + 1, :]).astype(BF16))
    for j in range(D_FF // FF_TILE):
        lo = j * FF_TILE
        for rs, h in zip(row_sets, hs):
            gate = _dot(h, wgu_ref[:, lo:lo + FF_TILE])
            up = _dot(h, wgu_ref[:, D_FF + lo:D_FF + lo + FF_TILE])
            act_ref[rs, lo:lo + FF_TILE] = (gate * _sigmoid(gate) * up).astype(BF16)
    ys = [_dot(act_ref[rs, :], wd_ref[...]) for rs in row_sets]
    for rs, y in zip(row_sets, ys):
        o_ref[rs, :] = o_ref[rs, :] + 0.5 * _rms(y, g_ref[g_post:g_post + 1, :])


def _ffn(x2, norms, wgu, wd, layer, which, parts=(), w_out=None, j=0):
    m, d = x2.shape
    tile = FFN_POST_ROW_TILE if parts else FFN_ROW_TILE
    tm = tile if m % tile == 0 else m
    kern = functools.partial(_ffn_kernel, g_pre=4 * which, g_post=4 * which + 1,
                             n_parts=len(parts))
    in_specs = [
        pl.BlockSpec((tm, d), lambda i: (i, 0)),
        _resident((None, 6, d), lambda i: (layer, 0, 0)),
        _resident((None, None, d, 2 * D_FF), lambda i: (layer, which, 0, 0)),
        _resident((None, None, D_FF, d), lambda i: (layer, which, 0, 0)),
    ]
    args = [x2, norms, wgu, wd]
    if parts:
        k_total = sum(p.shape[-1] for p in parts)
        in_specs += [pl.BlockSpec((tm, p.shape[-1]), lambda i: (i, 0)) for p in parts]
        in_specs.append(_resident((None, k_total, d), lambda i: (j, 0, 0)))
        args += list(parts) + [w_out]
    return pl.pallas_call(
        kern,
        out_shape=jax.ShapeDtypeStruct((m, d), F32),
        grid=(m // tm,),
        in_specs=in_specs,
        out_specs=pl.BlockSpec((tm, d), lambda i: (i, 0)),
        scratch_shapes=[pltpu.VMEM((tm, D_FF), BF16)],
        compiler_params=pltpu.CompilerParams(
            dimension_semantics=("parallel",), vmem_limit_bytes=VMEM_LIMIT),
        name="ffn_post" if parts else "ffn",
    )(*args)


def _inproj_even_kernel(x_ref, g_ref, w_ref, za_ref, zb_ref):
    h = _rms(x_ref[...], g_ref[2:3, :]).astype(BF16)
    za_ref[...] = _dot(h, w_ref[:, :A_COLS]).astype(BF16)
    zb_ref[...] = _dot(h, w_ref[:, A_COLS:])


def _inproj_even(x2, norms, w_in, layer, j):
    m, d = x2.shape
    tm = _row_tile(m)
    return pl.pallas_call(
        _inproj_even_kernel,
        out_shape=(jax.ShapeDtypeStruct((m, A_COLS), BF16),
                   jax.ShapeDtypeStruct((m, B_COLS), F32)),
        grid=(m // tm,),
        in_specs=[
            pl.BlockSpec((tm, d), lambda i: (i, 0)),
            _resident((None, 6, d), lambda i: (layer, 0, 0)),
            _resident((None, d, EVEN_IN), lambda i: (j, 0, 0)),
        ],
        out_specs=(pl.BlockSpec((tm, A_COLS), lambda i: (i, 0)),
                   pl.BlockSpec((tm, B_COLS), lambda i: (i, 0))),
        compiler_params=pltpu.CompilerParams(
            dimension_semantics=("parallel",), vmem_limit_bytes=VMEM_LIMIT),
        name="inproj_even",
    )(x2, norms, w_in)


def _inproj_odd_kernel(x_ref, g_ref, w_ref, qk_ref, v_ref, gate_ref):
    h = _rms(x_ref[...], g_ref[2:3, :]).astype(BF16)
    qk_ref[...] = _dot(h, w_ref[:, :2 * C_QK_WIDTH]).astype(BF16)
    v_ref[...] = _dot(h, w_ref[:, 2 * C_QK_WIDTH:2 * C_QK_WIDTH + C_V_WIDTH]).astype(BF16)
    gate_ref[...] = _dot(h, w_ref[:, 2 * C_QK_WIDTH + C_V_WIDTH:]).astype(BF16)


def _inproj_odd(x2, norms, w_in, layer, j):
    m, d = x2.shape
    tm = _row_tile(m)
    return pl.pallas_call(
        _inproj_odd_kernel,
        out_shape=(jax.ShapeDtypeStruct((m, 2 * C_QK_WIDTH), BF16),
                   jax.ShapeDtypeStruct((m, C_V_WIDTH), BF16),
                   jax.ShapeDtypeStruct((m, C_V_WIDTH), BF16)),
        grid=(m // tm,),
        in_specs=[
            pl.BlockSpec((tm, d), lambda i: (i, 0)),
            _resident((None, 6, d), lambda i: (layer, 0, 0)),
            _resident((None, d, ODD_IN), lambda i: (j, 0, 0)),
        ],
        out_specs=(pl.BlockSpec((tm, 2 * C_QK_WIDTH), lambda i: (i, 0)),
                   pl.BlockSpec((tm, C_V_WIDTH), lambda i: (i, 0)),
                   pl.BlockSpec((tm, C_V_WIDTH), lambda i: (i, 0))),
        compiler_params=pltpu.CompilerParams(
            dimension_semantics=("parallel",), vmem_limit_bytes=VMEM_LIMIT),
        name="inproj_odd",
    )(x2, norms, w_in)


def _lane_tiles(x):
    return [x[:, i:i + LANES] for i in range(0, x.shape[1], LANES)]


def _attn_kernel(lam_ref, subln_ref, dbias_ref, q_ref, k_ref, v_ref, o_ref, *scratch,
                 tq, tk, lam_init):
    def tile_body(qi, carry):
        rows = pl.ds(pl.multiple_of(qi * tq, tq), tq)
        _attn_tile(qi, lam_ref, subln_ref, dbias_ref, q_ref.at[rows], k_ref, v_ref, o_ref.at[rows],
                   *scratch, tq=tq, tk=tk, lam_init=lam_init)
        return carry

    lax.fori_loop(0, q_ref.shape[0] // tq, tile_body, 0)


def _attn_tile(qi, lam_ref, subln_ref, dbias_ref, q_ref, k_ref, v_ref, o_ref,
               s_ref, m_ref, l_ref, acc_ref, *, tq, tk, lam_init):
    head = pl.program_id(1)
    n_diag = tq // tk
    first_diag = qi * n_diag
    log2e = 1.0 / math.log(2.0)
    slope2 = log2e * jnp.exp(
        (head + 1).astype(F32) * jnp.full((1, 1), -8.0 / A_HEADS * math.log(2.0), F32))

    lane = lax.broadcasted_iota(jnp.int32, (1, A_V_DIM), 1)
    q = q_ref[...] * jnp.asarray(A_QK_DIM ** -0.5, BF16)
    zero = jnp.zeros_like(q)
    q_maps = (jnp.where(lane < A_QK_DIM, q, zero), jnp.where(lane >= A_QK_DIM, q, zero))
    n_maps = len(q_maps)
    col_row = lax.broadcasted_iota(jnp.int32, (1, tk), 1)

    def lane_max(m, s):
        for t in _lane_tiles(s):
            m = t if m is None else jnp.maximum(m, t)
        return m

    def pass1(blocks, first):
        m_new = [None if first else m_ref[mi] for mi in range(n_maps)]
        for kb, diag_slot in blocks:
            k = k_ref[pl.ds(pl.multiple_of(kb * tk, tk), tk), :]
            if diag_slot is None:
                bias = slope2 * (col_row + (kb * tk - qi * tq)).astype(F32)
            else:
                bias = dbias_ref[:, diag_slot * tk:(diag_slot + 1) * tk]
            for mi, qm in enumerate(q_maps):
                s = _dot_nt(qm, k) * log2e + bias
                s_ref[mi, kb] = s
                m_new[mi] = lane_max(m_new[mi], s)
        for mi in range(n_maps):
            m_ref[mi] = m_new[mi]

    def pass2(blocks, first):
        for mi in range(n_maps):
            m_row = m_ref[mi]
            l_new = None if first else l_ref[mi]
            acc_new = None if first else acc_ref[mi]
            for kb in blocks:
                tiles = [jnp.exp2(t - m_row) for t in _lane_tiles(s_ref[mi, kb])]
                for t in tiles:
                    l_new = t if l_new is None else l_new + t
                pv = _dot(jnp.concatenate(tiles, axis=1).astype(BF16),
                          v_ref[pl.ds(pl.multiple_of(kb * tk, tk), tk), :])
                acc_new = pv if acc_new is None else acc_new + pv
            l_ref[mi] = l_new
            acc_ref[mi] = acc_new

    def visible_loops(fn):
        done = 0
        for size in tuple(sz for sz in (8, 4, 2, 1) if sz >= n_diag):
            trips = (first_diag - done) // size
            base = done

            def body(it, carry, size=size, base=base):
                fn([base + it * size + g for g in range(size)])
                return carry

            lax.fori_loop(0, trips, body, 0)
            done = done + trips * size

    pass1([(first_diag + d, d) for d in range(n_diag)], True)
    visible_loops(lambda kbs: pass1([(kb, None) for kb in kbs], False))
    for mi in range(n_maps):
        m_ref[mi] = jnp.broadcast_to(jnp.max(m_ref[mi], axis=-1, keepdims=True), (tq, LANES))
    pass2([first_diag + d for d in range(n_diag)], True)
    visible_loops(lambda kbs: pass2(kbs, False))

    lv = lam_ref[...]
    lam = (jnp.exp(jnp.sum(lv[0:1] * lv[1:2], axis=-1, keepdims=True))
           - jnp.exp(jnp.sum(lv[2:3] * lv[3:4], axis=-1, keepdims=True)) + lam_init)
    outs = [acc_ref[mi] * (1.0 / jnp.sum(l_ref[mi], axis=-1, keepdims=True))
            for mi in range(n_maps)]
    o = outs[0] - lam * outs[1]
    o_ref[...] = (_rms(o, subln_ref[...]) * (1.0 - lam_init)).astype(BF16)


def _attn_diag_bias(tq):
    slopes = 2.0 ** (-8.0 / A_HEADS * jnp.arange(1, A_HEADS + 1, dtype=F32))
    i = jnp.arange(tq)
    rel = (i[:, None] - jnp.abs(i[:, None] - i[None, :])).astype(F32)
    visible = (i[None, :] // CHUNK) <= (i[:, None] // CHUNK)
    bias = slopes[:, None, None] * rel[None] / math.log(2.0)
    return jnp.where(visible[None], bias, NEG_BIG)


def _diff_attention(za, lam_vecs, subln, lam_init, j):
    b, s, _ = za.shape
    tq = ATTN_Q_TILE if s % ATTN_Q_TILE == 0 else s
    tk = ATTN_K_TILE if tq % ATTN_K_TILE == 0 else tq
    kern = functools.partial(_attn_kernel, tq=tq, tk=tk, lam_init=lam_init)
    n_maps = 2
    return pl.pallas_call(
        kern,
        out_shape=jax.ShapeDtypeStruct((b, s, A_WIDTH), BF16),
        grid=(b, A_HEADS),
        in_specs=[
            _resident((None, 4, A_QK_DIM), lambda bi, h: (j, 0, 0)),
            _resident((None, 1, A_V_DIM), lambda bi, h: (j, 0, 0)),
            pl.BlockSpec((None, tq, tq), lambda bi, h: (h, 0, 0)),
            pl.BlockSpec((None, s, A_V_DIM), lambda bi, h: (bi, 0, h)),
            pl.BlockSpec((None, s, A_V_DIM), lambda bi, h: (bi, 0, A_HEADS + h)),
            pl.BlockSpec((None, s, A_V_DIM), lambda bi, h: (bi, 0, 2 * A_HEADS + h)),
        ],
        out_specs=pl.BlockSpec((None, s, A_V_DIM), lambda bi, h: (bi, 0, h)),
        scratch_shapes=[pltpu.VMEM((n_maps, s // tk, tq, tk), F32),
                        pltpu.VMEM((n_maps, tq, LANES), F32),
                        pltpu.VMEM((n_maps, tq, LANES), F32),
                        pltpu.VMEM((n_maps, tq, A_V_DIM), F32)],
        compiler_params=pltpu.CompilerParams(
            dimension_semantics=("parallel", "parallel"),
            vmem_limit_bytes=VMEM_LIMIT),
        name="diff_attention",
    )(lam_vecs, subln, _attn_diag_bias(tq), za, za, za)


def _rwkv_kernel(*refs, tile, has_vres):
    (zb_ref, mu_ref, vec_ref, wup_ref, aup_ref, gup_ref, hsum_ref, ltri_ref, lall_ref) = refs[:9]
    if has_vres:
        vfirst_ref, v0_ref, vdown_ref, vup_ref, ob_ref = refs[9:14]
        scratch = refs[14:]
    else:
        ob_ref, vb_ref = refs[9:11]
        scratch = refs[11:]
    (state_ref, prev_ref, at_s, rt_s, bt_s, kt_s, be_s, ke_s, v_s, gam_s, y_s) = scratch

    @pl.when(pl.program_id(1) == 0)
    def _():
        state_ref[...] = jnp.zeros_like(state_ref)
        prev_ref[...] = jnp.zeros_like(prev_ref)

    z = zb_ref[...]
    rolled = pltpu.roll(z, 1, axis=0)
    first_row = lax.broadcasted_iota(jnp.int32, (SUBLANES, z.shape[1]), 0) == 0
    z_prev = jnp.concatenate(
        [jnp.where(first_row, prev_ref[...], rolled[:SUBLANES]), rolled[SUBLANES:]], axis=0)
    prev_ref[...] = z[tile - 1:tile, :]
    zs = z + (z_prev - z) * mu_ref[...]

    w3 = B_WIDTH
    r = zs[:, 0:w3]
    k = zs[:, w3:2 * w3]
    v = zs[:, 2 * w3:3 * w3]
    x_wa = zs[:, 3 * w3:3 * w3 + B_DECAY_RANK + B_A_RANK]
    x_g = zs[:, 3 * w3 + B_DECAY_RANK + B_A_RANK:]
    w0, a0, k_k, k_a, r_k, ln_w, ln_b = (vec_ref[i:i + 1, :] for i in range(7))

    d_in = -(w0 + _dot(jnp.tanh(x_wa).astype(BF16), wup_ref[...]))
    softplus = jnp.maximum(d_in, 0.0) + jnp.log(1.0 + jnp.exp(-jnp.abs(d_in)))
    log_decay = -jnp.exp(-softplus - 0.5)

    if has_vres:
        low = _dot(v.astype(BF16), vdown_ref[...])
        mixed = _dot(low.astype(BF16), vup_ref[...])
        v = v + (vfirst_ref[...] - v) * _sigmoid(v0_ref[...] + mixed)
    else:
        vb_ref[...] = v
    gate_a = _sigmoid(a0 + _dot(x_wa.astype(BF16), aup_ref[...]))
    gate_out = _dot(_sigmoid(x_g).astype(BF16), gup_ref[...])

    hsum = hsum_ref[...]
    kk = k * k_k
    kk = kk / jnp.maximum(jnp.sqrt(_head_sums(kk * kk, hsum, split=True)), 1e-12)
    k = k * (1.0 + (gate_a - 1.0) * k_a)
    b_vec = kk * gate_a

    cum = _ones_matmul_left(ltri_ref[...], log_decay)
    total = _ones_matmul_left(lall_ref[...], log_decay)
    g_inc = jnp.exp(cum)
    g_inv = jnp.exp(-cum)
    g_tail = jnp.exp(total - cum)
    rt_s[...] = r * g_inc
    kt_s[...] = k * g_inv
    bt_s[...] = b_vec * g_inv
    at_s[...] = -kk * jnp.exp(cum - log_decay)
    ke_s[...] = k * g_tail
    be_s[...] = b_vec * g_tail
    gam_s[...] = jnp.exp(total)
    v_s[...] = v

    assert B_HEAD_DIM == CHUNK and 2 * B_HEAD_DIM == LANES
    pw = 2 * B_HEAD_DIM
    n_pairs = B_HEADS // 2
    lane_p = lax.broadcasted_iota(jnp.int32, (CHUNK, pw), 1)
    row_p = lax.broadcasted_iota(jnp.int32, (CHUNK, pw), 0)
    first_head = lane_p < B_HEAD_DIM
    eye = jnp.where(row_p == (lane_p & (B_HEAD_DIM - 1)), 1.0, 0.0)
    lane_2 = lax.broadcasted_iota(jnp.int32, (CHUNK, 2 * pw), 1)
    row_2 = lax.broadcasted_iota(jnp.int32, (CHUNK, 2 * pw), 0)
    strict = (lane_2 & (B_HEAD_DIM - 1)) < row_2
    incl = (lane_2 & (B_HEAD_DIM - 1)) <= row_2
    same_head = ((lax.broadcasted_iota(jnp.int32, (pw, pw), 0) < B_HEAD_DIM)
                 == (lax.broadcasted_iota(jnp.int32, (pw, pw), 1) < B_HEAD_DIM))
    zeros_pair = jnp.zeros((CHUNK, pw), F32)
    zeros_bd = jnp.zeros((pw, pw), BF16)

    def block_diag(x):
        return jnp.concatenate(
            [jnp.where(first_head, x, 0.0), jnp.where(first_head, 0.0, x)], axis=0).astype(BF16)

    def group_body(g, carry):
        pairs = [(ci, pr) for ci in range(RWKV_GROUP) for pr in range(n_pairs)]
        rows = [pl.ds(pl.multiple_of((g * RWKV_GROUP + ci) * CHUNK, CHUNK), CHUNK)
                for ci in range(RWKV_GROUP)]
        tile_of = lambda ref, p: ref[rows[p[0]], p[1] * pw:(p[1] + 1) * pw]

        a_t, r_t, v_p, a_sc, r_sc = {}, {}, {}, {}, {}
        for p in pairs:
            a_t[p] = tile_of(at_s, p)
            r_t[p] = tile_of(rt_s, p)
            v_p[p] = tile_of(v_s, p)
            lhs = jnp.concatenate([a_t[p], r_t[p]], axis=0).astype(BF16)
            rhs = jnp.concatenate([block_diag(tile_of(bt_s, p)), block_diag(tile_of(kt_s, p))],
                                  axis=0)
            scores = _dot_nt(lhs, rhs)
            a_sc[p] = jnp.where(strict, scores[:CHUNK], 0.0)
            r_sc[p] = jnp.where(incl, scores[CHUNK:], 0.0).astype(BF16)

        power, inv, akv, v_bd = {}, {}, {}, {}
        for p in pairs:
            n = a_sc[p][:, :pw]
            inv[p] = eye + n
            power[p] = _dot(n.astype(BF16), block_diag(n))
            v_bd[p] = block_diag(v_p[p])
            akv[p] = _dot(a_sc[p][:, pw:].astype(BF16), v_bd[p])
        for _ in range(int(math.log2(CHUNK)) - 2):
            for p in pairs:
                both = _dot(jnp.concatenate([inv[p], power[p]], axis=0).astype(BF16),
                            block_diag(power[p]))
                inv[p] = inv[p] + both[:CHUNK]
                power[p] = both[CHUNK:]
        for p in pairs:
            inv[p] = inv[p] + _dot(inv[p].astype(BF16), block_diag(power[p]))

        wu, g_mat, h_mat = {}, {}, {}
        for p in pairs:
            wu[p] = _dot(inv[p].astype(BF16),
                         jnp.concatenate([block_diag(a_t[p]), block_diag(akv[p])], axis=1))
        for p in pairs:
            z_bd = jnp.concatenate(
                [jnp.concatenate([block_diag(wu[p][:, :pw]), block_diag(wu[p][:, pw:])], axis=1),
                 jnp.concatenate([zeros_bd, v_bd[p]], axis=1)], axis=0)
            g_mat[p] = _dot(r_sc[p], z_bd)
        for p in pairs:
            z_rows = jnp.concatenate(
                [wu[p], jnp.concatenate([zeros_pair, v_p[p]], axis=1)], axis=0).astype(BF16)
            bk_e = jnp.concatenate([tile_of(be_s, p), tile_of(ke_s, p)], axis=0).astype(BF16)
            h_mat[p] = _dot_tn(z_rows, bk_e)

        state = [state_ref[pr] for pr in range(n_pairs)]
        y_rows = []
        for ci in range(RWKV_GROUP):
            y_pairs = []
            for pr in range(n_pairs):
                p = (ci, pr)
                r_eff = (r_t[p] + g_mat[p][:, :pw]).astype(BF16)
                y_pairs.append(_dot_nt(r_eff, block_diag(state[pr])) + g_mat[p][:, pw:])
                trans = jnp.where(same_head, h_mat[p][:pw], 0.0).astype(BF16)
                q_full = h_mat[p][pw:]
                gam = gam_s[pl.ds(pl.multiple_of((g * RWKV_GROUP + ci) * CHUNK, CHUNK), 1),
                            pr * pw:(pr + 1) * pw]
                state[pr] = (state[pr] * gam + _dot(state[pr].astype(BF16), trans)
                             + jnp.where(first_head, q_full[:B_HEAD_DIM], q_full[B_HEAD_DIM:]))
            y_rows.append(jnp.concatenate(y_pairs, axis=1))
        for ci in range(RWKV_GROUP):
            y_s[rows[ci], :] = y_rows[ci]
        for pr in range(n_pairs):
            state_ref[pr] = state[pr]
        return carry

    lax.fori_loop(0, tile // (CHUNK * RWKV_GROUP), group_body, 0)

    y = y_s[...]
    inv_n = 1.0 / B_HEAD_DIM
    mean = _head_sums(y, hsum, split=True) * inv_n
    yc = y - mean
    var = _head_sums(yc * yc, hsum, split=False) * inv_n
    yn = yc * lax.rsqrt(var + B_LN_EPS) * ln_w + ln_b
    bonus = _head_sums(r * k * r_k, hsum, split=False) * v
    ob_ref[...] = ((yn + bonus) * gate_out).astype(BF16)


def _block_ones(n, block, lower):
    i = jnp.arange(n)
    same = (i[:, None] // block) == (i[None, :] // block)
    if lower:
        same = same & (i[None, :] <= i[:, None])
    return same.astype(BF16)


def _pad_rows(w, lo, total):
    return jnp.zeros((total, w.shape[1]), w.dtype).at[lo:lo + w.shape[0]].set(w)


def _rwkv(zb, mu, vec, w_up, a_up, g_up, v_first, vres):
    b, s, _ = zb.shape
    tile = _seq_tile(s)
    lora_in = B_DECAY_RANK + B_A_RANK
    wup_p = _pad_rows(w_up, 0, lora_in).astype(BF16)
    aup_p = _pad_rows(a_up, B_DECAY_RANK, lora_in).astype(BF16)
    consts = [mu.reshape(1, B_COLS), vec, wup_p, aup_p, g_up.astype(BF16),
              _block_ones(MXU_TILE, B_HEAD_DIM, False), _block_ones(tile, CHUNK, True),
              _block_ones(tile, CHUNK, False)]
    full = lambda a: _resident(a.shape, lambda bi, t: (0,) * a.ndim)
    seq = lambda width: pl.BlockSpec((None, tile, width), lambda bi, t: (bi, t, 0))
    in_specs = [seq(B_COLS)] + [full(a) for a in consts]
    args = [zb] + consts
    out_shape = [jax.ShapeDtypeStruct((b, s, B_WIDTH), BF16)]
    out_specs = [seq(B_WIDTH)]
    if vres is not None:
        v0, v_down, v_up = vres
        vdown_p = jnp.zeros((B_WIDTH, LANES), F32).at[:, :B_VRES_RANK].set(v_down).astype(BF16)
        vup_p = _pad_rows(v_up, 0, LANES).astype(BF16)
        extra = [v0.reshape(1, B_WIDTH), vdown_p, vup_p]
        in_specs += [seq(B_WIDTH)] + [full(a) for a in extra]
        args += [v_first] + extra
    else:
        out_shape.append(jax.ShapeDtypeStruct((b, s, B_WIDTH), F32))
        out_specs.append(seq(B_WIDTH))
    wide = lambda: pltpu.VMEM((tile, B_WIDTH), F32)
    kern = functools.partial(_rwkv_kernel, tile=tile, has_vres=vres is not None)
    outs = pl.pallas_call(
        kern,
        out_shape=tuple(out_shape),
        grid=(b, s // tile),
        in_specs=in_specs,
        out_specs=tuple(out_specs),
        scratch_shapes=[pltpu.VMEM((B_HEADS // 2, B_HEAD_DIM, 2 * B_HEAD_DIM), F32),
                        pltpu.VMEM((1, B_COLS), F32)] + [wide() for _ in range(9)],
        compiler_params=pltpu.CompilerParams(
            dimension_semantics=("parallel", "arbitrary"), vmem_limit_bytes=VMEM_LIMIT),
        name="rwkv7",
    )(*args)
    return outs if vres is None else (outs[0], None)


def _retention_kernel(mask_ref, qdec_ref, kdec_ref, cdec_ref, q_ref, k_ref, v_ref, g_ref,
                      o_ref, state_ref):
    @pl.when(pl.program_id(1) == 0)
    def _():
        state_ref[...] = jnp.zeros_like(state_ref)

    heads = range(C_HEADS)
    qk_cols = lambda h: slice(h * C_QK_DIM, (h + 1) * C_QK_DIM)
    v_cols = lambda h: slice(h * C_V_DIM, (h + 1) * C_V_DIM)
    q = [q_ref[:, qk_cols(h)] for h in heads]
    k = [k_ref[:, qk_cols(h)] for h in heads]
    v = [v_ref[:, v_cols(h)] for h in heads]
    state = [state_ref[h] for h in heads]
    scores = [(_dot_nt(q[h], k[h]) * mask_ref[h]).astype(BF16) for h in heads]
    q_dec = [(q[h].astype(F32) * qdec_ref[h]).astype(BF16) for h in heads]
    y = [_dot(scores[h], v[h]) + _dot(q_dec[h], state[h].astype(BF16)) for h in heads]
    k_dec = [(k[h].astype(F32) * kdec_ref[h]).astype(BF16) for h in heads]
    for h in heads:
        state_ref[h] = state[h] * cdec_ref[h] + _dot_tn(k_dec[h], v[h])
    for h in heads:
        yn = y[h] * lax.rsqrt(jnp.mean(y[h] * y[h], axis=-1, keepdims=True) + NORM_EPS)
        g = g_ref[:, v_cols(h)].astype(F32)
        o_ref[:, v_cols(h)] = (g * _sigmoid(g) * yn).astype(BF16)


def _retention_tables(tile):
    log_g = jnp.log(1.0 - 2.0 ** (-5.0 - jnp.arange(C_HEADS, dtype=F32)))[:, None, None]
    i = jnp.arange(tile)
    dist = jnp.abs(i[:, None] - i[None, :]).astype(F32)
    visible = (i[None, :] // CHUNK) <= (i[:, None] // CHUNK)
    scale = C_QK_DIM ** -0.5
    mask = jnp.where(visible[None], jnp.exp(log_g * dist[None]), 0.0) * scale
    idx = i.astype(F32)[None, :, None]
    qdec = jnp.broadcast_to(jnp.exp(log_g * (idx + 1.0)), (C_HEADS, tile, C_QK_DIM))
    kdec = jnp.broadcast_to(jnp.exp(log_g * (tile - 1.0 - idx)) * scale, (C_HEADS, tile, C_QK_DIM))
    cdec = jnp.broadcast_to(jnp.exp(log_g * tile), (C_HEADS, 1, C_V_DIM))
    return mask, qdec, kdec, cdec


def _retention(qk, v, gate):
    b, s, _ = v.shape
    tile = _seq_tile(s)
    mask, qdec, kdec, cdec = _retention_tables(tile)
    return pl.pallas_call(
        _retention_kernel,
        out_shape=jax.ShapeDtypeStruct((b, s, C_V_WIDTH), BF16),
        grid=(b, s // tile),
        in_specs=[
            _resident(mask.shape, lambda bi, t: (0, 0, 0)),
            _resident(qdec.shape, lambda bi, t: (0, 0, 0)),
            _resident(kdec.shape, lambda bi, t: (0, 0, 0)),
            _resident(cdec.shape, lambda bi, t: (0, 0, 0)),
            pl.BlockSpec((None, tile, C_QK_WIDTH), lambda bi, t: (bi, t, 0)),
            pl.BlockSpec((None, tile, C_QK_WIDTH), lambda bi, t: (bi, t, 1)),
            pl.BlockSpec((None, tile, C_V_WIDTH), lambda bi, t: (bi, t, 0)),
            pl.BlockSpec((None, tile, C_V_WIDTH), lambda bi, t: (bi, t, 0)),
        ],
        out_specs=pl.BlockSpec((None, tile, C_V_WIDTH), lambda bi, t: (bi, t, 0)),
        scratch_shapes=[pltpu.VMEM((C_HEADS, C_QK_DIM, C_V_DIM), F32)],
        compiler_params=pltpu.CompilerParams(
            dimension_semantics=("parallel", "arbitrary"),
            vmem_limit_bytes=VMEM_LIMIT),
        name="retention",
    )(mask, qdec, kdec, cdec, qk, qk, v, gate)


def kernel(x, norms, ffn_wgu, ffn_wd, even_w_in, even_w_out, diff_lam, diff_subln, rwkv_mu, rwkv_vec, rwkv_w_up, rwkv_a_up, rwkv_g_up, rwkv_v0, rwkv_v_down, rwkv_v_up, odd_w_in, odd_w_out):
    b, s, d = x.shape
    m = b * s
    wgu = ffn_wgu.astype(BF16)
    wd = ffn_wd.astype(BF16)
    even_in = even_w_in.astype(BF16)
    even_out = even_w_out.astype(BF16)
    odd_in = odd_w_in.astype(BF16)
    odd_out = odd_w_out.astype(BF16)
    subln = diff_subln.reshape(-1, 1, A_V_DIM)

    x2 = x.reshape(m, d)
    v_first = None
    for i in range(DEPTH):
        j = i // 2
        x2 = _ffn(x2, norms, wgu, wd, i, 0)
        if i % 2 == 0:
            lam_init = 0.8 - 0.6 * math.exp(-0.3 * i)
            za, zb = _inproj_even(x2, norms, even_in, i, j)
            oa = _diff_attention(za.reshape(b, s, A_COLS), diff_lam, subln, lam_init, j)
            vres = None if j == 0 else (rwkv_v0[j - 1], rwkv_v_down[j - 1], rwkv_v_up[j - 1])
            ob, v_b = _rwkv(zb.reshape(b, s, B_COLS), rwkv_mu[j], rwkv_vec[j], rwkv_w_up[j],
                            rwkv_a_up[j], rwkv_g_up[j], v_first, vres)
            if j == 0:
                v_first = v_b
            parts, w_out = [oa.reshape(m, A_WIDTH), ob.reshape(m, B_WIDTH)], even_out
        else:
            qk, v, gate = _inproj_odd(x2, norms, odd_in, i, j)
            y = _retention(qk.reshape(b, s, -1), v.reshape(b, s, -1), gate.reshape(b, s, -1))
            parts, w_out = [y.reshape(m, C_V_WIDTH)], odd_out
        x2 = _ffn(x2, norms, wgu, wd, i, 1, parts, w_out, j)
    return x2.reshape(b, s, d)
```

```python
import functools
import math

import jax
import jax.numpy as jnp
from jax import lax
from jax.experimental import pallas as pl
from jax.experimental.pallas import tpu as pltpu

F32 = jnp.float32
BF16 = jnp.bfloat16

D_MODEL = 1024
DEPTH = 4
CHUNK = 64
D_FF = 2816
NORM_EPS = 1e-6

A_HEADS = 4
A_QK_DIM = 64
A_V_DIM = 2 * A_QK_DIM
A_WIDTH = A_HEADS * A_V_DIM
A_COLS = 3 * A_WIDTH

B_HEAD_DIM = 64
B_WIDTH = D_MODEL - A_WIDTH
B_HEADS = B_WIDTH // B_HEAD_DIM
B_DECAY_RANK = 64
B_A_RANK = 64
B_GATE_RANK = 128
B_VRES_RANK = 32
B_LN_EPS = 64e-5
B_COLS = 3 * B_WIDTH + B_DECAY_RANK + B_A_RANK + B_GATE_RANK
EVEN_IN = A_COLS + B_COLS

C_HEADS = 4
C_QK_DIM = D_MODEL // C_HEADS
C_V_DIM = 2 * C_QK_DIM
C_QK_WIDTH = C_HEADS * C_QK_DIM
C_V_WIDTH = C_HEADS * C_V_DIM
ODD_IN = 2 * C_QK_WIDTH + 2 * C_V_WIDTH

LANES = 128
SUBLANES = 8
MXU_TILE = 256
ROW_TILE = 512
FFN_ROW_TILE = 1024
FFN_POST_ROW_TILE = 512
FFN_SUBTILES = 2
FF_TILE = 256
SEQ_TILE = 256
ATTN_Q_TILE = 512
ATTN_K_TILE = 256
ATTN_GROUP = 8
RWKV_GROUP = 4
VMEM_LIMIT = 52 * 1024 * 1024
ATTN_VMEM_LIMIT = 62 * 1024 * 1024
NEG_BIG = -1e30


def _dot(a, b):
    return jnp.dot(a, b, preferred_element_type=F32)


def _dot_nt(a, b):
    return lax.dot_general(a, b, (((1,), (1,)), ((), ())), preferred_element_type=F32)


def _dot_tn(a, b):
    return lax.dot_general(a, b, (((0,), (0,)), ((), ())), preferred_element_type=F32)


def _split2(x):
    hi = x.astype(BF16)
    lo = (x - hi.astype(F32)).astype(BF16)
    return hi, lo


def _head_sums(x, ones_block, split):
    width = ones_block.shape[0]
    out = []
    for lo_col in range(0, x.shape[1], width):
        xs = x[:, lo_col:lo_col + width]
        if split:
            hi, lo = _split2(xs)
            out.append(_dot(hi, ones_block) + _dot(lo, ones_block))
        else:
            out.append(_dot(xs.astype(BF16), ones_block))
    return jnp.concatenate(out, axis=1)


def _ones_matmul_left(m, x):
    hi, lo = _split2(x)
    return _dot(m, hi) + _dot(m, lo)


def _rms(x, g, eps=NORM_EPS):
    return x * lax.rsqrt(jnp.mean(x * x, axis=-1, keepdims=True) + eps) * g


def _sigmoid(x):
    return 1.0 / (1.0 + jnp.exp(-x))


def _row_tile(m):
    return ROW_TILE if m % ROW_TILE == 0 else m


def _seq_tile(s):
    return SEQ_TILE if s % SEQ_TILE == 0 else s


def _resident(block_shape, index_map):
    return pl.BlockSpec(block_shape, index_map, pipeline_mode=pl.Buffered(1))


def _ffn_kernel(*refs, g_pre, g_post, n_parts):
    x_ref, g_ref, wgu_ref, wd_ref = refs[:4]
    part_refs = refs[4:4 + n_parts]
    o_ref, act_ref = refs[-2:]
    sub = x_ref.shape[0] // FFN_SUBTILES
    row_sets = [slice(i * sub, (i + 1) * sub) for i in range(FFN_SUBTILES)]
    hs = []
    for rs in row_sets:
        x = x_ref[rs, :]
        if n_parts:
            wout_ref = refs[4 + n_parts]
            mix = None
            lo = 0
            for p in part_refs:
                width = p.shape[-1]
                term = _dot(p[rs, :], wout_ref[lo:lo + width, :])
                mix = term if mix is None else mix + term
                lo += width
            x = x + _rms(mix, g_ref[3:4, :])
        o_ref[rs, :] = x
        hs.append(_rms(x, g_ref[g_pre:g_pre + 1, :]).astype(BF16))
    for j in range(D_FF // FF_TILE):
        lo = j * FF_TILE
        for rs, h in zip(row_sets, hs):
            gate = _dot(h, wgu_ref[:, lo:lo + FF_TILE])
            up = _dot(h, wgu_ref[:, D_FF + lo:D_FF + lo + FF_TILE])
            act_ref[rs, lo:lo + FF_TILE] = (gate * _sigmoid(gate) * up).astype(BF16)
    ys = [_dot(act_ref[rs, :], wd_ref[...]) for rs in row_sets]
    for rs, y in zip(row_sets, ys):
        o_ref[rs, :] = o_ref[rs, :] + 0.5 * _rms(y, g_ref[g_post:g_post + 1, :])


def _ffn(x2, norms, wgu, wd, layer, which, parts=(), w_out=None, j=0):
    m, d = x2.shape
    tile = FFN_POST_ROW_TILE if parts else FFN_ROW_TILE
    tm = tile if m % tile == 0 else m
    kern = functools.partial(_ffn_kernel, g_pre=4 * which, g_post=4 * which + 1,
                             n_parts=len(parts))
    in_specs = [
        pl.BlockSpec((tm, d), lambda i: (i, 0)),
        _resident((None, 6, d), lambda i: (layer, 0, 0)),
        _resident((None, None, d, 2 * D_FF), lambda i: (layer, which, 0, 0)),
        _resident((None, None, D_FF, d), lambda i: (layer, which, 0, 0)),
    ]
    args = [x2, norms, wgu, wd]
    if parts:
        k_total = sum(p.shape[-1] for p in parts)
        in_specs += [pl.BlockSpec((tm, p.shape[-1]), lambda i: (i, 0)) for p in parts]
        in_specs.append(_resident((None, k_total, d), lambda i: (j, 0, 0)))
        args += list(parts) + [w_out]
    return pl.pallas_call(
        kern,
        out_shape=jax.ShapeDtypeStruct((m, d), F32),
        grid=(m // tm,),
        in_specs=in_specs,
        out_specs=pl.BlockSpec((tm, d), lambda i: (i, 0)),
        scratch_shapes=[pltpu.VMEM((tm, D_FF), BF16)],
        compiler_params=pltpu.CompilerParams(
            dimension_semantics=("parallel",), vmem_limit_bytes=VMEM_LIMIT),
        name="ffn_post" if parts else "ffn",
    )(*args)


def _inproj_even_kernel(x_ref, g_ref, w_ref, za_ref, zb_ref):
    h = _rms(x_ref[...], g_ref[2:3, :]).astype(BF16)
    za_ref[...] = _dot(h, w_ref[:, :A_COLS]).astype(BF16)
    zb_ref[...] = _dot(h, w_ref[:, A_COLS:])


def _inproj_even(x2, norms, w_in, layer, j):
    m, d = x2.shape
    tm = _row_tile(m)
    return pl.pallas_call(
        _inproj_even_kernel,
        out_shape=(jax.ShapeDtypeStruct((m, A_COLS), BF16),
                   jax.ShapeDtypeStruct((m, B_COLS), F32)),
        grid=(m // tm,),
        in_specs=[
            pl.BlockSpec((tm, d), lambda i: (i, 0)),
            _resident((None, 6, d), lambda i: (layer, 0, 0)),
            _resident((None, d, EVEN_IN), lambda i: (j, 0, 0)),
        ],
        out_specs=(pl.BlockSpec((tm, A_COLS), lambda i: (i, 0)),
                   pl.BlockSpec((tm, B_COLS), lambda i: (i, 0))),
        compiler_params=pltpu.CompilerParams(
            dimension_semantics=("parallel",), vmem_limit_bytes=VMEM_LIMIT),
        name="inproj_even",
    )(x2, norms, w_in)


def _inproj_odd_kernel(x_ref, g_ref, w_ref, qk_ref, v_ref, gate_ref):
    h = _rms(x_ref[...], g_ref[2:3, :]).astype(BF16)
    qk_ref[...] = _dot(h, w_ref[:, :2 * C_QK_WIDTH]).astype(BF16)
    v_ref[...] = _dot(h, w_ref[:, 2 * C_QK_WIDTH:2 * C_QK_WIDTH + C_V_WIDTH]).astype(BF16)
    gate_ref[...] = _dot(h, w_ref[:, 2 * C_QK_WIDTH + C_V_WIDTH:]).astype(BF16)


def _inproj_odd(x2, norms, w_in, layer, j):
    m, d = x2.shape
    tm = _row_tile(m)
    return pl.pallas_call(
        _inproj_odd_kernel,
        out_shape=(jax.ShapeDtypeStruct((m, 2 * C_QK_WIDTH), BF16),
                   jax.ShapeDtypeStruct((m, C_V_WIDTH), BF16),
                   jax.ShapeDtypeStruct((m, C_V_WIDTH), BF16)),
        grid=(m // tm,),
        in_specs=[
            pl.BlockSpec((tm, d), lambda i: (i, 0)),
            _resident((None, 6, d), lambda i: (layer, 0, 0)),
            _resident((None, d, ODD_IN), lambda i: (j, 0, 0)),
        ],
        out_specs=(pl.BlockSpec((tm, 2 * C_QK_WIDTH), lambda i: (i, 0)),
                   pl.BlockSpec((tm, C_V_WIDTH), lambda i: (i, 0)),
                   pl.BlockSpec((tm, C_V_WIDTH), lambda i: (i, 0))),
        compiler_params=pltpu.CompilerParams(
            dimension_semantics=("parallel",), vmem_limit_bytes=VMEM_LIMIT),
        name="inproj_odd",
    )(x2, norms, w_in)


def _lane_tiles(x):
    return [x[:, i:i + LANES] for i in range(0, x.shape[1], LANES)]


def _attn_kernel(lam_ref, subln_ref, dbias_ref, q_ref, k_ref, v_ref, o_ref, *scratch,
                 tq, tk, lam_init):
    for qi in range(q_ref.shape[0] // tq):
        rows = pl.ds(qi * tq, tq)
        _attn_tile(qi, lam_ref, subln_ref, dbias_ref, q_ref.at[rows], k_ref, v_ref, o_ref.at[rows],
                   *scratch, tq=tq, tk=tk, lam_init=lam_init)


def _attn_tile(qi, lam_ref, subln_ref, dbias_ref, q_ref, k_ref, v_ref, o_ref,
               s_ref, m_ref, l_ref, acc_ref, *, tq, tk, lam_init):
    head = pl.program_id(1)
    n_diag = tq // tk
    first_diag = qi * n_diag
    log2e = 1.0 / math.log(2.0)
    slope2 = log2e * jnp.exp(
        (head + 1).astype(F32) * jnp.full((1, 1), -8.0 / A_HEADS * math.log(2.0), F32))

    lane = lax.broadcasted_iota(jnp.int32, (1, A_V_DIM), 1)
    q = q_ref[...] * jnp.asarray(A_QK_DIM ** -0.5, BF16)
    zero = jnp.zeros_like(q)
    q_maps = (jnp.where(lane < A_QK_DIM, q, zero), jnp.where(lane >= A_QK_DIM, q, zero))
    n_maps = len(q_maps)
    col_row = lax.broadcasted_iota(jnp.int32, (1, tk), 1)

    def lane_max(m, s):
        for t in _lane_tiles(s):
            m = t if m is None else jnp.maximum(m, t)
        return m

    def pass1(blocks, first):
        m_new = [None if first else m_ref[mi] for mi in range(n_maps)]
        for kb, diag_slot in blocks:
            k = k_ref[pl.ds(kb * tk, tk), :]
            if diag_slot is None:
                bias = slope2 * (col_row + (kb * tk - qi * tq)).astype(F32)
            else:
                bias = dbias_ref[:, diag_slot * tk:(diag_slot + 1) * tk]
            for mi, qm in enumerate(q_maps):
                s = _dot_nt(qm, k) * log2e + bias
                s_ref[mi, kb] = s
                m_new[mi] = lane_max(m_new[mi], s)
        for mi in range(n_maps):
            m_ref[mi] = m_new[mi]

    def pass2(blocks, first):
        for mi in range(n_maps):
            m_row = m_ref[mi]
            l_new = None if first else l_ref[mi]
            acc_new = None if first else acc_ref[mi]
            for kb in blocks:
                tiles = [jnp.exp2(t - m_row) for t in _lane_tiles(s_ref[mi, kb])]
                for t in tiles:
                    l_new = t if l_new is None else l_new + t
                pv = _dot(jnp.concatenate(tiles, axis=1).astype(BF16), v_ref[pl.ds(kb * tk, tk), :])
                acc_new = pv if acc_new is None else acc_new + pv
            l_ref[mi] = l_new
            acc_ref[mi] = acc_new

    blocks = [(kb, None) for kb in range(first_diag)] + [(first_diag + d, d) for d in range(n_diag)]
    groups = [blocks[i:i + ATTN_GROUP] for i in range(0, len(blocks), ATTN_GROUP)]
    for gi, group in enumerate(groups):
        pass1(group, gi == 0)
    for mi in range(n_maps):
        m_ref[mi] = jnp.broadcast_to(jnp.max(m_ref[mi], axis=-1, keepdims=True), (tq, LANES))
    for gi, group in enumerate(groups):
        pass2([kb for kb, _ in group], gi == 0)

    lv = lam_ref[...]
    lam = (jnp.exp(jnp.sum(lv[0:1] * lv[1:2], axis=-1, keepdims=True))
           - jnp.exp(jnp.sum(lv[2:3] * lv[3:4], axis=-1, keepdims=True)) + lam_init)
    outs = [acc_ref[mi] * (1.0 / jnp.sum(l_ref[mi], axis=-1, keepdims=True))
            for mi in range(n_maps)]
    o = outs[0] - lam * outs[1]
    o_ref[...] = (_rms(o, subln_ref[...]) * (1.0 - lam_init)).astype(BF16)


def _attn_diag_bias(tq):
    slopes = 2.0 ** (-8.0 / A_HEADS * jnp.arange(1, A_HEADS + 1, dtype=F32))
    i = jnp.arange(tq)
    rel = (i[:, None] - jnp.abs(i[:, None] - i[None, :])).astype(F32)
    visible = (i[None, :] // CHUNK) <= (i[:, None] // CHUNK)
    bias = slopes[:, None, None] * rel[None] / math.log(2.0)
    return jnp.where(visible[None], bias, NEG_BIG)


def _diff_attention(za, lam_vecs, subln, lam_init, j):
    b, s, _ = za.shape
    tq = ATTN_Q_TILE if s % ATTN_Q_TILE == 0 else s
    tk = ATTN_K_TILE if tq % ATTN_K_TILE == 0 else tq
    kern = functools.partial(_attn_kernel, tq=tq, tk=tk, lam_init=lam_init)
    n_maps = 2
    return pl.pallas_call(
        kern,
        out_shape=jax.ShapeDtypeStruct((b, s, A_WIDTH), BF16),
        grid=(b, A_HEADS),
        in_specs=[
            _resident((None, 4, A_QK_DIM), lambda bi, h: (j, 0, 0)),
            _resident((None, 1, A_V_DIM), lambda bi, h: (j, 0, 0)),
            pl.BlockSpec((None, tq, tq), lambda bi, h: (h, 0, 0)),
            pl.BlockSpec((None, s, A_V_DIM), lambda bi, h: (bi, 0, h)),
            pl.BlockSpec((None, s, A_V_DIM), lambda bi, h: (bi, 0, A_HEADS + h)),
            pl.BlockSpec((None, s, A_V_DIM), lambda bi, h: (bi, 0, 2 * A_HEADS + h)),
        ],
        out_specs=pl.BlockSpec((None, s, A_V_DIM), lambda bi, h: (bi, 0, h)),
        scratch_shapes=[pltpu.VMEM((n_maps, s // tk, tq, tk), F32),
                        pltpu.VMEM((n_maps, tq, LANES), F32),
                        pltpu.VMEM((n_maps, tq, LANES), F32),
                        pltpu.VMEM((n_maps, tq, A_V_DIM), F32)],
        compiler_params=pltpu.CompilerParams(
            dimension_semantics=("parallel", "parallel"),
            vmem_limit_bytes=ATTN_VMEM_LIMIT),
        name="diff_attention",
    )(lam_vecs, subln, _attn_diag_bias(tq), za, za, za)


def _rwkv_kernel(*refs, tile, has_vres):
    (zb_ref, mu_ref, vec_ref, wup_ref, aup_ref, gup_ref, hsum_ref, ltri_ref, lall_ref) = refs[:9]
    if has_vres:
        vfirst_ref, v0_ref, vdown_ref, vup_ref, ob_ref = refs[9:14]
        scratch = refs[14:]
    else:
        ob_ref, vb_ref = refs[9:11]
        scratch = refs[11:]
    (state_ref, prev_ref, at_s, rt_s, bt_s, kt_s, be_s, ke_s, v_s, gam_s, y_s) = scratch

    @pl.when(pl.program_id(1) == 0)
    def _():
        state_ref[...] = jnp.zeros_like(state_ref)
        prev_ref[...] = jnp.zeros_like(prev_ref)

    z = zb_ref[...]
    rolled = pltpu.roll(z, 1, axis=0)
    first_row = lax.broadcasted_iota(jnp.int32, (SUBLANES, z.shape[1]), 0) == 0
    z_prev = jnp.concatenate(
        [jnp.where(first_row, prev_ref[...], rolled[:SUBLANES]), rolled[SUBLANES:]], axis=0)
    prev_ref[...] = z[tile - 1:tile, :]
    zs = z + (z_prev - z) * mu_ref[...]

    w3 = B_WIDTH
    r = zs[:, 0:w3]
    k = zs[:, w3:2 * w3]
    v = zs[:, 2 * w3:3 * w3]
    x_wa = zs[:, 3 * w3:3 * w3 + B_DECAY_RANK + B_A_RANK]
    x_g = zs[:, 3 * w3 + B_DECAY_RANK + B_A_RANK:]
    w0, a0, k_k, k_a, r_k, ln_w, ln_b = (vec_ref[i:i + 1, :] for i in range(7))

    d_in = -(w0 + _dot(jnp.tanh(x_wa).astype(BF16), wup_ref[...]))
    softplus = jnp.maximum(d_in, 0.0) + jnp.log(1.0 + jnp.exp(-jnp.abs(d_in)))
    log_decay = -jnp.exp(-softplus - 0.5)

    if has_vres:
        low = _dot(v.astype(BF16), vdown_ref[...])
        mixed = _dot(low.astype(BF16), vup_ref[...])
        v = v + (vfirst_ref[...] - v) * _sigmoid(v0_ref[...] + mixed)
    else:
        vb_ref[...] = v
    gate_a = _sigmoid(a0 + _dot(x_wa.astype(BF16), aup_ref[...]))
    gate_out = _dot(_sigmoid(x_g).astype(BF16), gup_ref[...])

    hsum = hsum_ref[...]
    kk = k * k_k
    kk = kk / jnp.maximum(jnp.sqrt(_head_sums(kk * kk, hsum, split=True)), 1e-12)
    k = k * (1.0 + (gate_a - 1.0) * k_a)
    b_vec = kk * gate_a

    cum = _ones_matmul_left(ltri_ref[...], log_decay)
    total = _ones_matmul_left(lall_ref[...], log_decay)
    g_inc = jnp.exp(cum)
    g_inv = jnp.exp(-cum)
    g_tail = jnp.exp(total - cum)
    rt_s[...] = r * g_inc
    kt_s[...] = k * g_inv
    bt_s[...] = b_vec * g_inv
    at_s[...] = -kk * jnp.exp(cum - log_decay)
    ke_s[...] = k * g_tail
    be_s[...] = b_vec * g_tail
    gam_s[...] = jnp.exp(total)
    v_s[...] = v

    assert B_HEAD_DIM == CHUNK and 2 * B_HEAD_DIM == LANES
    pw = 2 * B_HEAD_DIM
    n_pairs = B_HEADS // 2
    lane_p = lax.broadcasted_iota(jnp.int32, (CHUNK, pw), 1)
    row_p = lax.broadcasted_iota(jnp.int32, (CHUNK, pw), 0)
    first_head = lane_p < B_HEAD_DIM
    eye = jnp.where(row_p == (lane_p & (B_HEAD_DIM - 1)), 1.0, 0.0)
    lane_2 = lax.broadcasted_iota(jnp.int32, (CHUNK, 2 * pw), 1)
    row_2 = lax.broadcasted_iota(jnp.int32, (CHUNK, 2 * pw), 0)
    strict = (lane_2 & (B_HEAD_DIM - 1)) < row_2
    incl = (lane_2 & (B_HEAD_DIM - 1)) <= row_2
    same_head = ((lax.broadcasted_iota(jnp.int32, (pw, pw), 0) < B_HEAD_DIM)
                 == (lax.broadcasted_iota(jnp.int32, (pw, pw), 1) < B_HEAD_DIM))
    zeros_pair = jnp.zeros((CHUNK, pw), F32)
    zeros_bd = jnp.zeros((pw, pw), BF16)

    def block_diag(x):
        return jnp.concatenate(
            [jnp.where(first_head, x, 0.0), jnp.where(first_head, 0.0, x)], axis=0).astype(BF16)

    def group_body(g, carry):
        pairs = [(ci, pr) for ci in range(RWKV_GROUP) for pr in range(n_pairs)]
        rows = [pl.ds(pl.multiple_of((g * RWKV_GROUP + ci) * CHUNK, CHUNK), CHUNK)
                for ci in range(RWKV_GROUP)]
        tile_of = lambda ref, p: ref[rows[p[0]], p[1] * pw:(p[1] + 1) * pw]

        a_t, r_t, v_p, a_sc, r_sc = {}, {}, {}, {}, {}
        for p in pairs:
            a_t[p] = tile_of(at_s, p)
            r_t[p] = tile_of(rt_s, p)
            v_p[p] = tile_of(v_s, p)
            lhs = jnp.concatenate([a_t[p], r_t[p]], axis=0).astype(BF16)
            rhs = jnp.concatenate([block_diag(tile_of(bt_s, p)), block_diag(tile_of(kt_s, p))],
                                  axis=0)
            scores = _dot_nt(lhs, rhs)
            a_sc[p] = jnp.where(strict, scores[:CHUNK], 0.0)
            r_sc[p] = jnp.where(incl, scores[CHUNK:], 0.0).astype(BF16)

        power, inv, akv, v_bd = {}, {}, {}, {}
        for p in pairs:
            n = a_sc[p][:, :pw]
            inv[p] = eye + n
            power[p] = _dot(n.astype(BF16), block_diag(n))
            v_bd[p] = block_diag(v_p[p])
            akv[p] = _dot(a_sc[p][:, pw:].astype(BF16), v_bd[p])
        for _ in range(int(math.log2(CHUNK)) - 2):
            for p in pairs:
                both = _dot(jnp.concatenate([inv[p], power[p]], axis=0).astype(BF16),
                            block_diag(power[p]))
                inv[p] = inv[p] + both[:CHUNK]
                power[p] = both[CHUNK:]
        for p in pairs:
            inv[p] = inv[p] + _dot(inv[p].astype(BF16), block_diag(power[p]))

        wu, g_mat, h_mat = {}, {}, {}
        for p in pairs:
            wu[p] = _dot(inv[p].astype(BF16),
                         jnp.concatenate([block_diag(a_t[p]), block_diag(akv[p])], axis=1))
        for p in pairs:
            z_bd = jnp.concatenate(
                [jnp.concatenate([block_diag(wu[p][:, :pw]), block_diag(wu[p][:, pw:])], axis=1),
                 jnp.concatenate([zeros_bd, v_bd[p]], axis=1)], axis=0)
            g_mat[p] = _dot(r_sc[p], z_bd)
        for p in pairs:
            z_rows = jnp.concatenate(
                [wu[p], jnp.concatenate([zeros_pair, v_p[p]], axis=1)], axis=0).astype(BF16)
            bk_e = jnp.concatenate([tile_of(be_s, p), tile_of(ke_s, p)], axis=0).astype(BF16)
            h_mat[p] = _dot_tn(z_rows, bk_e)

        state = [state_ref[pr] for pr in range(n_pairs)]
        y_rows = []
        for ci in range(RWKV_GROUP):
            y_pairs = []
            for pr in range(n_pairs):
                p = (ci, pr)
                r_eff = (r_t[p] + g_mat[p][:, :pw]).astype(BF16)
                y_pairs.append(_dot_nt(r_eff, block_diag(state[pr])) + g_mat[p][:, pw:])
                trans = jnp.where(same_head, h_mat[p][:pw], 0.0).astype(BF16)
                q_full = h_mat[p][pw:]
                gam = gam_s[pl.ds(pl.multiple_of((g * RWKV_GROUP + ci) * CHUNK, CHUNK), 1),
                            pr * pw:(pr + 1) * pw]
                state[pr] = (state[pr] * gam + _dot(state[pr].astype(BF16), trans)
                             + jnp.where(first_head, q_full[:B_HEAD_DIM], q_full[B_HEAD_DIM:]))
            y_rows.append(jnp.concatenate(y_pairs, axis=1))
        for ci in range(RWKV_GROUP):
            y_s[rows[ci], :] = y_rows[ci]
        for pr in range(n_pairs):
            state_ref[pr] = state[pr]
        return carry

    lax.fori_loop(0, tile // (CHUNK * RWKV_GROUP), group_body, 0)

    y = y_s[...]
    inv_n = 1.0 / B_HEAD_DIM
    mean = _head_sums(y, hsum, split=True) * inv_n
    yc = y - mean
    var = _head_sums(yc * yc, hsum, split=False) * inv_n
    yn = yc * lax.rsqrt(var + B_LN_EPS) * ln_w + ln_b
    bonus = _head_sums(r * k * r_k, hsum, split=False) * v
    ob_ref[...] = ((yn + bonus) * gate_out).astype(BF16)


def _block_ones(n, block, lower):
    i = jnp.arange(n)
    same = (i[:, None] // block) == (i[None, :] // block)
    if lower:
        same = same & (i[None, :] <= i[:, None])
    return same.astype(BF16)


def _pad_rows(w, lo, total):
    return jnp.zeros((total, w.shape[1]), w.dtype).at[lo:lo + w.shape[0]].set(w)


def _rwkv(zb, mu, vec, w_up, a_up, g_up, v_first, vres):
    b, s, _ = zb.shape
    tile = _seq_tile(s)
    lora_in = B_DECAY_RANK + B_A_RANK
    wup_p = _pad_rows(w_up, 0, lora_in).astype(BF16)
    aup_p = _pad_rows(a_up, B_DECAY_RANK, lora_in).astype(BF16)
    consts = [mu.reshape(1, B_COLS), vec, wup_p, aup_p, g_up.astype(BF16),
              _block_ones(MXU_TILE, B_HEAD_DIM, False), _block_ones(tile, CHUNK, True),
              _block_ones(tile, CHUNK, False)]
    full = lambda a: _resident(a.shape, lambda bi, t: (0,) * a.ndim)
    seq = lambda width: pl.BlockSpec((None, tile, width), lambda bi, t: (bi, t, 0))
    in_specs = [seq(B_COLS)] + [full(a) for a in consts]
    args = [zb] + consts
    out_shape = [jax.ShapeDtypeStruct((b, s, B_WIDTH), BF16)]
    out_specs = [seq(B_WIDTH)]
    if vres is not None:
        v0, v_down, v_up = vres
        vdown_p = jnp.zeros((B_WIDTH, LANES), F32).at[:, :B_VRES_RANK].set(v_down).astype(BF16)
        vup_p = _pad_rows(v_up, 0, LANES).astype(BF16)
        extra = [v0.reshape(1, B_WIDTH), vdown_p, vup_p]
        in_specs += [seq(B_WIDTH)] + [full(a) for a in extra]
        args += [v_first] + extra
    else:
        out_shape.append(jax.ShapeDtypeStruct((b, s, B_WIDTH), F32))
        out_specs.append(seq(B_WIDTH))
    wide = lambda: pltpu.VMEM((tile, B_WIDTH), F32)
    kern = functools.partial(_rwkv_kernel, tile=tile, has_vres=vres is not None)
    outs = pl.pallas_call(
        kern,
        out_shape=tuple(out_shape),
        grid=(b, s // tile),
        in_specs=in_specs,
        out_specs=tuple(out_specs),
        scratch_shapes=[pltpu.VMEM((B_HEADS // 2, B_HEAD_DIM, 2 * B_HEAD_DIM), F32),
                        pltpu.VMEM((1, B_COLS), F32)] + [wide() for _ in range(9)],
        compiler_params=pltpu.CompilerParams(
            dimension_semantics=("parallel", "arbitrary"), vmem_limit_bytes=VMEM_LIMIT),
        name="rwkv7",
    )(*args)
    return outs if vres is None else (outs[0], None)


def _retention_kernel(mask_ref, qdec_ref, kdec_ref, cdec_ref, q_ref, k_ref, v_ref, g_ref,
                      o_ref, state_ref):
    @pl.when(pl.program_id(1) == 0)
    def _():
        state_ref[...] = jnp.zeros_like(state_ref)

    heads = range(C_HEADS)
    qk_cols = lambda h: slice(h * C_QK_DIM, (h + 1) * C_QK_DIM)
    v_cols = lambda h: slice(h * C_V_DIM, (h + 1) * C_V_DIM)
    q = [q_ref[:, qk_cols(h)] for h in heads]
    k = [k_ref[:, qk_cols(h)] for h in heads]
    v = [v_ref[:, v_cols(h)] for h in heads]
    state = [state_ref[h] for h in heads]
    scores = [(_dot_nt(q[h], k[h]) * mask_ref[h]).astype(BF16) for h in heads]
    q_dec = [(q[h].astype(F32) * qdec_ref[h]).astype(BF16) for h in heads]
    y = [_dot(scores[h], v[h]) + _dot(q_dec[h], state[h].astype(BF16)) for h in heads]
    k_dec = [(k[h].astype(F32) * kdec_ref[h]).astype(BF16) for h in heads]
    for h in heads:
        state_ref[h] = state[h] * cdec_ref[h] + _dot_tn(k_dec[h], v[h])
    for h in heads:
        yn = y[h] * lax.rsqrt(jnp.mean(y[h] * y[h], axis=-1, keepdims=True) + NORM_EPS)
        g = g_ref[:, v_cols(h)].astype(F32)
        o_ref[:, v_cols(h)] = (g * _sigmoid(g) * yn).astype(BF16)


def _retention_tables(tile):
    log_g = jnp.log(1.0 - 2.0 ** (-5.0 - jnp.arange(C_HEADS, dtype=F32)))[:, None, None]
    i = jnp.arange(tile)
    dist = jnp.abs(i[:, None] - i[None, :]).astype(F32)
    visible = (i[None, :] // CHUNK) <= (i[:, None] // CHUNK)
    scale = C_QK_DIM ** -0.5
    mask = jnp.where(visible[None], jnp.exp(log_g * dist[None]), 0.0) * scale
    idx = i.astype(F32)[None, :, None]
    qdec = jnp.broadcast_to(jnp.exp(log_g * (idx + 1.0)), (C_HEADS, tile, C_QK_DIM))
    kdec = jnp.broadcast_to(jnp.exp(log_g * (tile - 1.0 - idx)) * scale, (C_HEADS, tile, C_QK_DIM))
    cdec = jnp.broadcast_to(jnp.exp(log_g * tile), (C_HEADS, 1, C_V_DIM))
    return mask, qdec, kdec, cdec


def _retention(qk, v, gate):
    b, s, _ = v.shape
    tile = _seq_tile(s)
    mask, qdec, kdec, cdec = _retention_tables(tile)
    return pl.pallas_call(
        _retention_kernel,
        out_shape=jax.ShapeDtypeStruct((b, s, C_V_WIDTH), BF16),
        grid=(b, s // tile),
        in_specs=[
            _resident(mask.shape, lambda bi, t: (0, 0, 0)),
            _resident(qdec.shape, lambda bi, t: (0, 0, 0)),
            _resident(kdec.shape, lambda bi, t: (0, 0, 0)),
            _resident(cdec.shape, lambda bi, t: (0, 0, 0)),
            pl.BlockSpec((None, tile, C_QK_WIDTH), lambda bi, t: (bi, t, 0)),
            pl.BlockSpec((None, tile, C_QK_WIDTH), lambda bi, t: (bi, t, 1)),
            pl.BlockSpec((None, tile, C_V_WIDTH), lambda bi, t: (bi, t, 0)),
            pl.BlockSpec((None, tile, C_V_WIDTH), lambda bi, t: (bi, t, 0)),
        ],
        out_specs=pl.BlockSpec((None, tile, C_V_WIDTH), lambda bi, t: (bi, t, 0)),
        scratch_shapes=[pltpu.VMEM((C_HEADS, C_QK_DIM, C_V_DIM), F32)],
        compiler_params=pltpu.CompilerParams(
            dimension_semantics=("parallel", "arbitrary"),
            vmem_limit_bytes=VMEM_LIMIT),
        name="retention",
    )(mask, qdec, kdec, cdec, qk, qk, v, gate)


def kernel(x, norms, ffn_wgu, ffn_wd, even_w_in, even_w_out, diff_lam, diff_subln, rwkv_mu, rwkv_vec, rwkv_w_up, rwkv_a_up, rwkv_g_up, rwkv_v0, rwkv_v_down, rwkv_v_up, odd_w_in, odd_w_out):
    b, s, d = x.shape
    m = b * s
    wgu = ffn_wgu.astype(BF16)
    wd = ffn_wd.astype(BF16)
    even_in = even_w_in.astype(BF16)
    even_out = even_w_out.astype(BF16)
    odd_in = odd_w_in.astype(BF16)
    odd_out = odd_w_out.astype(BF16)
    subln = diff_subln.reshape(-1, 1, A_V_DIM)

    x2 = x.reshape(m, d)
    v_first = None
    for i in range(DEPTH):
        j = i // 2
        x2 = _ffn(x2, norms, wgu, wd, i, 0)
        if i % 2 == 0:
            lam_init = 0.8 - 0.6 * math.exp(-0.3 * i)
            za, zb = _inproj_even(x2, norms, even_in, i, j)
            oa = _diff_attention(za.reshape(b, s, A_COLS), diff_lam, subln, lam_init, j)
            vres = None if j == 0 else (rwkv_v0[j - 1], rwkv_v_down[j - 1], rwkv_v_up[j - 1])
            ob, v_b = _rwkv(zb.reshape(b, s, B_COLS), rwkv_mu[j], rwkv_vec[j], rwkv_w_up[j],
                            rwkv_a_up[j], rwkv_g_up[j], v_first, vres)
            if j == 0:
                v_first = v_b
            parts, w_out = [oa.reshape(m, A_WIDTH), ob.reshape(m, B_WIDTH)], even_out
        else:
            qk, v, gate = _inproj_odd(x2, norms, odd_in, i, j)
            y = _retention(qk.reshape(b, s, -1), v.reshape(b, s, -1), gate.reshape(b, s, -1))
            parts, w_out = [y.reshape(m, C_V_WIDTH)], odd_out
        x2 = _ffn(x2, norms, wgu, wd, i, 1, parts, w_out, j)
    return x2.reshape(b, s, d)
```

```python
import functools
import math

import jax
import jax.numpy as jnp
from jax import lax
from jax.experimental import pallas as pl
from jax.experimental.pallas import tpu as pltpu

F32 = jnp.float32
BF16 = jnp.bfloat16

D_MODEL = 1024
DEPTH = 4
CHUNK = 64
D_FF = 2816
NORM_EPS = 1e-6

A_HEADS = 4
A_QK_DIM = 64
A_V_DIM = 2 * A_QK_DIM
A_WIDTH = A_HEADS * A_V_DIM
A_COLS = 3 * A_WIDTH

B_HEAD_DIM = 64
B_WIDTH = D_MODEL - A_WIDTH
B_HEADS = B_WIDTH // B_HEAD_DIM
B_DECAY_RANK = 64
B_A_RANK = 64
B_GATE_RANK = 128
B_VRES_RANK = 32
B_LN_EPS = 64e-5
B_COLS = 3 * B_WIDTH + B_DECAY_RANK + B_A_RANK + B_GATE_RANK
EVEN_IN = A_COLS + B_COLS

C_HEADS = 4
C_QK_DIM = D_MODEL // C_HEADS
C_V_DIM = 2 * C_QK_DIM
C_QK_WIDTH = C_HEADS * C_QK_DIM
C_V_WIDTH = C_HEADS * C_V_DIM
ODD_IN = 2 * C_QK_WIDTH + 2 * C_V_WIDTH

LANES = 128
SUBLANES = 8
MXU_TILE = 256
ROW_TILE = 512
FFN_ROW_TILE = 1024
FFN_POST_ROW_TILE = 512
FFN_SUBTILES = 2
FF_TILE = 256
SEQ_TILE = 256
ATTN_Q_TILE = 512
ATTN_K_TILE = 256
ATTN_GROUP = 8
ATTN_MAX_STRIP = 64
RWKV_GROUP = 4
VMEM_LIMIT = 52 * 1024 * 1024
ATTN_VMEM_LIMIT = 62 * 1024 * 1024
NEG_BIG = -1e30


def _dot(a, b):
    return jnp.dot(a, b, preferred_element_type=F32)


def _dot_nt(a, b):
    return lax.dot_general(a, b, (((1,), (1,)), ((), ())), preferred_element_type=F32)


def _dot_tn(a, b):
    return lax.dot_general(a, b, (((0,), (0,)), ((), ())), preferred_element_type=F32)


def _split2(x):
    hi = x.astype(BF16)
    lo = (x - hi.astype(F32)).astype(BF16)
    return hi, lo


def _head_sums(x, ones_block, split):
    width = ones_block.shape[0]
    out = []
    for lo_col in range(0, x.shape[1], width):
        xs = x[:, lo_col:lo_col + width]
        if split:
            hi, lo = _split2(xs)
            out.append(_dot(hi, ones_block) + _dot(lo, ones_block))
        else:
            out.append(_dot(xs.astype(BF16), ones_block))
    return jnp.concatenate(out, axis=1)


def _ones_matmul_left(m, x):
    hi, lo = _split2(x)
    return _dot(m, hi) + _dot(m, lo)


def _rms(x, g, eps=NORM_EPS):
    return x * lax.rsqrt(jnp.mean(x * x, axis=-1, keepdims=True) + eps) * g


def _sigmoid(x):
    return 1.0 / (1.0 + jnp.exp(-x))


def _row_tile(m):
    return ROW_TILE if m % ROW_TILE == 0 else m


def _seq_tile(s):
    return SEQ_TILE if s % SEQ_TILE == 0 else s


def _resident(block_shape, index_map):
    return pl.BlockSpec(block_shape, index_map, pipeline_mode=pl.Buffered(1))


def _ffn_kernel(*refs, g_pre, g_post, n_parts):
    x_ref, g_ref, wgu_ref, wd_ref = refs[:4]
    part_refs = refs[4:4 + n_parts]
    o_ref, act_ref = refs[-2:]
    sub = x_ref.shape[0] // FFN_SUBTILES
    row_sets = [slice(i * sub, (i + 1) * sub) for i in range(FFN_SUBTILES)]
    hs = []
    for rs in row_sets:
        x = x_ref[rs, :]
        if n_parts:
            wout_ref = refs[4 + n_parts]
            mix = None
            lo = 0
            for p in part_refs:
                width = p.shape[-1]
                term = _dot(p[rs, :], wout_ref[lo:lo + width, :])
                mix = term if mix is None else mix + term
                lo += width
            x = x + _rms(mix, g_ref[3:4, :])
        o_ref[rs, :] = x
        hs.append(_rms(x, g_ref[g_pre:g_pre + 1, :]).astype(BF16))
    for j in range(D_FF // FF_TILE):
        lo = j * FF_TILE
        for rs, h in zip(row_sets, hs):
            gate = _dot(h, wgu_ref[:, lo:lo + FF_TILE])
            up = _dot(h, wgu_ref[:, D_FF + lo:D_FF + lo + FF_TILE])
            act_ref[rs, lo:lo + FF_TILE] = (gate * _sigmoid(gate) * up).astype(BF16)
    ys = [_dot(act_ref[rs, :], wd_ref[...]) for rs in row_sets]
    for rs, y in zip(row_sets, ys):
        o_ref[rs, :] = o_ref[rs, :] + 0.5 * _rms(y, g_ref[g_post:g_post + 1, :])


def _ffn(x2, norms, wgu, wd, layer, which, parts=(), w_out=None, j=0):
    m, d = x2.shape
    tile = FFN_POST_ROW_TILE if parts else FFN_ROW_TILE
    tm = tile if m % tile == 0 else m
    kern = functools.partial(_ffn_kernel, g_pre=4 * which, g_post=4 * which + 1,
                             n_parts=len(parts))
    in_specs = [
        pl.BlockSpec((tm, d), lambda i: (i, 0)),
        _resident((None, 6, d), lambda i: (layer, 0, 0)),
        _resident((None, None, d, 2 * D_FF), lambda i: (layer, which, 0, 0)),
        _resident((None, None, D_FF, d), lambda i: (layer, which, 0, 0)),
    ]
    args = [x2, norms, wgu, wd]
    if parts:
        k_total = sum(p.shape[-1] for p in parts)
        in_specs += [pl.BlockSpec((tm, p.shape[-1]), lambda i: (i, 0)) for p in parts]
        in_specs.append(_resident((None, k_total, d), lambda i: (j, 0, 0)))
        args += list(parts) + [w_out]
    return pl.pallas_call(
        kern,
        out_shape=jax.ShapeDtypeStruct((m, d), F32),
        grid=(m // tm,),
        in_specs=in_specs,
        out_specs=pl.BlockSpec((tm, d), lambda i: (i, 0)),
        scratch_shapes=[pltpu.VMEM((tm, D_FF), BF16)],
        compiler_params=pltpu.CompilerParams(
            dimension_semantics=("parallel",), vmem_limit_bytes=VMEM_LIMIT),
        name="ffn_post" if parts else "ffn",
    )(*args)


def _inproj_even_kernel(x_ref, g_ref, w_ref, za_ref, zb_ref):
    h = _rms(x_ref[...], g_ref[2:3, :]).astype(BF16)
    za_ref[...] = _dot(h, w_ref[:, :A_COLS]).astype(BF16)
    zb_ref[...] = _dot(h, w_ref[:, A_COLS:])


def _inproj_even(x2, norms, w_in, layer, j):
    m, d = x2.shape
    tm = _row_tile(m)
    return pl.pallas_call(
        _inproj_even_kernel,
        out_shape=(jax.ShapeDtypeStruct((m, A_COLS), BF16),
                   jax.ShapeDtypeStruct((m, B_COLS), F32)),
        grid=(m // tm,),
        in_specs=[
            pl.BlockSpec((tm, d), lambda i: (i, 0)),
            _resident((None, 6, d), lambda i: (layer, 0, 0)),
            _resident((None, d, EVEN_IN), lambda i: (j, 0, 0)),
        ],
        out_specs=(pl.BlockSpec((tm, A_COLS), lambda i: (i, 0)),
                   pl.BlockSpec((tm, B_COLS), lambda i: (i, 0))),
        compiler_params=pltpu.CompilerParams(
            dimension_semantics=("parallel",), vmem_limit_bytes=VMEM_LIMIT),
        name="inproj_even",
    )(x2, norms, w_in)


def _inproj_odd_kernel(x_ref, g_ref, w_ref, qk_ref, v_ref, gate_ref):
    h = _rms(x_ref[...], g_ref[2:3, :]).astype(BF16)
    qk_ref[...] = _dot(h, w_ref[:, :2 * C_QK_WIDTH]).astype(BF16)
    v_ref[...] = _dot(h, w_ref[:, 2 * C_QK_WIDTH:2 * C_QK_WIDTH + C_V_WIDTH]).astype(BF16)
    gate_ref[...] = _dot(h, w_ref[:, 2 * C_QK_WIDTH + C_V_WIDTH:]).astype(BF16)


def _inproj_odd(x2, norms, w_in, layer, j):
    m, d = x2.shape
    tm = _row_tile(m)
    return pl.pallas_call(
        _inproj_odd_kernel,
        out_shape=(jax.ShapeDtypeStruct((m, 2 * C_QK_WIDTH), BF16),
                   jax.ShapeDtypeStruct((m, C_V_WIDTH), BF16),
                   jax.ShapeDtypeStruct((m, C_V_WIDTH), BF16)),
        grid=(m // tm,),
        in_specs=[
            pl.BlockSpec((tm, d), lambda i: (i, 0)),
            _resident((None, 6, d), lambda i: (layer, 0, 0)),
            _resident((None, d, ODD_IN), lambda i: (j, 0, 0)),
        ],
        out_specs=(pl.BlockSpec((tm, 2 * C_QK_WIDTH), lambda i: (i, 0)),
                   pl.BlockSpec((tm, C_V_WIDTH), lambda i: (i, 0)),
                   pl.BlockSpec((tm, C_V_WIDTH), lambda i: (i, 0))),
        compiler_params=pltpu.CompilerParams(
            dimension_semantics=("parallel",), vmem_limit_bytes=VMEM_LIMIT),
        name="inproj_odd",
    )(x2, norms, w_in)


def _lane_tiles(x):
    return [x[:, i:i + LANES] for i in range(0, x.shape[1], LANES)]


def _attn_kernel(lam_ref, subln_ref, dbias_ref, q_ref, k_ref, v_ref, o_ref, *scratch,
                 tq, tk, lam_init):
    for qi in range(q_ref.shape[0] // tq):
        rows = pl.ds(qi * tq, tq)
        _attn_tile(qi, lam_ref, subln_ref, dbias_ref, q_ref.at[rows], k_ref, v_ref, o_ref.at[rows],
                   *scratch, tq=tq, tk=tk, lam_init=lam_init)


def _attn_tile(qi, lam_ref, subln_ref, dbias_ref, q_ref, k_ref, v_ref, o_ref,
               s_ref, m_ref, acc_ref, *, tq, tk, lam_init):
    head = pl.program_id(1)
    n_diag = tq // tk
    first_diag = qi * n_diag
    log2e = 1.0 / math.log(2.0)
    slope2 = log2e * jnp.exp(
        (head + 1).astype(F32) * jnp.full((1, 1), -8.0 / A_HEADS * math.log(2.0), F32))

    lane = lax.broadcasted_iota(jnp.int32, (1, A_V_DIM), 1)
    q = q_ref[...] * jnp.asarray(A_QK_DIM ** -0.5, BF16)
    zero = jnp.zeros_like(q)
    q_maps = (jnp.where(lane < A_QK_DIM, q, zero), jnp.where(lane >= A_QK_DIM, q, zero))
    n_maps = len(q_maps)
    col_row = lax.broadcasted_iota(jnp.int32, (1, tk), 1)

    def lane_max(m, s):
        for t in _lane_tiles(s):
            m = t if m is None else jnp.maximum(m, t)
        return m

    def pass1(blocks):
        for kb, diag_slot in blocks:
            k = k_ref[pl.ds(kb * tk, tk), :]
            if diag_slot is None:
                bias = slope2 * (col_row + (kb * tk - qi * tq)).astype(F32)
            else:
                bias = dbias_ref[:, diag_slot * tk:(diag_slot + 1) * tk]
            for mi, qm in enumerate(q_maps):
                s_ref[mi, kb] = _dot_nt(qm, k) * log2e + bias

    def row_max(kbs):
        for mi in range(n_maps):
            for r0 in range(0, tq, ATTN_MAX_STRIP):
                rows = slice(r0, r0 + ATTN_MAX_STRIP)
                m = None
                for kb in kbs:
                    m = lane_max(m, s_ref[mi, kb, rows, :])
                m_ref[mi, rows, :] = jnp.broadcast_to(
                    jnp.max(m, axis=-1, keepdims=True), (ATTN_MAX_STRIP, LANES))

    ones_cols = jnp.ones((tk, A_V_DIM), BF16)

    def pass2(blocks, first):
        for mi in range(n_maps):
            m_row = m_ref[mi]
            acc_new = None if first else acc_ref[mi]
            for kb in blocks:
                p = jnp.concatenate(
                    [jnp.exp2(t - m_row) for t in _lane_tiles(s_ref[mi, kb])], axis=1)
                v_ext = jnp.concatenate([v_ref[pl.ds(kb * tk, tk), :], ones_cols], axis=1)
                pv = _dot(p.astype(BF16), v_ext)
                acc_new = pv if acc_new is None else acc_new + pv
            acc_ref[mi] = acc_new

    blocks = [(kb, None) for kb in range(first_diag)] + [(first_diag + d, d) for d in range(n_diag)]
    groups = [blocks[i:i + ATTN_GROUP] for i in range(0, len(blocks), ATTN_GROUP)]
    for group in groups:
        pass1(group)
    row_max([kb for kb, _ in blocks])
    for gi, group in enumerate(groups):
        pass2([kb for kb, _ in group], gi == 0)

    lv = lam_ref[...]
    lam = (jnp.exp(jnp.sum(lv[0:1] * lv[1:2], axis=-1, keepdims=True))
           - jnp.exp(jnp.sum(lv[2:3] * lv[3:4], axis=-1, keepdims=True)) + lam_init)
    outs = [acc_ref[mi][:, :A_V_DIM] * (1.0 / acc_ref[mi][:, A_V_DIM:]) for mi in range(n_maps)]
    o = outs[0] - lam * outs[1]
    o_ref[...] = (_rms(o, subln_ref[...]) * (1.0 - lam_init)).astype(BF16)


def _attn_diag_bias(tq):
    slopes = 2.0 ** (-8.0 / A_HEADS * jnp.arange(1, A_HEADS + 1, dtype=F32))
    i = jnp.arange(tq)
    rel = (i[:, None] - jnp.abs(i[:, None] - i[None, :])).astype(F32)
    visible = (i[None, :] // CHUNK) <= (i[:, None] // CHUNK)
    bias = slopes[:, None, None] * rel[None] / math.log(2.0)
    return jnp.where(visible[None], bias, NEG_BIG)


def _diff_attention(za, lam_vecs, subln, lam_init, j):
    b, s, _ = za.shape
    tq = ATTN_Q_TILE if s % ATTN_Q_TILE == 0 else s
    tk = ATTN_K_TILE if tq % ATTN_K_TILE == 0 else tq
    kern = functools.partial(_attn_kernel, tq=tq, tk=tk, lam_init=lam_init)
    n_maps = 2
    return pl.pallas_call(
        kern,
        out_shape=jax.ShapeDtypeStruct((b, s, A_WIDTH), BF16),
        grid=(b, A_HEADS),
        in_specs=[
            _resident((None, 4, A_QK_DIM), lambda bi, h: (j, 0, 0)),
            _resident((None, 1, A_V_DIM), lambda bi, h: (j, 0, 0)),
            pl.BlockSpec((None, tq, tq), lambda bi, h: (h, 0, 0)),
            pl.BlockSpec((None, s, A_V_DIM), lambda bi, h: (bi, 0, h)),
            pl.BlockSpec((None, s, A_V_DIM), lambda bi, h: (bi, 0, A_HEADS + h)),
            pl.BlockSpec((None, s, A_V_DIM), lambda bi, h: (bi, 0, 2 * A_HEADS + h)),
        ],
        out_specs=pl.BlockSpec((None, s, A_V_DIM), lambda bi, h: (bi, 0, h)),
        scratch_shapes=[pltpu.VMEM((n_maps, s // tk, tq, tk), F32),
                        pltpu.VMEM((n_maps, tq, LANES), F32),
                        pltpu.VMEM((n_maps, tq, 2 * A_V_DIM), F32)],
        compiler_params=pltpu.CompilerParams(
            dimension_semantics=("parallel", "parallel"),
            vmem_limit_bytes=ATTN_VMEM_LIMIT),
        name="diff_attention",
    )(lam_vecs, subln, _attn_diag_bias(tq), za, za, za)


def _rwkv_kernel(*refs, tile, has_vres):
    (zb_ref, mu_ref, vec_ref, wup_ref, aup_ref, gup_ref, hsum_ref, ltri_ref, lall_ref) = refs[:9]
    if has_vres:
        vfirst_ref, v0_ref, vdown_ref, vup_ref, ob_ref = refs[9:14]
        scratch = refs[14:]
    else:
        ob_ref, vb_ref = refs[9:11]
        scratch = refs[11:]
    (state_ref, prev_ref, at_s, rt_s, bt_s, kt_s, be_s, ke_s, v_s, gam_s, y_s) = scratch

    @pl.when(pl.program_id(1) == 0)
    def _():
        state_ref[...] = jnp.zeros_like(state_ref)
        prev_ref[...] = jnp.zeros_like(prev_ref)

    z = zb_ref[...]
    rolled = pltpu.roll(z, 1, axis=0)
    first_row = lax.broadcasted_iota(jnp.int32, (SUBLANES, z.shape[1]), 0) == 0
    z_prev = jnp.concatenate(
        [jnp.where(first_row, prev_ref[...], rolled[:SUBLANES]), rolled[SUBLANES:]], axis=0)
    prev_ref[...] = z[tile - 1:tile, :]
    zs = z + (z_prev - z) * mu_ref[...]

    w3 = B_WIDTH
    r = zs[:, 0:w3]
    k = zs[:, w3:2 * w3]
    v = zs[:, 2 * w3:3 * w3]
    x_wa = zs[:, 3 * w3:3 * w3 + B_DECAY_RANK + B_A_RANK]
    x_g = zs[:, 3 * w3 + B_DECAY_RANK + B_A_RANK:]
    w0, a0, k_k, k_a, r_k, ln_w, ln_b = (vec_ref[i:i + 1, :] for i in range(7))

    d_in = -(w0 + _dot(jnp.tanh(x_wa).astype(BF16), wup_ref[...]))
    softplus = jnp.maximum(d_in, 0.0) + jnp.log(1.0 + jnp.exp(-jnp.abs(d_in)))
    log_decay = -jnp.exp(-softplus - 0.5)

    if has_vres:
        low = _dot(v.astype(BF16), vdown_ref[...])
        mixed = _dot(low.astype(BF16), vup_ref[...])
        v = v + (vfirst_ref[...] - v) * _sigmoid(v0_ref[...] + mixed)
    else:
        vb_ref[...] = v
    gate_a = _sigmoid(a0 + _dot(x_wa.astype(BF16), aup_ref[...]))
    gate_out = _dot(_sigmoid(x_g).astype(BF16), gup_ref[...])

    hsum = hsum_ref[...]
    kk = k * k_k
    kk = kk / jnp.maximum(jnp.sqrt(_head_sums(kk * kk, hsum, split=True)), 1e-12)
    k = k * (1.0 + (gate_a - 1.0) * k_a)
    b_vec = kk * gate_a

    cum = _ones_matmul_left(ltri_ref[...], log_decay)
    total = _ones_matmul_left(lall_ref[...], log_decay)
    g_inc = jnp.exp(cum)
    g_inv = jnp.exp(-cum)
    g_tail = jnp.exp(total - cum)
    rt_s[...] = r * g_inc
    kt_s[...] = k * g_inv
    bt_s[...] = b_vec * g_inv
    at_s[...] = -kk * jnp.exp(cum - log_decay)
    ke_s[...] = k * g_tail
    be_s[...] = b_vec * g_tail
    gam_s[...] = jnp.exp(total)
    v_s[...] = v

    assert B_HEAD_DIM == CHUNK and 2 * B_HEAD_DIM == LANES
    pw = 2 * B_HEAD_DIM
    n_pairs = B_HEADS // 2
    lane_p = lax.broadcasted_iota(jnp.int32, (CHUNK, pw), 1)
    row_p = lax.broadcasted_iota(jnp.int32, (CHUNK, pw), 0)
    first_head = lane_p < B_HEAD_DIM
    eye = jnp.where(row_p == (lane_p & (B_HEAD_DIM - 1)), 1.0, 0.0)
    lane_2 = lax.broadcasted_iota(jnp.int32, (CHUNK, 2 * pw), 1)
    row_2 = lax.broadcasted_iota(jnp.int32, (CHUNK, 2 * pw), 0)
    strict = (lane_2 & (B_HEAD_DIM - 1)) < row_2
    incl = (lane_2 & (B_HEAD_DIM - 1)) <= row_2
    same_head = ((lax.broadcasted_iota(jnp.int32, (pw, pw), 0) < B_HEAD_DIM)
                 == (lax.broadcasted_iota(jnp.int32, (pw, pw), 1) < B_HEAD_DIM))
    zeros_pair = jnp.zeros((CHUNK, pw), F32)
    zeros_bd = jnp.zeros((pw, pw), BF16)

    def block_diag(x):
        return jnp.concatenate(
            [jnp.where(first_head, x, 0.0), jnp.where(first_head, 0.0, x)], axis=0).astype(BF16)

    def group_body(g, carry):
        pairs = [(ci, pr) for ci in range(RWKV_GROUP) for pr in range(n_pairs)]
        rows = [pl.ds(pl.multiple_of((g * RWKV_GROUP + ci) * CHUNK, CHUNK), CHUNK)
                for ci in range(RWKV_GROUP)]
        tile_of = lambda ref, p: ref[rows[p[0]], p[1] * pw:(p[1] + 1) * pw]

        a_t, r_t, v_p, a_sc, r_sc = {}, {}, {}, {}, {}
        for p in pairs:
            a_t[p] = tile_of(at_s, p)
            r_t[p] = tile_of(rt_s, p)
            v_p[p] = tile_of(v_s, p)
            lhs = jnp.concatenate([a_t[p], r_t[p]], axis=0).astype(BF16)
            rhs = jnp.concatenate([block_diag(tile_of(bt_s, p)), block_diag(tile_of(kt_s, p))],
                                  axis=0)
            scores = _dot_nt(lhs, rhs)
            a_sc[p] = jnp.where(strict, scores[:CHUNK], 0.0)
            r_sc[p] = jnp.where(incl, scores[CHUNK:], 0.0).astype(BF16)

        power, inv, akv, v_bd = {}, {}, {}, {}
        for p in pairs:
            n = a_sc[p][:, :pw]
            inv[p] = eye + n
            power[p] = _dot(n.astype(BF16), block_diag(n))
            v_bd[p] = block_diag(v_p[p])
            akv[p] = _dot(a_sc[p][:, pw:].astype(BF16), v_bd[p])
        for _ in range(int(math.log2(CHUNK)) - 2):
            for p in pairs:
                both = _dot(jnp.concatenate([inv[p], power[p]], axis=0).astype(BF16),
                            block_diag(power[p]))
                inv[p] = inv[p] + both[:CHUNK]
                power[p] = both[CHUNK:]
        for p in pairs:
            inv[p] = inv[p] + _dot(inv[p].astype(BF16), block_diag(power[p]))

        wu, g_mat, h_mat = {}, {}, {}
        for p in pairs:
            wu[p] = _dot(inv[p].astype(BF16),
                         jnp.concatenate([block_diag(a_t[p]), block_diag(akv[p])], axis=1))
        for p in pairs:
            z_bd = jnp.concatenate(
                [jnp.concatenate([block_diag(wu[p][:, :pw]), block_diag(wu[p][:, pw:])], axis=1),
                 jnp.concatenate([zeros_bd, v_bd[p]], axis=1)], axis=0)
            g_mat[p] = _dot(r_sc[p], z_bd)
        for p in pairs:
            z_rows = jnp.concatenate(
                [wu[p], jnp.concatenate([zeros_pair, v_p[p]], axis=1)], axis=0).astype(BF16)
            bk_e = jnp.concatenate([tile_of(be_s, p), tile_of(ke_s, p)], axis=0).astype(BF16)
            h_mat[p] = _dot_tn(z_rows, bk_e)

        state = [state_ref[pr] for pr in range(n_pairs)]
        y_rows = []
        for ci in range(RWKV_GROUP):
            y_pairs = []
            for pr in range(n_pairs):
                p = (ci, pr)
                r_eff = (r_t[p] + g_mat[p][:, :pw]).astype(BF16)
                y_pairs.append(_dot_nt(r_eff, block_diag(state[pr])) + g_mat[p][:, pw:])
                trans = jnp.where(same_head, h_mat[p][:pw], 0.0).astype(BF16)
                q_full = h_mat[p][pw:]
                gam = gam_s[pl.ds(pl.multiple_of((g * RWKV_GROUP + ci) * CHUNK, CHUNK), 1),
                            pr * pw:(pr + 1) * pw]
                state[pr] = (state[pr] * gam + _dot(state[pr].astype(BF16), trans)
                             + jnp.where(first_head, q_full[:B_HEAD_DIM], q_full[B_HEAD_DIM:]))
            y_rows.append(jnp.concatenate(y_pairs, axis=1))
        for ci in range(RWKV_GROUP):
            y_s[rows[ci], :] = y_rows[ci]
        for pr in range(n_pairs):
            state_ref[pr] = state[pr]
        return carry

    lax.fori_loop(0, tile // (CHUNK * RWKV_GROUP), group_body, 0)

    y = y_s[...]
    inv_n = 1.0 / B_HEAD_DIM
    mean = _head_sums(y, hsum, split=True) * inv_n
    yc = y - mean
    var = _head_sums(yc * yc, hsum, split=False) * inv_n
    yn = yc * lax.rsqrt(var + B_LN_EPS) * ln_w + ln_b
    bonus = _head_sums(r * k * r_k, hsum, split=False) * v
    ob_ref[...] = ((yn + bonus) * gate_out).astype(BF16)


def _block_ones(n, block, lower):
    i = jnp.arange(n)
    same = (i[:, None] // block) == (i[None, :] // block)
    if lower:
        same = same & (i[None, :] <= i[:, None])
    return same.astype(BF16)


def _pad_rows(w, lo, total):
    return jnp.zeros((total, w.shape[1]), w.dtype).at[lo:lo + w.shape[0]].set(w)


def _rwkv(zb, mu, vec, w_up, a_up, g_up, v_first, vres):
    b, s, _ = zb.shape
    tile = _seq_tile(s)
    lora_in = B_DECAY_RANK + B_A_RANK
    wup_p = _pad_rows(w_up, 0, lora_in).astype(BF16)
    aup_p = _pad_rows(a_up, B_DECAY_RANK, lora_in).astype(BF16)
    consts = [mu.reshape(1, B_COLS), vec, wup_p, aup_p, g_up.astype(BF16),
              _block_ones(MXU_TILE, B_HEAD_DIM, False), _block_ones(tile, CHUNK, True),
              _block_ones(tile, CHUNK, False)]
    full = lambda a: _resident(a.shape, lambda bi, t: (0,) * a.ndim)
    seq = lambda width: pl.BlockSpec((None, tile, width), lambda bi, t: (bi, t, 0))
    in_specs = [seq(B_COLS)] + [full(a) for a in consts]
    args = [zb] + consts
    out_shape = [jax.ShapeDtypeStruct((b, s, B_WIDTH), BF16)]
    out_specs = [seq(B_WIDTH)]
    if vres is not None:
        v0, v_down, v_up = vres
        vdown_p = jnp.zeros((B_WIDTH, LANES), F32).at[:, :B_VRES_RANK].set(v_down).astype(BF16)
        vup_p = _pad_rows(v_up, 0, LANES).astype(BF16)
        extra = [v0.reshape(1, B_WIDTH), vdown_p, vup_p]
        in_specs += [seq(B_WIDTH)] + [full(a) for a in extra]
        args += [v_first] + extra
    else:
        out_shape.append(jax.ShapeDtypeStruct((b, s, B_WIDTH), F32))
        out_specs.append(seq(B_WIDTH))
    wide = lambda: pltpu.VMEM((tile, B_WIDTH), F32)
    kern = functools.partial(_rwkv_kernel, tile=tile, has_vres=vres is not None)
    outs = pl.pallas_call(
        kern,
        out_shape=tuple(out_shape),
        grid=(b, s // tile),
        in_specs=in_specs,
        out_specs=tuple(out_specs),
        scratch_shapes=[pltpu.VMEM((B_HEADS // 2, B_HEAD_DIM, 2 * B_HEAD_DIM), F32),
                        pltpu.VMEM((1, B_COLS), F32)] + [wide() for _ in range(9)],
        compiler_params=pltpu.CompilerParams(
            dimension_semantics=("parallel", "arbitrary"), vmem_limit_bytes=VMEM_LIMIT),
        name="rwkv7",
    )(*args)
    return outs if vres is None else (outs[0], None)


def _retention_kernel(mask_ref, qdec_ref, kdec_ref, cdec_ref, q_ref, k_ref, v_ref, g_ref,
                      o_ref, state_ref):
    @pl.when(pl.program_id(1) == 0)
    def _():
        state_ref[...] = jnp.zeros_like(state_ref)

    heads = range(C_HEADS)
    qk_cols = lambda h: slice(h * C_QK_DIM, (h + 1) * C_QK_DIM)
    v_cols = lambda h: slice(h * C_V_DIM, (h + 1) * C_V_DIM)
    q = [q_ref[:, qk_cols(h)] for h in heads]
    k = [k_ref[:, qk_cols(h)] for h in heads]
    v = [v_ref[:, v_cols(h)] for h in heads]
    state = [state_ref[h] for h in heads]
    scores = [(_dot_nt(q[h], k[h]) * mask_ref[h]).astype(BF16) for h in heads]
    q_dec = [(q[h].astype(F32) * qdec_ref[h]).astype(BF16) for h in heads]
    y = [_dot(scores[h], v[h]) + _dot(q_dec[h], state[h].astype(BF16)) for h in heads]
    k_dec = [(k[h].astype(F32) * kdec_ref[h]).astype(BF16) for h in heads]
    for h in heads:
        state_ref[h] = state[h] * cdec_ref[h] + _dot_tn(k_dec[h], v[h])
    for h in heads:
        yn = y[h] * lax.rsqrt(jnp.mean(y[h] * y[h], axis=-1, keepdims=True) + NORM_EPS)
        g = g_ref[:, v_cols(h)].astype(F32)
        o_ref[:, v_cols(h)] = (g * _sigmoid(g) * yn).astype(BF16)


def _retention_tables(tile):
    log_g = jnp.log(1.0 - 2.0 ** (-5.0 - jnp.arange(C_HEADS, dtype=F32)))[:, None, None]
    i = jnp.arange(tile)
    dist = jnp.abs(i[:, None] - i[None, :]).astype(F32)
    visible = (i[None, :] // CHUNK) <= (i[:, None] // CHUNK)
    scale = C_QK_DIM ** -0.5
    mask = jnp.where(visible[None], jnp.exp(log_g * dist[None]), 0.0) * scale
    idx = i.astype(F32)[None, :, None]
    qdec = jnp.broadcast_to(jnp.exp(log_g * (idx + 1.0)), (C_HEADS, tile, C_QK_DIM))
    kdec = jnp.broadcast_to(jnp.exp(log_g * (tile - 1.0 - idx)) * scale, (C_HEADS, tile, C_QK_DIM))
    cdec = jnp.broadcast_to(jnp.exp(log_g * tile), (C_HEADS, 1, C_V_DIM))
    return mask, qdec, kdec, cdec


def _retention(qk, v, gate):
    b, s, _ = v.shape
    tile = _seq_tile(s)
    mask, qdec, kdec, cdec = _retention_tables(tile)
    return pl.pallas_call(
        _retention_kernel,
        out_shape=jax.ShapeDtypeStruct((b, s, C_V_WIDTH), BF16),
        grid=(b, s // tile),
        in_specs=[
            _resident(mask.shape, lambda bi, t: (0, 0, 0)),
            _resident(qdec.shape, lambda bi, t: (0, 0, 0)),
            _resident(kdec.shape, lambda bi, t: (0, 0, 0)),
            _resident(cdec.shape, lambda bi, t: (0, 0, 0)),
            pl.BlockSpec((None, tile, C_QK_WIDTH), lambda bi, t: (bi, t, 0)),
            pl.BlockSpec((None, tile, C_QK_WIDTH), lambda bi, t: (bi, t, 1)),
            pl.BlockSpec((None, tile, C_V_WIDTH), lambda bi, t: (bi, t, 0)),
            pl.BlockSpec((None, tile, C_V_WIDTH), lambda bi, t: (bi, t, 0)),
        ],
        out_specs=pl.BlockSpec((None, tile, C_V_WIDTH), lambda bi, t: (bi, t, 0)),
        scratch_shapes=[pltpu.VMEM((C_HEADS, C_QK_DIM, C_V_DIM), F32)],
        compiler_params=pltpu.CompilerParams(
            dimension_semantics=("parallel", "arbitrary"),
            vmem_limit_bytes=VMEM_LIMIT),
        name="retention",
    )(mask, qdec, kdec, cdec, qk, qk, v, gate)


def kernel(x, norms, ffn_wgu, ffn_wd, even_w_in, even_w_out, diff_lam, diff_subln, rwkv_mu, rwkv_vec, rwkv_w_up, rwkv_a_up, rwkv_g_up, rwkv_v0, rwkv_v_down, rwkv_v_up, odd_w_in, odd_w_out):
    b, s, d = x.shape
    m = b * s
    wgu = ffn_wgu.astype(BF16)
    wd = ffn_wd.astype(BF16)
    even_in = even_w_in.astype(BF16)
    even_out = even_w_out.astype(BF16)
    odd_in = odd_w_in.astype(BF16)
    odd_out = odd_w_out.astype(BF16)
    subln = diff_subln.reshape(-1, 1, A_V_DIM)

    x2 = x.reshape(m, d)
    v_first = None
    for i in range(DEPTH):
        j = i // 2
        x2 = _ffn(x2, norms, wgu, wd, i, 0)
        if i % 2 == 0:
            lam_init = 0.8 - 0.6 * math.exp(-0.3 * i)
            za, zb = _inproj_even(x2, norms, even_in, i, j)
            oa = _diff_attention(za.reshape(b, s, A_COLS), diff_lam, subln, lam_init, j)
            vres = None if j == 0 else (rwkv_v0[j - 1], rwkv_v_down[j - 1], rwkv_v_up[j - 1])
            ob, v_b = _rwkv(zb.reshape(b, s, B_COLS), rwkv_mu[j], rwkv_vec[j], rwkv_w_up[j],
                            rwkv_a_up[j], rwkv_g_up[j], v_first, vres)
            if j == 0:
                v_first = v_b
            parts, w_out = [oa.reshape(m, A_WIDTH), ob.reshape(m, B_WIDTH)], even_out
        else:
            qk, v, gate = _inproj_odd(x2, norms, odd_in, i, j)
            y = _retention(qk.reshape(b, s, -1), v.reshape(b, s, -1), gate.reshape(b, s, -1))
            parts, w_out = [y.reshape(m, C_V_WIDTH)], odd_out
        x2 = _ffn(x2, norms, wgu, wd, i, 1, parts, w_out, j)
    return x2.reshape(b, s, d)
```

```python
import functools
import math

import jax
import jax.numpy as jnp
from jax import lax
from jax.experimental import pallas as pl
from jax.experimental.pallas import tpu as pltpu

F32 = jnp.float32
BF16 = jnp.bfloat16

D_MODEL = 1024
DEPTH = 4
CHUNK = 64
D_FF = 2816
NORM_EPS = 1e-6

A_HEADS = 4
A_QK_DIM = 64
A_V_DIM = 2 * A_QK_DIM
A_WIDTH = A_HEADS * A_V_DIM
A_COLS = 3 * A_WIDTH

B_HEAD_DIM = 64
B_WIDTH = D_MODEL - A_WIDTH
B_HEADS = B_WIDTH // B_HEAD_DIM
B_DECAY_RANK = 64
B_A_RANK = 64
B_GATE_RANK = 128
B_VRES_RANK = 32
B_LN_EPS = 64e-5
B_COLS = 3 * B_WIDTH + B_DECAY_RANK + B_A_RANK + B_GATE_RANK
EVEN_IN = A_COLS + B_COLS

C_HEADS = 4
C_QK_DIM = D_MODEL // C_HEADS
C_V_DIM = 2 * C_QK_DIM
C_QK_WIDTH = C_HEADS * C_QK_DIM
C_V_WIDTH = C_HEADS * C_V_DIM
ODD_IN = 2 * C_QK_WIDTH + 2 * C_V_WIDTH

LANES = 128
SUBLANES = 8
MXU_TILE = 256
ROW_TILE = 512
FFN_ROW_TILE = 1024
FFN_POST_ROW_TILE = 512
FFN_SUBTILES = 2
FF_TILE = 256
SEQ_TILE = 256
ATTN_Q_TILE = 512
ATTN_K_TILE = 512
ATTN_GROUP = 8
ATTN_MAX_STRIP = 64
RWKV_GROUP = 4
VMEM_LIMIT = 52 * 1024 * 1024
ATTN_VMEM_LIMIT = 62 * 1024 * 1024
NEG_BIG = -1e30


def _dot(a, b):
    return jnp.dot(a, b, preferred_element_type=F32)


def _dot_nt(a, b):
    return lax.dot_general(a, b, (((1,), (1,)), ((), ())), preferred_element_type=F32)


def _dot_tn(a, b):
    return lax.dot_general(a, b, (((0,), (0,)), ((), ())), preferred_element_type=F32)


def _split2(x):
    hi = x.astype(BF16)
    lo = (x - hi.astype(F32)).astype(BF16)
    return hi, lo


def _head_sums(x, ones_block, split):
    width = ones_block.shape[0]
    out = []
    for lo_col in range(0, x.shape[1], width):
        xs = x[:, lo_col:lo_col + width]
        if split:
            hi, lo = _split2(xs)
            out.append(_dot(hi, ones_block) + _dot(lo, ones_block))
        else:
            out.append(_dot(xs.astype(BF16), ones_block))
    return jnp.concatenate(out, axis=1)


def _ones_matmul_left(m, x):
    hi, lo = _split2(x)
    return _dot(m, hi) + _dot(m, lo)


def _rms(x, g, eps=NORM_EPS):
    return x * lax.rsqrt(jnp.mean(x * x, axis=-1, keepdims=True) + eps) * g


def _sigmoid(x):
    return 1.0 / (1.0 + jnp.exp(-x))


def _row_tile(m):
    return ROW_TILE if m % ROW_TILE == 0 else m


def _seq_tile(s):
    return SEQ_TILE if s % SEQ_TILE == 0 else s


def _resident(block_shape, index_map):
    return pl.BlockSpec(block_shape, index_map, pipeline_mode=pl.Buffered(1))


def _ffn_kernel(*refs, g_pre, g_post, n_parts):
    x_ref, g_ref, wgu_ref, wd_ref = refs[:4]
    part_refs = refs[4:4 + n_parts]
    o_ref, act_ref = refs[-2:]
    sub = x_ref.shape[0] // FFN_SUBTILES
    row_sets = [slice(i * sub, (i + 1) * sub) for i in range(FFN_SUBTILES)]
    hs = []
    for rs in row_sets:
        x = x_ref[rs, :]
        if n_parts:
            wout_ref = refs[4 + n_parts]
            mix = None
            lo = 0
            for p in part_refs:
                width = p.shape[-1]
                term = _dot(p[rs, :], wout_ref[lo:lo + width, :])
                mix = term if mix is None else mix + term
                lo += width
            x = x + _rms(mix, g_ref[3:4, :])
        o_ref[rs, :] = x
        hs.append(_rms(x, g_ref[g_pre:g_pre + 1, :]).astype(BF16))
    for j in range(D_FF // FF_TILE):
        lo = j * FF_TILE
        for rs, h in zip(row_sets, hs):
            gate = _dot(h, wgu_ref[:, lo:lo + FF_TILE])
            up = _dot(h, wgu_ref[:, D_FF + lo:D_FF + lo + FF_TILE])
            act_ref[rs, lo:lo + FF_TILE] = (gate * _sigmoid(gate) * up).astype(BF16)
    ys = [_dot(act_ref[rs, :], wd_ref[...]) for rs in row_sets]
    for rs, y in zip(row_sets, ys):
        o_ref[rs, :] = o_ref[rs, :] + 0.5 * _rms(y, g_ref[g_post:g_post + 1, :])


def _ffn(x2, norms, wgu, wd, layer, which, parts=(), w_out=None, j=0):
    m, d = x2.shape
    tile = FFN_POST_ROW_TILE if parts else FFN_ROW_TILE
    tm = tile if m % tile == 0 else m
    kern = functools.partial(_ffn_kernel, g_pre=4 * which, g_post=4 * which + 1,
                             n_parts=len(parts))
    in_specs = [
        pl.BlockSpec((tm, d), lambda i: (i, 0)),
        _resident((None, 6, d), lambda i: (layer, 0, 0)),
        _resident((None, None, d, 2 * D_FF), lambda i: (layer, which, 0, 0)),
        _resident((None, None, D_FF, d), lambda i: (layer, which, 0, 0)),
    ]
    args = [x2, norms, wgu, wd]
    if parts:
        k_total = sum(p.shape[-1] for p in parts)
        in_specs += [pl.BlockSpec((tm, p.shape[-1]), lambda i: (i, 0)) for p in parts]
        in_specs.append(_resident((None, k_total, d), lambda i: (j, 0, 0)))
        args += list(parts) + [w_out]
    return pl.pallas_call(
        kern,
        out_shape=jax.ShapeDtypeStruct((m, d), F32),
        grid=(m // tm,),
        in_specs=in_specs,
        out_specs=pl.BlockSpec((tm, d), lambda i: (i, 0)),
        scratch_shapes=[pltpu.VMEM((tm, D_FF), BF16)],
        compiler_params=pltpu.CompilerParams(
            dimension_semantics=("parallel",), vmem_limit_bytes=VMEM_LIMIT),
        name="ffn_post" if parts else "ffn",
    )(*args)


def _inproj_even_kernel(x_ref, g_ref, w_ref, za_ref, zb_ref):
    h = _rms(x_ref[...], g_ref[2:3, :]).astype(BF16)
    za_ref[...] = _dot(h, w_ref[:, :A_COLS]).astype(BF16)
    zb_ref[...] = _dot(h, w_ref[:, A_COLS:])


def _inproj_even(x2, norms, w_in, layer, j):
    m, d = x2.shape
    tm = _row_tile(m)
    return pl.pallas_call(
        _inproj_even_kernel,
        out_shape=(jax.ShapeDtypeStruct((m, A_COLS), BF16),
                   jax.ShapeDtypeStruct((m, B_COLS), F32)),
        grid=(m // tm,),
        in_specs=[
            pl.BlockSpec((tm, d), lambda i: (i, 0)),
            _resident((None, 6, d), lambda i: (layer, 0, 0)),
            _resident((None, d, EVEN_IN), lambda i: (j, 0, 0)),
        ],
        out_specs=(pl.BlockSpec((tm, A_COLS), lambda i: (i, 0)),
                   pl.BlockSpec((tm, B_COLS), lambda i: (i, 0))),
        compiler_params=pltpu.CompilerParams(
            dimension_semantics=("parallel",), vmem_limit_bytes=VMEM_LIMIT),
        name="inproj_even",
    )(x2, norms, w_in)


def _inproj_odd_kernel(x_ref, g_ref, w_ref, qk_ref, v_ref, gate_ref):
    h = _rms(x_ref[...], g_ref[2:3, :]).astype(BF16)
    qk_ref[...] = _dot(h, w_ref[:, :2 * C_QK_WIDTH]).astype(BF16)
    v_ref[...] = _dot(h, w_ref[:, 2 * C_QK_WIDTH:2 * C_QK_WIDTH + C_V_WIDTH]).astype(BF16)
    gate_ref[...] = _dot(h, w_ref[:, 2 * C_QK_WIDTH + C_V_WIDTH:]).astype(BF16)


def _inproj_odd(x2, norms, w_in, layer, j):
    m, d = x2.shape
    tm = _row_tile(m)
    return pl.pallas_call(
        _inproj_odd_kernel,
        out_shape=(jax.ShapeDtypeStruct((m, 2 * C_QK_WIDTH), BF16),
                   jax.ShapeDtypeStruct((m, C_V_WIDTH), BF16),
                   jax.ShapeDtypeStruct((m, C_V_WIDTH), BF16)),
        grid=(m // tm,),
        in_specs=[
            pl.BlockSpec((tm, d), lambda i: (i, 0)),
            _resident((None, 6, d), lambda i: (layer, 0, 0)),
            _resident((None, d, ODD_IN), lambda i: (j, 0, 0)),
        ],
        out_specs=(pl.BlockSpec((tm, 2 * C_QK_WIDTH), lambda i: (i, 0)),
                   pl.BlockSpec((tm, C_V_WIDTH), lambda i: (i, 0)),
                   pl.BlockSpec((tm, C_V_WIDTH), lambda i: (i, 0))),
        compiler_params=pltpu.CompilerParams(
            dimension_semantics=("parallel",), vmem_limit_bytes=VMEM_LIMIT),
        name="inproj_odd",
    )(x2, norms, w_in)


def _lane_tiles(x):
    return [x[:, i:i + LANES] for i in range(0, x.shape[1], LANES)]


def _attn_kernel(lam_ref, subln_ref, dbias_ref, q_ref, k_ref, v_ref, o_ref, *scratch,
                 tq, tk, lam_init):
    for qi in range(q_ref.shape[0] // tq):
        rows = pl.ds(qi * tq, tq)
        _attn_tile(qi, lam_ref, subln_ref, dbias_ref, q_ref.at[rows], k_ref, v_ref, o_ref.at[rows],
                   *scratch, tq=tq, tk=tk, lam_init=lam_init)


def _attn_tile(qi, lam_ref, subln_ref, dbias_ref, q_ref, k_ref, v_ref, o_ref,
               s_ref, m_ref, acc_ref, *, tq, tk, lam_init):
    head = pl.program_id(1)
    n_diag = tq // tk
    first_diag = qi * n_diag
    log2e = 1.0 / math.log(2.0)
    slope2 = log2e * jnp.exp(
        (head + 1).astype(F32) * jnp.full((1, 1), -8.0 / A_HEADS * math.log(2.0), F32))

    lane = lax.broadcasted_iota(jnp.int32, (1, A_V_DIM), 1)
    q = q_ref[...] * jnp.asarray(A_QK_DIM ** -0.5, BF16)
    zero = jnp.zeros_like(q)
    q_maps = (jnp.where(lane < A_QK_DIM, q, zero), jnp.where(lane >= A_QK_DIM, q, zero))
    n_maps = len(q_maps)
    col_row = lax.broadcasted_iota(jnp.int32, (1, tk), 1)

    def lane_max(m, s):
        for t in _lane_tiles(s):
            m = t if m is None else jnp.maximum(m, t)
        return m

    def pass1(blocks):
        for kb, diag_slot in blocks:
            k = k_ref[pl.ds(kb * tk, tk), :]
            if diag_slot is None:
                bias = slope2 * (col_row + (kb * tk - qi * tq)).astype(F32)
            else:
                bias = dbias_ref[:, diag_slot * tk:(diag_slot + 1) * tk]
            for mi, qm in enumerate(q_maps):
                s_ref[mi, kb] = _dot_nt(qm, k) * log2e + bias

    def row_max(kbs):
        for mi in range(n_maps):
            for r0 in range(0, tq, ATTN_MAX_STRIP):
                rows = slice(r0, r0 + ATTN_MAX_STRIP)
                m = None
                for kb in kbs:
                    m = lane_max(m, s_ref[mi, kb, rows, :])
                m_ref[mi, rows, :] = jnp.broadcast_to(
                    jnp.max(m, axis=-1, keepdims=True), (ATTN_MAX_STRIP, LANES))

    ones_cols = jnp.ones((tk, A_V_DIM), BF16)

    def pass2(blocks, first):
        for mi in range(n_maps):
            m_row = m_ref[mi]
            acc_new = None if first else acc_ref[mi]
            for kb in blocks:
                p = jnp.concatenate(
                    [jnp.exp2(t - m_row) for t in _lane_tiles(s_ref[mi, kb])], axis=1)
                v_ext = jnp.concatenate([v_ref[pl.ds(kb * tk, tk), :], ones_cols], axis=1)
                pv = _dot(p.astype(BF16), v_ext)
                acc_new = pv if acc_new is None else acc_new + pv
            acc_ref[mi] = acc_new

    blocks = [(kb, None) for kb in range(first_diag)] + [(first_diag + d, d) for d in range(n_diag)]
    groups = [blocks[i:i + ATTN_GROUP] for i in range(0, len(blocks), ATTN_GROUP)]
    for group in groups:
        pass1(group)
    row_max([kb for kb, _ in blocks])
    for gi, group in enumerate(groups):
        pass2([kb for kb, _ in group], gi == 0)

    lv = lam_ref[...]
    lam = (jnp.exp(jnp.sum(lv[0:1] * lv[1:2], axis=-1, keepdims=True))
           - jnp.exp(jnp.sum(lv[2:3] * lv[3:4], axis=-1, keepdims=True)) + lam_init)
    outs = [acc_ref[mi][:, :A_V_DIM] * (1.0 / acc_ref[mi][:, A_V_DIM:]) for mi in range(n_maps)]
    o = outs[0] - lam * outs[1]
    o_ref[...] = (_rms(o, subln_ref[...]) * (1.0 - lam_init)).astype(BF16)


def _attn_diag_bias(tq):
    slopes = 2.0 ** (-8.0 / A_HEADS * jnp.arange(1, A_HEADS + 1, dtype=F32))
    i = jnp.arange(tq)
    rel = (i[:, None] - jnp.abs(i[:, None] - i[None, :])).astype(F32)
    visible = (i[None, :] // CHUNK) <= (i[:, None] // CHUNK)
    bias = slopes[:, None, None] * rel[None] / math.log(2.0)
    return jnp.where(visible[None], bias, NEG_BIG)


def _diff_attention(za, lam_vecs, subln, lam_init, j):
    b, s, _ = za.shape
    tq = ATTN_Q_TILE if s % ATTN_Q_TILE == 0 else s
    tk = ATTN_K_TILE if tq % ATTN_K_TILE == 0 else tq
    kern = functools.partial(_attn_kernel, tq=tq, tk=tk, lam_init=lam_init)
    n_maps = 2
    return pl.pallas_call(
        kern,
        out_shape=jax.ShapeDtypeStruct((b, s, A_WIDTH), BF16),
        grid=(b, A_HEADS),
        in_specs=[
            _resident((None, 4, A_QK_DIM), lambda bi, h: (j, 0, 0)),
            _resident((None, 1, A_V_DIM), lambda bi, h: (j, 0, 0)),
            pl.BlockSpec((None, tq, tq), lambda bi, h: (h, 0, 0)),
            pl.BlockSpec((None, s, A_V_DIM), lambda bi, h: (bi, 0, h)),
            pl.BlockSpec((None, s, A_V_DIM), lambda bi, h: (bi, 0, A_HEADS + h)),
            pl.BlockSpec((None, s, A_V_DIM), lambda bi, h: (bi, 0, 2 * A_HEADS + h)),
        ],
        out_specs=pl.BlockSpec((None, s, A_V_DIM), lambda bi, h: (bi, 0, h)),
        scratch_shapes=[pltpu.VMEM((n_maps, s // tk, tq, tk), F32),
                        pltpu.VMEM((n_maps, tq, LANES), F32),
                        pltpu.VMEM((n_maps, tq, 2 * A_V_DIM), F32)],
        compiler_params=pltpu.CompilerParams(
            dimension_semantics=("parallel", "parallel"),
            vmem_limit_bytes=ATTN_VMEM_LIMIT),
        name="diff_attention",
    )(lam_vecs, subln, _attn_diag_bias(tq), za, za, za)


def _rwkv_kernel(*refs, tile, has_vres):
    (zb_ref, mu_ref, vec_ref, wup_ref, aup_ref, gup_ref, hsum_ref, ltri_ref, lall_ref) = refs[:9]
    if has_vres:
        vfirst_ref, v0_ref, vdown_ref, vup_ref, ob_ref = refs[9:14]
        scratch = refs[14:]
    else:
        ob_ref, vb_ref = refs[9:11]
        scratch = refs[11:]
    (state_ref, prev_ref, at_s, rt_s, bt_s, kt_s, be_s, ke_s, v_s, gam_s, y_s) = scratch

    @pl.when(pl.program_id(1) == 0)
    def _():
        state_ref[...] = jnp.zeros_like(state_ref)
        prev_ref[...] = jnp.zeros_like(prev_ref)

    z = zb_ref[...]
    rolled = pltpu.roll(z, 1, axis=0)
    first_row = lax.broadcasted_iota(jnp.int32, (SUBLANES, z.shape[1]), 0) == 0
    z_prev = jnp.concatenate(
        [jnp.where(first_row, prev_ref[...], rolled[:SUBLANES]), rolled[SUBLANES:]], axis=0)
    prev_ref[...] = z[tile - 1:tile, :]
    zs = z + (z_prev - z) * mu_ref[...]

    w3 = B_WIDTH
    r = zs[:, 0:w3]
    k = zs[:, w3:2 * w3]
    v = zs[:, 2 * w3:3 * w3]
    x_wa = zs[:, 3 * w3:3 * w3 + B_DECAY_RANK + B_A_RANK]
    x_g = zs[:, 3 * w3 + B_DECAY_RANK + B_A_RANK:]
    w0, a0, k_k, k_a, r_k, ln_w, ln_b = (vec_ref[i:i + 1, :] for i in range(7))

    d_in = -(w0 + _dot(jnp.tanh(x_wa).astype(BF16), wup_ref[...]))
    softplus = jnp.maximum(d_in, 0.0) + jnp.log(1.0 + jnp.exp(-jnp.abs(d_in)))
    log_decay = -jnp.exp(-softplus - 0.5)

    if has_vres:
        low = _dot(v.astype(BF16), vdown_ref[...])
        mixed = _dot(low.astype(BF16), vup_ref[...])
        v = v + (vfirst_ref[...] - v) * _sigmoid(v0_ref[...] + mixed)
    else:
        vb_ref[...] = v
    gate_a = _sigmoid(a0 + _dot(x_wa.astype(BF16), aup_ref[...]))
    gate_out = _dot(_sigmoid(x_g).astype(BF16), gup_ref[...])

    hsum = hsum_ref[...]
    kk = k * k_k
    kk = kk / jnp.maximum(jnp.sqrt(_head_sums(kk * kk, hsum, split=True)), 1e-12)
    k = k * (1.0 + (gate_a - 1.0) * k_a)
    b_vec = kk * gate_a

    cum = _ones_matmul_left(ltri_ref[...], log_decay)
    total = _ones_matmul_left(lall_ref[...], log_decay)
    g_inc = jnp.exp(cum)
    g_inv = jnp.exp(-cum)
    g_tail = jnp.exp(total - cum)
    rt_s[...] = r * g_inc
    kt_s[...] = k * g_inv
    bt_s[...] = b_vec * g_inv
    at_s[...] = -kk * jnp.exp(cum - log_decay)
    ke_s[...] = k * g_tail
    be_s[...] = b_vec * g_tail
    gam_s[...] = jnp.exp(total)
    v_s[...] = v

    assert B_HEAD_DIM == CHUNK and 2 * B_HEAD_DIM == LANES
    pw = 2 * B_HEAD_DIM
    n_pairs = B_HEADS // 2
    lane_p = lax.broadcasted_iota(jnp.int32, (CHUNK, pw), 1)
    row_p = lax.broadcasted_iota(jnp.int32, (CHUNK, pw), 0)
    first_head = lane_p < B_HEAD_DIM
    eye = jnp.where(row_p == (lane_p & (B_HEAD_DIM - 1)), 1.0, 0.0)
    lane_2 = lax.broadcasted_iota(jnp.int32, (CHUNK, 2 * pw), 1)
    row_2 = lax.broadcasted_iota(jnp.int32, (CHUNK, 2 * pw), 0)
    strict = (lane_2 & (B_HEAD_DIM - 1)) < row_2
    incl = (lane_2 & (B_HEAD_DIM - 1)) <= row_2
    same_head = ((lax.broadcasted_iota(jnp.int32, (pw, pw), 0) < B_HEAD_DIM)
                 == (lax.broadcasted_iota(jnp.int32, (pw, pw), 1) < B_HEAD_DIM))
    zeros_pair = jnp.zeros((CHUNK, pw), F32)
    zeros_bd = jnp.zeros((pw, pw), BF16)

    def block_diag(x):
        return jnp.concatenate(
            [jnp.where(first_head, x, 0.0), jnp.where(first_head, 0.0, x)], axis=0).astype(BF16)

    def group_body(g, carry):
        pairs = [(ci, pr) for ci in range(RWKV_GROUP) for pr in range(n_pairs)]
        rows = [pl.ds(pl.multiple_of((g * RWKV_GROUP + ci) * CHUNK, CHUNK), CHUNK)
                for ci in range(RWKV_GROUP)]
        tile_of = lambda ref, p: ref[rows[p[0]], p[1] * pw:(p[1] + 1) * pw]

        a_t, r_t, v_p, a_sc, r_sc = {}, {}, {}, {}, {}
        for p in pairs:
            a_t[p] = tile_of(at_s, p)
            r_t[p] = tile_of(rt_s, p)
            v_p[p] = tile_of(v_s, p)
            lhs = jnp.concatenate([a_t[p], r_t[p]], axis=0).astype(BF16)
            rhs = jnp.concatenate([block_diag(tile_of(bt_s, p)), block_diag(tile_of(kt_s, p))],
                                  axis=0)
            scores = _dot_nt(lhs, rhs)
            a_sc[p] = jnp.where(strict, scores[:CHUNK], 0.0)
            r_sc[p] = jnp.where(incl, scores[CHUNK:], 0.0).astype(BF16)

        power, inv, akv, v_bd = {}, {}, {}, {}
        for p in pairs:
            n = a_sc[p][:, :pw]
            inv[p] = eye + n
            power[p] = _dot(n.astype(BF16), block_diag(n))
            v_bd[p] = block_diag(v_p[p])
            akv[p] = _dot(a_sc[p][:, pw:].astype(BF16), v_bd[p])
        for _ in range(int(math.log2(CHUNK)) - 2):
            for p in pairs:
                both = _dot(jnp.concatenate([inv[p], power[p]], axis=0).astype(BF16),
                            block_diag(power[p]))
                inv[p] = inv[p] + both[:CHUNK]
                power[p] = both[CHUNK:]
        for p in pairs:
            inv[p] = inv[p] + _dot(inv[p].astype(BF16), block_diag(power[p]))

        wu, g_mat, h_mat = {}, {}, {}
        for p in pairs:
            wu[p] = _dot(inv[p].astype(BF16),
                         jnp.concatenate([block_diag(a_t[p]), block_diag(akv[p])], axis=1))
        for p in pairs:
            z_bd = jnp.concatenate(
                [jnp.concatenate([block_diag(wu[p][:, :pw]), block_diag(wu[p][:, pw:])], axis=1),
                 jnp.concatenate([zeros_bd, v_bd[p]], axis=1)], axis=0)
            g_mat[p] = _dot(r_sc[p], z_bd)
        for p in pairs:
            z_rows = jnp.concatenate(
                [wu[p], jnp.concatenate([zeros_pair, v_p[p]], axis=1)], axis=0).astype(BF16)
            bk_e = jnp.concatenate([tile_of(be_s, p), tile_of(ke_s, p)], axis=0).astype(BF16)
            h_mat[p] = _dot_tn(z_rows, bk_e)

        state = [state_ref[pr] for pr in range(n_pairs)]
        y_rows = []
        for ci in range(RWKV_GROUP):
            y_pairs = []
            for pr in range(n_pairs):
                p = (ci, pr)
                r_eff = (r_t[p] + g_mat[p][:, :pw]).astype(BF16)
                y_pairs.append(_dot_nt(r_eff, block_diag(state[pr])) + g_mat[p][:, pw:])
                trans = jnp.where(same_head, h_mat[p][:pw], 0.0).astype(BF16)
                q_full = h_mat[p][pw:]
                gam = gam_s[pl.ds(pl.multiple_of((g * RWKV_GROUP + ci) * CHUNK, CHUNK), 1),
                            pr * pw:(pr + 1) * pw]
                state[pr] = (state[pr] * gam + _dot(state[pr].astype(BF16), trans)
                             + jnp.where(first_head, q_full[:B_HEAD_DIM], q_full[B_HEAD_DIM:]))
            y_rows.append(jnp.concatenate(y_pairs, axis=1))
        for ci in range(RWKV_GROUP):
            y_s[rows[ci], :] = y_rows[ci]
        for pr in range(n_pairs):
            state_ref[pr] = state[pr]
        return carry

    lax.fori_loop(0, tile // (CHUNK * RWKV_GROUP), group_body, 0)

    y = y_s[...]
    inv_n = 1.0 / B_HEAD_DIM
    mean = _head_sums(y, hsum, split=True) * inv_n
    yc = y - mean
    var = _head_sums(yc * yc, hsum, split=False) * inv_n
    yn = yc * lax.rsqrt(var + B_LN_EPS) * ln_w + ln_b
    bonus = _head_sums(r * k * r_k, hsum, split=False) * v
    ob_ref[...] = ((yn + bonus) * gate_out).astype(BF16)


def _block_ones(n, block, lower):
    i = jnp.arange(n)
    same = (i[:, None] // block) == (i[None, :] // block)
    if lower:
        same = same & (i[None, :] <= i[:, None])
    return same.astype(BF16)


def _pad_rows(w, lo, total):
    return jnp.zeros((total, w.shape[1]), w.dtype).at[lo:lo + w.shape[0]].set(w)


def _rwkv(zb, mu, vec, w_up, a_up, g_up, v_first, vres):
    b, s, _ = zb.shape
    tile = _seq_tile(s)
    lora_in = B_DECAY_RANK + B_A_RANK
    wup_p = _pad_rows(w_up, 0, lora_in).astype(BF16)
    aup_p = _pad_rows(a_up, B_DECAY_RANK, lora_in).astype(BF16)
    consts = [mu.reshape(1, B_COLS), vec, wup_p, aup_p, g_up.astype(BF16),
              _block_ones(MXU_TILE, B_HEAD_DIM, False), _block_ones(tile, CHUNK, True),
              _block_ones(tile, CHUNK, False)]
    full = lambda a: _resident(a.shape, lambda bi, t: (0,) * a.ndim)
    seq = lambda width: pl.BlockSpec((None, tile, width), lambda bi, t: (bi, t, 0))
    in_specs = [seq(B_COLS)] + [full(a) for a in consts]
    args = [zb] + consts
    out_shape = [jax.ShapeDtypeStruct((b, s, B_WIDTH), BF16)]
    out_specs = [seq(B_WIDTH)]
    if vres is not None:
        v0, v_down, v_up = vres
        vdown_p = jnp.zeros((B_WIDTH, LANES), F32).at[:, :B_VRES_RANK].set(v_down).astype(BF16)
        vup_p = _pad_rows(v_up, 0, LANES).astype(BF16)
        extra = [v0.reshape(1, B_WIDTH), vdown_p, vup_p]
        in_specs += [seq(B_WIDTH)] + [full(a) for a in extra]
        args += [v_first] + extra
    else:
        out_shape.append(jax.ShapeDtypeStruct((b, s, B_WIDTH), F32))
        out_specs.append(seq(B_WIDTH))
    wide = lambda: pltpu.VMEM((tile, B_WIDTH), F32)
    kern = functools.partial(_rwkv_kernel, tile=tile, has_vres=vres is not None)
    outs = pl.pallas_call(
        kern,
        out_shape=tuple(out_shape),
        grid=(b, s // tile),
        in_specs=in_specs,
        out_specs=tuple(out_specs),
        scratch_shapes=[pltpu.VMEM((B_HEADS // 2, B_HEAD_DIM, 2 * B_HEAD_DIM), F32),
                        pltpu.VMEM((1, B_COLS), F32)] + [wide() for _ in range(9)],
        compiler_params=pltpu.CompilerParams(
            dimension_semantics=("parallel", "arbitrary"), vmem_limit_bytes=VMEM_LIMIT),
        name="rwkv7",
    )(*args)
    return outs if vres is None else (outs[0], None)


def _retention_kernel(mask_ref, qdec_ref, kdec_ref, cdec_ref, q_ref, k_ref, v_ref, g_ref,
                      o_ref, state_ref):
    @pl.when(pl.program_id(1) == 0)
    def _():
        state_ref[...] = jnp.zeros_like(state_ref)

    heads = range(C_HEADS)
    qk_cols = lambda h: slice(h * C_QK_DIM, (h + 1) * C_QK_DIM)
    v_cols = lambda h: slice(h * C_V_DIM, (h + 1) * C_V_DIM)
    q = [q_ref[:, qk_cols(h)] for h in heads]
    k = [k_ref[:, qk_cols(h)] for h in heads]
    v = [v_ref[:, v_cols(h)] for h in heads]
    state = [state_ref[h] for h in heads]
    scores = [(_dot_nt(q[h], k[h]) * mask_ref[h]).astype(BF16) for h in heads]
    q_dec = [(q[h].astype(F32) * qdec_ref[h]).astype(BF16) for h in heads]
    y = [_dot(scores[h], v[h]) + _dot(q_dec[h], state[h].astype(BF16)) for h in heads]
    k_dec = [(k[h].astype(F32) * kdec_ref[h]).astype(BF16) for h in heads]
    for h in heads:
        state_ref[h] = state[h] * cdec_ref[h] + _dot_tn(k_dec[h], v[h])
    for h in heads:
        yn = y[h] * lax.rsqrt(jnp.mean(y[h] * y[h], axis=-1, keepdims=True) + NORM_EPS)
        g = g_ref[:, v_cols(h)].astype(F32)
        o_ref[:, v_cols(h)] = (g * _sigmoid(g) * yn).astype(BF16)


def _retention_tables(tile):
    log_g = jnp.log(1.0 - 2.0 ** (-5.0 - jnp.arange(C_HEADS, dtype=F32)))[:, None, None]
    i = jnp.arange(tile)
    dist = jnp.abs(i[:, None] - i[None, :]).astype(F32)
    visible = (i[None, :] // CHUNK) <= (i[:, None] // CHUNK)
    scale = C_QK_DIM ** -0.5
    mask = jnp.where(visible[None], jnp.exp(log_g * dist[None]), 0.0) * scale
    idx = i.astype(F32)[None, :, None]
    qdec = jnp.broadcast_to(jnp.exp(log_g * (idx + 1.0)), (C_HEADS, tile, C_QK_DIM))
    kdec = jnp.broadcast_to(jnp.exp(log_g * (tile - 1.0 - idx)) * scale, (C_HEADS, tile, C_QK_DIM))
    cdec = jnp.broadcast_to(jnp.exp(log_g * tile), (C_HEADS, 1, C_V_DIM))
    return mask, qdec, kdec, cdec


def _retention(qk, v, gate):
    b, s, _ = v.shape
    tile = _seq_tile(s)
    mask, qdec, kdec, cdec = _retention_tables(tile)
    return pl.pallas_call(
        _retention_kernel,
        out_shape=jax.ShapeDtypeStruct((b, s, C_V_WIDTH), BF16),
        grid=(b, s // tile),
        in_specs=[
            _resident(mask.shape, lambda bi, t: (0, 0, 0)),
            _resident(qdec.shape, lambda bi, t: (0, 0, 0)),
            _resident(kdec.shape, lambda bi, t: (0, 0, 0)),
            _resident(cdec.shape, lambda bi, t: (0, 0, 0)),
            pl.BlockSpec((None, tile, C_QK_WIDTH), lambda bi, t: (bi, t, 0)),
            pl.BlockSpec((None, tile, C_QK_WIDTH), lambda bi, t: (bi, t, 1)),
            pl.BlockSpec((None, tile, C_V_WIDTH), lambda bi, t: (bi, t, 0)),
            pl.BlockSpec((None, tile, C_V_WIDTH), lambda bi, t: (bi, t, 0)),
        ],
        out_specs=pl.BlockSpec((None, tile, C_V_WIDTH), lambda bi, t: (bi, t, 0)),
        scratch_shapes=[pltpu.VMEM((C_HEADS, C_QK_DIM, C_V_DIM), F32)],
        compiler_params=pltpu.CompilerParams(
            dimension_semantics=("parallel", "arbitrary"),
            vmem_limit_bytes=VMEM_LIMIT),
        name="retention",
    )(mask, qdec, kdec, cdec, qk, qk, v, gate)


def kernel(x, norms, ffn_wgu, ffn_wd, even_w_in, even_w_out, diff_lam, diff_subln, rwkv_mu, rwkv_vec, rwkv_w_up, rwkv_a_up, rwkv_g_up, rwkv_v0, rwkv_v_down, rwkv_v_up, odd_w_in, odd_w_out):
    b, s, d = x.shape
    m = b * s
    wgu = ffn_wgu.astype(BF16)
    wd = ffn_wd.astype(BF16)
    even_in = even_w_in.astype(BF16)
    even_out = even_w_out.astype(BF16)
    odd_in = odd_w_in.astype(BF16)
    odd_out = odd_w_out.astype(BF16)
    subln = diff_subln.reshape(-1, 1, A_V_DIM)

    x2 = x.reshape(m, d)
    v_first = None
    for i in range(DEPTH):
        j = i // 2
        x2 = _ffn(x2, norms, wgu, wd, i, 0)
        if i % 2 == 0:
            lam_init = 0.8 - 0.6 * math.exp(-0.3 * i)
            za, zb = _inproj_even(x2, norms, even_in, i, j)
            oa = _diff_attention(za.reshape(b, s, A_COLS), diff_lam, subln, lam_init, j)
            vres = None if j == 0 else (rwkv_v0[j - 1], rwkv_v_down[j - 1], rwkv_v_up[j - 1])
            ob, v_b = _rwkv(zb.reshape(b, s, B_COLS), rwkv_mu[j], rwkv_vec[j], rwkv_w_up[j],
                            rwkv_a_up[j], rwkv_g_up[j], v_first, vres)
            if j == 0:
                v_first = v_b
            parts, w_out = [oa.reshape(m, A_WIDTH), ob.reshape(m, B_WIDTH)], even_out
        else:
            qk, v, gate = _inproj_odd(x2, norms, odd_in, i, j)
            y = _retention(qk.reshape(b, s, -1), v.reshape(b, s, -1), gate.reshape(b, s, -1))
            parts, w_out = [y.reshape(m, C_V_WIDTH)], odd_out
        x2 = _ffn(x2, norms, wgu, wd, i, 1, parts, w_out, j)
    return x2.reshape(b, s, d)
```
